```python
import jax, jax.numpy as jnp
from jax import lax

D_MODEL = 1024
BATCH = 2
SEQ = 16384
DEPTH = 4

CHUNK = 64
Q_BLOCK = 128
HEAD_DIM = 64
ROPE_THETA = 10000.0
NORM_EPS = 1e-6
LRU_WIDTH = D_MODEL // 2
LRU_BLOCKS = 8
LRU_BLOCK_DIM = LRU_WIDTH // LRU_BLOCKS
LRU_CONV = 4
LRU_C = 8.0
DSA_HEADS = (D_MODEL // 2) // HEAD_DIM
IDX_HEADS = 8
IDX_DIM = 64
DSA_TOPK_MAX = 256
FOX_HEADS = (D_MODEL // 2) // HEAD_DIM
CONV_WIDTH = D_MODEL // 2
CONV_KERNEL = 31
FFN_HIDDEN = -(-8 * D_MODEL // (3 * 256)) * 256
N_EVEN = (DEPTH + 1) // 2
N_ODD = DEPTH // 2

EVEN_SPLITS = (LRU_WIDTH, LRU_WIDTH,
               DSA_HEADS * HEAD_DIM, DSA_HEADS * HEAD_DIM, DSA_HEADS * HEAD_DIM,
               IDX_HEADS * IDX_DIM, IDX_DIM, IDX_HEADS)
ODD_SPLITS = (FOX_HEADS * HEAD_DIM, FOX_HEADS * HEAD_DIM, FOX_HEADS * HEAD_DIM,
              FOX_HEADS, 2 * CONV_WIDTH)
EVEN_IN = sum(EVEN_SPLITS)
ODD_IN = sum(ODD_SPLITS)
EVEN_OUT = LRU_WIDTH + DSA_HEADS * HEAD_DIM
ODD_OUT = FOX_HEADS * HEAD_DIM + CONV_WIDTH

kernel_name = "hybrid_lru_dsa_fox_conformer_trunk"


def split_cols(z, sizes):
    offsets, acc = [], 0
    for s in sizes[:-1]:
        acc += s
        offsets.append(acc)
    return jnp.split(z, offsets, axis=-1)


def rms_norm(x, g):
    xf = x.astype(jnp.float32)
    y = xf * lax.rsqrt(jnp.mean(xf * xf, axis=-1, keepdims=True) + NORM_EPS)
    return (y * g.astype(jnp.float32)).astype(x.dtype)


def layer_norm(x, g, b):
    xf = x.astype(jnp.float32)
    mu = jnp.mean(xf, axis=-1, keepdims=True)
    var = jnp.mean(jnp.square(xf - mu), axis=-1, keepdims=True)
    y = (xf - mu) * lax.rsqrt(var + NORM_EPS)
    return (y * g.astype(jnp.float32) + b.astype(jnp.float32)).astype(x.dtype)


def rope(x, pos):
    d = x.shape[-1]
    inv = ROPE_THETA ** (-jnp.arange(0, d, 2, dtype=jnp.float32) / d)
    ang = pos[:, None] * inv[None, :]
    if x.ndim == 4:
        ang = ang[:, None, :]
    cos, sin = jnp.cos(ang), jnp.sin(ang)
    xf = x.astype(jnp.float32)
    x1, x2 = xf[..., : d // 2], xf[..., d // 2:]
    return jnp.concatenate([x1 * cos - x2 * sin, x1 * sin + x2 * cos], axis=-1).astype(x.dtype)


def causal_dwconv(x, w, b):
    K, C = w.shape
    y = lax.conv_general_dilated(x, w[:, None, :].astype(x.dtype), (1,), [(K - 1, 0)],
                                 dimension_numbers=('NWC', 'WIO', 'NWC'),
                                 feature_group_count=C)
    return y + b.astype(x.dtype)


def _lru_combine(c1, c2):
    a1, b1 = c1
    a2, b2 = c2
    return a1 * a2, a2 * b1 + b2


def rg_lru(x, w_r, b_r, w_i, b_i, lam):
    B, S, W = x.shape
    xf = x.astype(jnp.float32)
    xb = xf.reshape(B, S, LRU_BLOCKS, LRU_BLOCK_DIM)
    r = jax.nn.sigmoid(jnp.einsum('bsnd,nde->bsne', xb, w_r.astype(jnp.float32)).reshape(B, S, W)
                       + b_r.astype(jnp.float32))
    i = jax.nn.sigmoid(jnp.einsum('bsnd,nde->bsne', xb, w_i.astype(jnp.float32)).reshape(B, S, W)
                       + b_i.astype(jnp.float32))
    log_a = -LRU_C * r * jax.nn.softplus(-lam.astype(jnp.float32))
    a = jnp.exp(log_a)
    u = jnp.sqrt(-jnp.expm1(2.0 * log_a)) * (i * xf)
    _, h = lax.associative_scan(_lru_combine, (a, u), axis=1)
    return h.astype(x.dtype)


def dsa_attention(q, k, v, qi, ki, wi):
    B, S, H, Dh = q.shape
    topk = min(DSA_TOPK_MAX, S // 4)
    n_blk = S // Q_BLOCK
    key_chunk = jnp.arange(S) // CHUNK
    k_flat = k.reshape(B, S, H * Dh)
    v_flat = v.reshape(B, S, H * Dh)
    gather = jax.vmap(lambda arr, idx: jnp.take(arr, idx, axis=0))
    scale = Dh ** -0.5

    def block(blk):
        t0 = blk * Q_BLOCK
        qb = lax.dynamic_slice_in_dim(q, t0, Q_BLOCK, axis=1)
        qib = lax.dynamic_slice_in_dim(qi, t0, Q_BLOCK, axis=1)
        wib = lax.dynamic_slice_in_dim(wi, t0, Q_BLOCK, axis=1)
        q_chunk = (t0 + jnp.arange(Q_BLOCK)) // CHUNK
        rel = jax.nn.relu(jnp.einsum('bthd,bsd->bths', qib, ki).astype(jnp.float32))
        score = jnp.einsum('bth,bths->bts', wib.astype(jnp.float32), rel)
        vis = key_chunk[None, :] <= q_chunk[:, None]
        score = jnp.where(vis[None], score, -jnp.inf)
        _, idx = lax.top_k(score, topk)
        valid = (idx // CHUNK) <= q_chunk[None, :, None]
        flat = idx.reshape(B, Q_BLOCK * topk)
        ks = gather(k_flat, flat).reshape(B, Q_BLOCK, topk, H, Dh)
        vs = gather(v_flat, flat).reshape(B, Q_BLOCK, topk, H, Dh)
        logits = jnp.einsum('bthd,btkhd->bthk', qb, ks).astype(jnp.float32) * scale
        logits = jnp.where(valid[:, :, None, :], logits, -jnp.inf)
        p = jax.nn.softmax(logits, axis=-1)
        return jnp.einsum('bthk,btkhd->bthd', p.astype(v.dtype), vs)

    out = lax.map(block, jnp.arange(n_blk))
    return jnp.moveaxis(out, 0, 1).reshape(B, S, H * Dh)


def fox_attention(q, k, v, c):
    B, S, H, Dh = q.shape
    n_blk = S // Q_BLOCK
    cT = jnp.swapaxes(c, 1, 2)
    key_pos = jnp.arange(S)
    scale = Dh ** -0.5

    def block(blk):
        t0 = blk * Q_BLOCK
        qb = lax.dynamic_slice_in_dim(q, t0, Q_BLOCK, axis=1)
        cq = lax.dynamic_slice_in_dim(cT, t0, Q_BLOCK, axis=2)
        logits = jnp.einsum('bthd,bshd->bhts', qb, k).astype(jnp.float32) * scale
        logits = logits + cq[..., None] - cT[:, :, None, :]
        causal = key_pos[None, :] <= (t0 + jnp.arange(Q_BLOCK))[:, None]
        logits = jnp.where(causal[None, None], logits, -jnp.inf)
        p = jax.nn.softmax(logits, axis=-1)
        return jnp.einsum('bhts,bshd->bthd', p.astype(v.dtype), v)

    out = lax.map(block, jnp.arange(n_blk))
    return jnp.moveaxis(out, 0, 1).reshape(B, S, H * Dh)


def even_mixer(h, w_in, conv_w, conv_b, w_r, b_r, w_i, b_i, lam, q_g, k_g, w_out, pos):
    B, S, _ = h.shape
    z = h @ w_in
    xa, ga, q, k, v, qi, ki, wi = split_cols(z, EVEN_SPLITS)
    xa = causal_dwconv(xa, conv_w, conv_b)
    ya = rg_lru(xa, w_r, b_r, w_i, b_i, lam) * jax.nn.gelu(ga, approximate=True)
    q = rope(rms_norm(q.reshape(B, S, DSA_HEADS, HEAD_DIM), q_g), pos)
    k = rope(rms_norm(k.reshape(B, S, DSA_HEADS, HEAD_DIM), k_g), pos)
    v = v.reshape(B, S, DSA_HEADS, HEAD_DIM)
    qi = rope(qi.reshape(B, S, IDX_HEADS, IDX_DIM), pos)
    ki = rope(ki, pos)
    yb = dsa_attention(q, k, v, qi, ki, wi)
    return jnp.concatenate([ya, yb], axis=-1) @ w_out


def odd_mixer(h, w_in, b_f, q_g, k_g, conv_w, conv_b, ln_g, ln_b, w_out):
    B, S, _ = h.shape
    z = h @ w_in
    q, k, v, fl, u = split_cols(z, ODD_SPLITS)
    q = rms_norm(q.reshape(B, S, FOX_HEADS, HEAD_DIM), q_g)
    k = rms_norm(k.reshape(B, S, FOX_HEADS, HEAD_DIM), k_g)
    v = v.reshape(B, S, FOX_HEADS, HEAD_DIM)
    log_f = jax.nn.log_sigmoid(fl.astype(jnp.float32) + b_f.astype(jnp.float32))
    c = jnp.cumsum(log_f, axis=1)
    yc = fox_attention(q, k, v, c)
    ua, ug = jnp.split(u, 2, axis=-1)
    yd = causal_dwconv(ua * jax.nn.sigmoid(ug), conv_w, conv_b)
    yd = jax.nn.silu(layer_norm(yd, ln_g, ln_b))
    return jnp.concatenate([yc, yd], axis=-1) @ w_out


def swiglu(h, w_gu, w_down):
    g, u = jnp.split(h @ w_gu, 2, axis=-1)
    return (jax.nn.silu(g) * u) @ w_down


def setup_inputs(seed: int = 0) -> dict:
    key = jax.random.key(seed)
    ks = jax.random.split(key, 32)

    def nrm(k, shape, scale):
        return jax.random.normal(k, shape, jnp.float32) * scale

    out_scale = (2.0 * DEPTH) ** -0.5
    u = jax.random.uniform(ks[9], (N_EVEN, LRU_WIDTH), jnp.float32, 0.9, 0.999)
    a = u ** (1.0 / LRU_C)
    lam = jnp.log(a) - jnp.log1p(-a)
    return {
        "x": nrm(ks[0], (BATCH, SEQ, D_MODEL), 1.0),
        "norm_mix": 1.0 + nrm(ks[1], (DEPTH, D_MODEL), 0.02),
        "norm_ffn": 1.0 + nrm(ks[2], (DEPTH, D_MODEL), 0.02),
        "ev_w_in": nrm(ks[3], (N_EVEN, D_MODEL, EVEN_IN), D_MODEL ** -0.5),
        "ev_conv_w": nrm(ks[4], (N_EVEN, LRU_CONV, LRU_WIDTH), LRU_CONV ** -0.5),
        "ev_conv_b": nrm(ks[5], (N_EVEN, LRU_WIDTH), 0.02),
        "ev_w_r": nrm(ks[6], (N_EVEN, LRU_BLOCKS, LRU_BLOCK_DIM, LRU_BLOCK_DIM), LRU_BLOCK_DIM ** -0.5),
        "ev_b_r": nrm(ks[7], (N_EVEN, LRU_WIDTH), 0.02),
        "ev_w_i": nrm(ks[8], (N_EVEN, LRU_BLOCKS, LRU_BLOCK_DIM, LRU_BLOCK_DIM), LRU_BLOCK_DIM ** -0.5),
        "ev_b_i": nrm(ks[10], (N_EVEN, LRU_WIDTH), 0.02),
        "ev_lam": lam,
        "ev_q_norm": 1.0 + nrm(ks[11], (N_EVEN, HEAD_DIM), 0.02),
        "ev_k_norm": 1.0 + nrm(ks[12], (N_EVEN, HEAD_DIM), 0.02),
        "ev_w_out": nrm(ks[13], (N_EVEN, EVEN_OUT, D_MODEL), EVEN_OUT ** -0.5 * out_scale),
        "od_w_in": nrm(ks[14], (N_ODD, D_MODEL, ODD_IN), D_MODEL ** -0.5),
        "od_b_f": jax.random.uniform(ks[15], (N_ODD, FOX_HEADS), jnp.float32, 1.0, 5.0),
        "od_q_norm": 1.0 + nrm(ks[16], (N_ODD, HEAD_DIM), 0.02),
        "od_k_norm": 1.0 + nrm(ks[17], (N_ODD, HEAD_DIM), 0.02),
        "od_conv_w": nrm(ks[18], (N_ODD, CONV_KERNEL, CONV_WIDTH), CONV_KERNEL ** -0.5),
        "od_conv_b": nrm(ks[19], (N_ODD, CONV_WIDTH), 0.02),
        "od_ln_g": 1.0 + nrm(ks[20], (N_ODD, CONV_WIDTH), 0.02),
        "od_ln_b": nrm(ks[21], (N_ODD, CONV_WIDTH), 0.02),
        "od_w_out": nrm(ks[22], (N_ODD, ODD_OUT, D_MODEL), ODD_OUT ** -0.5 * out_scale),
        "ffn_w_gu": nrm(ks[23], (DEPTH, D_MODEL, 2 * FFN_HIDDEN), D_MODEL ** -0.5),
        "ffn_w_down": nrm(ks[24], (DEPTH, FFN_HIDDEN, D_MODEL), FFN_HIDDEN ** -0.5 * out_scale),
    }


def reference(x, norm_mix, norm_ffn,
              ev_w_in, ev_conv_w, ev_conv_b, ev_w_r, ev_b_r, ev_w_i, ev_b_i, ev_lam,
              ev_q_norm, ev_k_norm, ev_w_out,
              od_w_in, od_b_f, od_q_norm, od_k_norm, od_conv_w, od_conv_b, od_ln_g, od_ln_b,
              od_w_out, ffn_w_gu, ffn_w_down):
    S = x.shape[1]
    pos = jnp.arange(S, dtype=jnp.float32)
    for l in range(DEPTH):
        j = l // 2
        h = rms_norm(x, norm_mix[l])
        if l % 2 == 0:
            y = even_mixer(h, ev_w_in[j], ev_conv_w[j], ev_conv_b[j], ev_w_r[j], ev_b_r[j],
                           ev_w_i[j], ev_b_i[j], ev_lam[j], ev_q_norm[j], ev_k_norm[j],
                           ev_w_out[j], pos)
        else:
            y = odd_mixer(h, od_w_in[j], od_b_f[j], od_q_norm[j], od_k_norm[j],
                          od_conv_w[j], od_conv_b[j], od_ln_g[j], od_ln_b[j], od_w_out[j])
        x = x + y
        x = x + swiglu(rms_norm(x, norm_ffn[l]), ffn_w_gu[l], ffn_w_down[l])
    return x
```

```python
import functools

import jax
import jax.numpy as jnp
from jax import lax
from jax.experimental import pallas as pl
from jax.experimental.pallas import tpu as pltpu

F32 = jnp.float32
BF16 = jnp.bfloat16

HEAD_DIM = 64
N_HEADS = 8
N_PAIRS = N_HEADS // 2
LANES = 128
CHUNK = 64
CHUNK_SHIFT = 6
TOPK_MAX = 256
ROPE_THETA = 10000.0
NORM_EPS = 1e-6
LRU_C = 8.0
LRU_CONV = 4
CONV_KERNEL = 31
MASKED = -1e30
VMEM_LIMIT = 56 * 1024 * 1024

assert CHUNK == 1 << CHUNK_SHIFT


def _params(*sem):
    return pltpu.CompilerParams(dimension_semantics=sem, vmem_limit_bytes=VMEM_LIMIT)


def _resident(shape, index_map):
    return pl.BlockSpec(shape, index_map, pipeline_mode=pl.Buffered(1))


def _norm_matmul_kernel(x_ref, g_ref, w_ref, *o_refs, widths):
    x = x_ref[...]
    ms = jnp.mean(x * x, axis=-1, keepdims=True)
    h = (x * lax.rsqrt(ms + NORM_EPS) * g_ref[...]).astype(BF16)
    off = 0
    for o_ref, wd in zip(o_refs, widths):
        o_ref[...] = jnp.dot(h, w_ref[:, off:off + wd], preferred_element_type=F32)
        off += wd


def _norm_matmul(x2, g, w, widths, tm=512):
    n, d = x2.shape
    assert n % tm == 0 and sum(widths) == w.shape[1]
    return pl.pallas_call(
        functools.partial(_norm_matmul_kernel, widths=widths),
        grid=(n // tm,),
        in_specs=[pl.BlockSpec((tm, d), lambda i: (i, 0)),
                  _resident((1, d), lambda i: (0, 0)),
                  _resident(w.shape, lambda i: (0, 0))],
        out_specs=[pl.BlockSpec((tm, wd), lambda i: (i, 0)) for wd in widths],
        out_shape=[jax.ShapeDtypeStruct((n, wd), F32) for wd in widths],
        compiler_params=_params("parallel"),
        name="norm_matmul",
    )(x2, g.reshape(1, d), w)


def _shift_rows(x, d, fill):
    row = lax.broadcasted_iota(jnp.int32, x.shape, 0)
    return jnp.where(row >= d, pltpu.roll(x, d, 0), fill)


def _linear_scan_rows(a, u):
    d = 1
    while d < a.shape[0]:
        u = a * _shift_rows(u, d, 0.0) + u
        a = a * _shift_rows(a, d, 1.0)
        d *= 2
    return a, u


def _cumsum_rows(x):
    d = 1
    while d < x.shape[0]:
        x = x + _shift_rows(x, d, 0.0)
        d *= 2
    return x


def _softplus(x):
    return jnp.maximum(x, 0.0) + jnp.log1p(jnp.exp(-jnp.abs(x)))


def _lru_kernel(xg_ref, cw_ref, cb_ref, wr_ref, br_ref, wi_ref, bi_ref, lam_ref, o_ref,
                xbuf_ref, h_ref, *, ts, width):
    @pl.when(pl.program_id(1) == 0)
    def _():
        xbuf_ref[0:8, :] = jnp.zeros((8, width), F32)
        h_ref[...] = jnp.zeros_like(h_ref)

    xa = xg_ref[0, :, 0:width]
    ga = xg_ref[0, :, width:2 * width]
    xbuf_ref[8:8 + ts, :] = xa
    xc = cb_ref[...] + cw_ref[0:1, :] * xbuf_ref[pl.ds(8 - (LRU_CONV - 1), ts), :]
    for j in range(1, LRU_CONV):
        xc = xc + cw_ref[j:j + 1, :] * xbuf_ref[pl.ds(8 - (LRU_CONV - 1) + j, ts), :]
    xbuf_ref[0:8, :] = xa[ts - 8:ts, :]

    xb = xc.astype(BF16)
    r = jax.nn.sigmoid(jnp.dot(xb, wr_ref[...], preferred_element_type=F32) + br_ref[...])
    gate = jax.nn.sigmoid(jnp.dot(xb, wi_ref[...], preferred_element_type=F32) + bi_ref[...])
    log_a = -LRU_C * r * _softplus(-lam_ref[...])
    a = jnp.exp(log_a)
    th = jnp.tanh(log_a)
    u = jnp.sqrt(-2.0 * th / (1.0 - th)) * (gate * xc)
    a_cum, h = _linear_scan_rows(a, u)
    h = h + a_cum * h_ref[...]
    h_ref[...] = h[ts - 1:ts, :]
    o_ref[0] = (h * jax.nn.gelu(ga, approximate=True)).astype(o_ref.dtype)


def _lru(xg, conv_w, conv_b, w_r, b_r, w_i, b_i, lam, ts=256):
    b, s, w2 = xg.shape
    width = w2 // 2
    assert s % ts == 0
    row = lambda v: v.reshape(1, width)
    const = lambda shape: _resident(shape, lambda bi, si: (0, 0))
    return pl.pallas_call(
        functools.partial(_lru_kernel, ts=ts, width=width),
        grid=(b, s // ts),
        in_specs=[pl.BlockSpec((1, ts, w2), lambda bi, si: (bi, si, 0)),
                  const((LRU_CONV, width)), const((1, width)),
                  const((width, width)), const((1, width)),
                  const((width, width)), const((1, width)), const((1, width))],
        out_specs=pl.BlockSpec((1, ts, width), lambda bi, si: (bi, si, 0)),
        out_shape=jax.ShapeDtypeStruct((b, s, width), BF16),
        scratch_shapes=[pltpu.VMEM((ts + 8, width), F32), pltpu.VMEM((1, width), F32)],
        compiler_params=_params("parallel", "arbitrary"),
        name="rg_lru",
    )(xg, conv_w, row(conv_b), w_r, row(b_r), w_i, row(b_i), row(lam))


def _head_mean_square(x, bd_ref):
    x2 = x * x
    hi = x2.astype(BF16)
    r1 = x2 - hi.astype(F32)
    mid = r1.astype(BF16)
    lo = (r1 - mid.astype(F32)).astype(BF16)
    bd = bd_ref[...]
    tot = (jnp.dot(hi, bd, preferred_element_type=F32) + jnp.dot(mid, bd, preferred_element_type=F32)
           + jnp.dot(lo, bd, preferred_element_type=F32))
    return tot * (1.0 / HEAD_DIM)


def _head_rms_norm(x, g, bd_ref):
    return x * lax.rsqrt(_head_mean_square(x, bd_ref) + NORM_EPS) * g


def _rope(x, cos, sin_signed):
    n = x.shape[1]
    lane = lax.broadcasted_iota(jnp.int32, x.shape, 1)
    first_half = (lane & (HEAD_DIM - 1)) < HEAD_DIM // 2
    partner = jnp.where(first_half, pltpu.roll(x, n - HEAD_DIM // 2, 1), pltpu.roll(x, HEAD_DIM // 2, 1))
    return x * cos + partner * sin_signed


def _prep_even_kernel(q_ref, k_ref, v_ref, qi_ref, kw_ref, cos_ref, sin_ref, gq_ref, gk_ref, bd_ref,
                      qo_ref, kto_ref, vo_ref, qio_ref, kito_ref):
    cos = cos_ref[...]
    sin = sin_ref[...]
    q = _rope(_head_rms_norm(q_ref[0], gq_ref[...], bd_ref), cos, sin)
    qo_ref[0] = (q * HEAD_DIM ** -0.5).astype(BF16)
    k = _rope(_head_rms_norm(k_ref[0], gk_ref[...], bd_ref), cos, sin)
    kto_ref[0] = k.T.astype(BF16)
    vo_ref[0] = v_ref[0].astype(BF16)
    qio_ref[0] = _rope(qi_ref[0], cos, sin).astype(BF16)
    ki = _rope(kw_ref[0], cos[:, 0:LANES], sin[:, 0:LANES])
    lane = lax.broadcasted_iota(jnp.int32, ki.shape, 1)
    ki2 = jnp.where(lane < HEAD_DIM, ki, pltpu.roll(ki, HEAD_DIM, 1))
    kito_ref[0] = ki2.T.astype(BF16)


def _prep_even(q, k, v, qi, kw, cos, sin, gq, gk, bd, ts=512):
    b, s, w = q.shape
    assert s % ts == 0
    tile = pl.BlockSpec((1, ts, w), lambda bi, si: (bi, si, 0))
    tab = pl.BlockSpec((ts, w), lambda bi, si: (si, 0))
    const = lambda shape: _resident(shape, lambda bi, si: (0, 0))
    return pl.pallas_call(
        _prep_even_kernel,
        grid=(b, s // ts),
        in_specs=[tile, tile, tile, tile, pl.BlockSpec((1, ts, LANES), lambda bi, si: (bi, si, 0)),
                  tab, tab, const((1, w)), const((1, w)), const((w, w))],
        out_specs=[tile, pl.BlockSpec((1, w, ts), lambda bi, si: (bi, 0, si)), tile, tile,
                   pl.BlockSpec((1, LANES, ts), lambda bi, si: (bi, 0, si))],
        out_shape=[jax.ShapeDtypeStruct((b, s, w), BF16), jax.ShapeDtypeStruct((b, w, s), BF16),
                   jax.ShapeDtypeStruct((b, s, w), BF16), jax.ShapeDtypeStruct((b, s, w), BF16),
                   jax.ShapeDtypeStruct((b, LANES, s), BF16)],
        compiler_params=_params("parallel", "parallel"),
        name="prep_even",
    )(q, k, v, qi, kw, cos, sin, gq, gk, bd)


def _prep_odd_kernel(q_ref, k_ref, v_ref, fl_ref, bf_ref, gq_ref, gk_ref, bd_ref,
                     qo_ref, kto_ref, vo_ref, c_ref, carry_ref):
    @pl.when(pl.program_id(1) == 0)
    def _():
        carry_ref[...] = jnp.zeros_like(carry_ref)

    q = _head_rms_norm(q_ref[0], gq_ref[...], bd_ref)
    qo_ref[0] = (q * HEAD_DIM ** -0.5).astype(BF16)
    kto_ref[0] = _head_rms_norm(k_ref[0], gk_ref[...], bd_ref).T.astype(BF16)
    vo_ref[0] = v_ref[0].astype(BF16)
    log_f = -_softplus(-(fl_ref[0] + bf_ref[...]))
    c = _cumsum_rows(log_f) + carry_ref[...]
    carry_ref[...] = c[c.shape[0] - 1:, :]
    c_ref[0] = c


def _prep_odd(q, k, v, fl, bf, gq, gk, bd, ts=512):
    b, s, w = q.shape
    assert s % ts == 0
    tile = pl.BlockSpec((1, ts, w), lambda bi, si: (bi, si, 0))
    narrow = pl.BlockSpec((1, ts, LANES), lambda bi, si: (bi, si, 0))
    const = lambda shape: _resident(shape, lambda bi, si: (0, 0))
    return pl.pallas_call(
        _prep_odd_kernel,
        grid=(b, s // ts),
        in_specs=[tile, tile, tile, narrow, const((1, LANES)), const((1, w)), const((1, w)), const((w, w))],
        out_specs=[tile, pl.BlockSpec((1, w, ts), lambda bi, si: (bi, 0, si)), tile, narrow],
        out_shape=[jax.ShapeDtypeStruct((b, s, w), BF16), jax.ShapeDtypeStruct((b, w, s), BF16),
                   jax.ShapeDtypeStruct((b, s, w), BF16), jax.ShapeDtypeStruct((b, s, LANES), F32)],
        scratch_shapes=[pltpu.VMEM((1, LANES), F32)],
        compiler_params=_params("parallel", "arbitrary"),
        name="prep_odd",
    )(q, k, v, fl, bf, gq, gk, bd)


def _head_halves(x_pair):
    lane = lax.broadcasted_iota(jnp.int32, x_pair.shape, 1)
    zero = jnp.zeros_like(x_pair)
    return jnp.where(lane < HEAD_DIM, x_pair, zero), jnp.where(lane >= HEAD_DIM, x_pair, zero)


def _online_softmax_step(s, vv, m_ref, l_ref, acc_ref, idx):
    m_old = m_ref[idx]
    m_new = jnp.maximum(m_old, jnp.max(s, axis=1, keepdims=True))
    p = jnp.exp(s - m_new)
    alpha = jnp.exp(m_old - m_new)
    l_ref[idx] = alpha * l_ref[idx] + jnp.sum(p, axis=1, keepdims=True)
    acc_ref[idx] = alpha * acc_ref[idx] + jnp.dot(p.astype(BF16), vv, preferred_element_type=F32)
    m_ref[idx] = m_new


def _merge_pair(acc_ref, l_ref, i0, i1):
    lane = lax.broadcasted_iota(jnp.int32, acc_ref.shape[1:], 1)
    return jnp.where(lane < HEAD_DIM, acc_ref[i0] / l_ref[i0], acc_ref[i1] / l_ref[i1])


def _dsa_kernel(qi_ref, kw_ref, kit_ref, q_ref, kt_ref, v_ref, o_ref,
                sc_ref, m_ref, l_ref, acc_ref, *, tq, tk, topk, max_iters):
    t0 = pl.program_id(1) * tq
    nb = (t0 + tq + tk - 1) // tk
    row_chunk = (t0 + lax.broadcasted_iota(jnp.int32, (tq, 1), 0)) >> CHUNK_SHIFT
    n_visible = (row_chunk + 1) * CHUNK
    kf = float(topk)

    def cols(j):
        return pl.ds(pl.multiple_of(j * tk, tk), tk)

    def fold_lanes(x):
        out = x[:, 0:LANES]
        for c in range(1, tk // LANES):
            out = out + x[:, c * LANES:(c + 1) * LANES]
        return out

    qi_heads = []
    for p in range(N_PAIRS):
        qi_heads.extend(_head_halves(qi_ref[0, :, p * LANES:(p + 1) * LANES]))
    kw = kw_ref[0]

    def score_block(j, carry):
        kit = kit_ref[0, :, cols(j)]
        score = jnp.zeros((tq, tk), F32)
        for h in range(N_HEADS):
            rel = jnp.maximum(jnp.dot(qi_heads[h], kit, preferred_element_type=F32), 0.0)
            score = score + kw[:, HEAD_DIM + h:HEAD_DIM + h + 1] * rel
        key_chunk = (j * tk + lax.broadcasted_iota(jnp.int32, (1, tk), 1)) >> CHUNK_SHIFT
        sc_ref[:, cols(j)] = jnp.where(key_chunk <= row_chunk, score, -jnp.inf)
        return carry

    lax.fori_loop(0, nb, score_block, 0)

    def count(pred):
        def body(j, cnt):
            return cnt + fold_lanes(jnp.where(pred(sc_ref[:, cols(j)]), 1.0, 0.0))
        return jnp.sum(lax.fori_loop(0, nb, body, jnp.zeros((tq, LANES), F32)), axis=1, keepdims=True)

    def range_body(j, carry):
        lo, hi = carry
        blk = sc_ref[:, cols(j)]
        lo = jnp.minimum(lo, fold_lanes_min(jnp.where(blk == -jnp.inf, jnp.inf, blk)))
        return lo, jnp.maximum(hi, fold_lanes_max(blk))

    def fold_lanes_min(x):
        out = x[:, 0:LANES]
        for c in range(1, tk // LANES):
            out = jnp.minimum(out, x[:, c * LANES:(c + 1) * LANES])
        return out

    def fold_lanes_max(x):
        out = x[:, 0:LANES]
        for c in range(1, tk // LANES):
            out = jnp.maximum(out, x[:, c * LANES:(c + 1) * LANES])
        return out

    lo_w, hi_w = lax.fori_loop(0, nb, range_body,
                               (jnp.full((tq, LANES), jnp.inf, F32), jnp.full((tq, LANES), -jnp.inf, F32)))
    lo0 = jnp.min(lo_w, axis=1, keepdims=True)
    hi0 = jnp.max(hi_w, axis=1, keepdims=True)

    def bisect_cond(state):
        it, _, _, open_rows = state
        return jnp.logical_and(it < max_iters, open_rows > 0.0)

    def bisect_body(state):
        it, lo, hi, _ = state
        mid = lo + (hi - lo) * 0.5
        enough = count(lambda blk: blk >= mid) >= kf
        still_open = jnp.where(jnp.logical_and(mid > lo, mid < hi), 1.0, 0.0)
        return (it + 1, jnp.where(enough, mid, lo), jnp.where(enough, hi, mid),
                jnp.max(still_open))

    _, lo, _, _ = lax.while_loop(bisect_cond, bisect_body, (jnp.int32(0), lo0, hi0, jnp.float32(1.0)))
    short_row = n_visible <= topk
    thr = jnp.where(short_row, jnp.finfo(F32).min, lo)

    n_ge = count(lambda blk: blk >= thr)
    has_extra = jnp.max(jnp.where(jnp.logical_and(n_ge > kf, jnp.logical_not(short_row)), 1.0, 0.0))

    @pl.when(has_extra > 0.0)
    def _():
        keep = kf - count(lambda blk: blk > thr)
        upper = (lax.broadcasted_iota(jnp.int32, (tk, tk), 0)
                 <= lax.broadcasted_iota(jnp.int32, (tk, tk), 1))
        prefix = jnp.where(upper, 1.0, 0.0).astype(BF16)

        def body(j, seen):
            blk = sc_ref[:, cols(j)]
            tie = jnp.where(blk == thr, 1.0, 0.0)
            rank = seen + jnp.dot(tie.astype(BF16), prefix, preferred_element_type=F32)
            drop = jnp.where(rank > keep, tie, 0.0)
            sc_ref[:, cols(j)] = jnp.where(drop > 0.0, -jnp.inf, blk)
            return seen + jnp.sum(tie, axis=1, keepdims=True)

        lax.fori_loop(0, nb, body, jnp.zeros((tq, 1), F32))

    m_ref[...] = jnp.full_like(m_ref, MASKED)
    l_ref[...] = jnp.zeros_like(l_ref)
    acc_ref[...] = jnp.zeros_like(acc_ref)
    q_heads = []
    for p in range(N_PAIRS):
        q_heads.extend(_head_halves(q_ref[0, :, p * LANES:(p + 1) * LANES]))

    def attend_block(j, carry):
        selected = sc_ref[:, cols(j)] >= thr
        for h in range(N_HEADS):
            p = h // 2
            kt = kt_ref[0, p * LANES:(p + 1) * LANES, cols(j)]
            s = jnp.where(selected, jnp.dot(q_heads[h], kt, preferred_element_type=F32), MASKED)
            _online_softmax_step(s, v_ref[0, cols(j), p * LANES:(p + 1) * LANES], m_ref, l_ref, acc_ref, h)
        return carry

    lax.fori_loop(0, nb, attend_block, 0)
    for p in range(N_PAIRS):
        o_ref[0, :, p * LANES:(p + 1) * LANES] = _merge_pair(acc_ref, l_ref, 2 * p, 2 * p + 1).astype(o_ref.dtype)


def _dsa(qi, kw, kit, q, kt, v, tq=128, tk=512):
    b, s, w = q.shape
    tk = min(tk, s)
    assert s % tk == 0 and s % tq == 0 and tq % CHUNK == 0
    topk = min(TOPK_MAX, s // 4)
    qtile = pl.BlockSpec((1, tq, w), lambda bi, i: (bi, i, 0))
    return pl.pallas_call(
        functools.partial(_dsa_kernel, tq=tq, tk=tk, topk=topk, max_iters=320),
        grid=(b, s // tq),
        in_specs=[qtile, pl.BlockSpec((1, tq, LANES), lambda bi, i: (bi, i, 0)),
                  _resident((1, LANES, s), lambda bi, i: (bi, 0, 0)),
                  qtile,
                  _resident((1, w, s), lambda bi, i: (bi, 0, 0)),
                  _resident((1, s, w), lambda bi, i: (bi, 0, 0))],
        out_specs=qtile,
        out_shape=jax.ShapeDtypeStruct((b, s, w), BF16),
        scratch_shapes=[pltpu.VMEM((tq, s), F32),
                        pltpu.VMEM((N_HEADS, tq, 1), F32), pltpu.VMEM((N_HEADS, tq, 1), F32),
                        pltpu.VMEM((N_HEADS, tq, LANES), F32)],
        compiler_params=_params("parallel", "arbitrary"),
        name="dsa",
    )(qi, kw, kit, q, kt, v)


def _fox_kernel(q_ref, kt_ref, v_ref, cq_ref, ck_ref, o_ref, m_ref, l_ref, acc_ref, *, tq):
    i = pl.program_id(2)
    m_ref[...] = jnp.full_like(m_ref, MASKED)
    l_ref[...] = jnp.zeros_like(l_ref)
    acc_ref[...] = jnp.zeros_like(acc_ref)
    q_heads = _head_halves(q_ref[0])
    cq = cq_ref[0, 0]
    causal = (lax.broadcasted_iota(jnp.int32, (tq, tq), 1) <= lax.broadcasted_iota(jnp.int32, (tq, tq), 0))

    def block(j, diagonal):
        cols = pl.ds(pl.multiple_of(j * tq, tq), tq)
        kt = kt_ref[0, :, cols]
        vv = v_ref[0, cols, :]
        for e in range(2):
            s = jnp.dot(q_heads[e], kt, preferred_element_type=F32) + cq[:, e:e + 1] - ck_ref[0, 0, e:e + 1, cols]
            if diagonal:
                s = jnp.where(causal, s, MASKED)
            _online_softmax_step(s, vv, m_ref, l_ref, acc_ref, e)

    def body(j, carry):
        block(j, False)
        return carry

    lax.fori_loop(0, i, body, 0)
    block(i, True)
    o_ref[0] = _merge_pair(acc_ref, l_ref, 0, 1).astype(o_ref.dtype)


def _fox(q, kt, v, cq, ck, tq=256):
    b, s, w = q.shape
    assert s % tq == 0
    return pl.pallas_call(
        functools.partial(_fox_kernel, tq=tq),
        grid=(b, N_PAIRS, s // tq),
        in_specs=[pl.BlockSpec((1, tq, LANES), lambda bi, p, i: (bi, i, p)),
                  pl.BlockSpec((1, LANES, s), lambda bi, p, i: (bi, p, 0)),
                  pl.BlockSpec((1, s, LANES), lambda bi, p, i: (bi, 0, p)),
                  pl.BlockSpec((1, 1, tq, 2), lambda bi, p, i: (bi, p, i, 0)),
                  pl.BlockSpec((1, 1, 2, s), lambda bi, p, i: (bi, p, 0, 0))],
        out_specs=pl.BlockSpec((1, tq, LANES), lambda bi, p, i: (bi, i, p)),
        out_shape=jax.ShapeDtypeStruct((b, s, w), BF16),
        scratch_shapes=[pltpu.VMEM((2, tq, 1), F32), pltpu.VMEM((2, tq, 1), F32),
                        pltpu.VMEM((2, tq, LANES), F32)],
        compiler_params=_params("parallel", "parallel", "arbitrary"),
        name="fox",
    )(q, kt, v, cq, ck)


def _convmod_kernel(u_ref, cw_ref, cb_ref, g_ref, b_ref, o_ref, xbuf_ref, *, ts, width, halo):
    @pl.when(pl.program_id(1) == 0)
    def _():
        xbuf_ref[0:halo, :] = jnp.zeros((halo, width), F32)

    x = u_ref[0, :, 0:width] * jax.nn.sigmoid(u_ref[0, :, width:2 * width])
    xbuf_ref[halo:halo + ts, :] = x
    base = halo - (CONV_KERNEL - 1)
    y = cb_ref[...] + cw_ref[0:1, :] * xbuf_ref[pl.ds(base, ts), :]
    for j in range(1, CONV_KERNEL):
        y = y + cw_ref[j:j + 1, :] * xbuf_ref[pl.ds(base + j, ts), :]
    xbuf_ref[0:halo, :] = x[ts - halo:ts, :]
    mu = jnp.mean(y, axis=-1, keepdims=True)
    var = jnp.mean(jnp.square(y - mu), axis=-1, keepdims=True)
    z = (y - mu) * lax.rsqrt(var + NORM_EPS) * g_ref[...] + b_ref[...]
    o_ref[0] = (z * jax.nn.sigmoid(z)).astype(o_ref.dtype)


def _convmod(u, conv_w, conv_b, ln_g, ln_b, ts=512, halo=32):
    b, s, w2 = u.shape
    width = w2 // 2
    assert s % ts == 0 and halo >= CONV_KERNEL - 1
    row = lambda v: v.reshape(1, width)
    const = lambda shape: _resident(shape, lambda bi, si: (0, 0))
    return pl.pallas_call(
        functools.partial(_convmod_kernel, ts=ts, width=width, halo=halo),
        grid=(b, s // ts),
        in_specs=[pl.BlockSpec((1, ts, w2), lambda bi, si: (bi, si, 0)),
                  const((CONV_KERNEL, width)), const((1, width)), const((1, width)), const((1, width))],
        out_specs=pl.BlockSpec((1, ts, width), lambda bi, si: (bi, si, 0)),
        out_shape=jax.ShapeDtypeStruct((b, s, width), BF16),
        scratch_shapes=[pltpu.VMEM((ts + halo, width), F32)],
        compiler_params=_params("parallel", "arbitrary"),
        name="conv_module",
    )(u, conv_w, row(conv_b), row(ln_g), row(ln_b))


def _out_ffn_kernel(x_ref, ya_ref, yb_ref, wo_ref, g_ref, wgu_ref, wd_ref, o_ref, *, hidden, th):
    half = ya_ref.shape[1]
    x = x_ref[...] + (jnp.dot(ya_ref[...], wo_ref[0:half, :], preferred_element_type=F32)
                      + jnp.dot(yb_ref[...], wo_ref[half:2 * half, :], preferred_element_type=F32))
    ms = jnp.mean(x * x, axis=-1, keepdims=True)
    h = (x * lax.rsqrt(ms + NORM_EPS) * g_ref[...]).astype(BF16)
    y = x
    for c in range(0, hidden, th):
        gate = jnp.dot(h, wgu_ref[:, c:c + th], preferred_element_type=F32)
        up = jnp.dot(h, wgu_ref[:, hidden + c:hidden + c + th], preferred_element_type=F32)
        act = (gate * jax.nn.sigmoid(gate) * up).astype(BF16)
        y = y + jnp.dot(act, wd_ref[c:c + th, :], preferred_element_type=F32)
    o_ref[...] = y


def _out_ffn(x2, ya, yb, w_out, g, w_gu, w_down, tm=512, th=256):
    n, d = x2.shape
    half = ya.shape[1]
    hidden = w_down.shape[0]
    assert n % tm == 0 and hidden % th == 0
    const = lambda shape: _resident(shape, lambda i: (0, 0))
    return pl.pallas_call(
        functools.partial(_out_ffn_kernel, hidden=hidden, th=th),
        grid=(n // tm,),
        in_specs=[pl.BlockSpec((tm, d), lambda i: (i, 0)),
                  pl.BlockSpec((tm, half), lambda i: (i, 0)), pl.BlockSpec((tm, half), lambda i: (i, 0)),
                  const(w_out.shape), const((1, d)), const(w_gu.shape), const(w_down.shape)],
        out_specs=pl.BlockSpec((tm, d), lambda i: (i, 0)),
        out_shape=jax.ShapeDtypeStruct((n, d), F32),
        compiler_params=_params("parallel"),
        name="out_ffn",
    )(x2, ya, yb, w_out, g.reshape(1, d), w_gu, w_down)


def _rope_tables(s, width):
    inv = ROPE_THETA ** (-jnp.arange(0, HEAD_DIM, 2, dtype=F32) / HEAD_DIM)
    ang = jnp.arange(s, dtype=F32)[:, None] * inv[None, :]
    cos, sin = jnp.cos(ang), jnp.sin(ang)
    reps = width // HEAD_DIM
    return (jnp.tile(jnp.concatenate([cos, cos], axis=-1), (1, reps)),
            jnp.tile(jnp.concatenate([-sin, sin], axis=-1), (1, reps)))


def _block_diag(blocks):
    n, d, _ = blocks.shape
    eye = jnp.eye(n, dtype=blocks.dtype)
    return jnp.einsum('nde,nm->ndme', blocks, eye).reshape(n * d, n * d)


def _pad_cols(w, total):
    return jnp.pad(w, ((0, 0), (0, total - w.shape[1])))


def kernel(x, norm_mix, norm_ffn,
           ev_w_in, ev_conv_w, ev_conv_b, ev_w_r, ev_b_r, ev_w_i, ev_b_i, ev_lam,
           ev_q_norm, ev_k_norm, ev_w_out,
           od_w_in, od_b_f, od_q_norm, od_k_norm, od_conv_w, od_conv_b, od_ln_g, od_ln_b,
           od_w_out, ffn_w_gu, ffn_w_down):
    b, s, d = x.shape
    depth = norm_mix.shape[0]
    w = N_HEADS * HEAD_DIM
    cos, sin = _rope_tables(s, w)
    head_ones = _block_diag(jnp.ones((N_HEADS, HEAD_DIM, HEAD_DIM), BF16))
    tile_gain = lambda g: jnp.tile(g, N_HEADS).reshape(1, w)
    x2 = x.reshape(b * s, d)
    seq = lambda t: t.reshape(b, s, t.shape[-1])
    for l in range(depth):
        j = l // 2
        if l % 2 == 0:
            w_in = _pad_cols(ev_w_in[j], 2 * w + 4 * w + LANES).astype(BF16)
            xg, q, k, v, qi, kw = _norm_matmul(x2, norm_mix[l], w_in, (2 * w, w, w, w, w, LANES))
            ya = _lru(seq(xg), ev_conv_w[j], ev_conv_b[j],
                      _block_diag(ev_w_r[j]).astype(BF16), ev_b_r[j],
                      _block_diag(ev_w_i[j]).astype(BF16), ev_b_i[j], ev_lam[j])
            qh, kt, vh, qih, kit = _prep_even(seq(q), seq(k), seq(v), seq(qi), seq(kw), cos, sin,
                                              tile_gain(ev_q_norm[j]), tile_gain(ev_k_norm[j]), head_ones)
            yb = _dsa(qih, seq(kw), kit, qh, kt, vh)
            w_out = ev_w_out[j]
        else:
            wi = od_w_in[j]
            w_in = jnp.concatenate([wi[:, 0:3 * w], wi[:, 3 * w + N_HEADS:],
                                    _pad_cols(wi[:, 3 * w:3 * w + N_HEADS], LANES)], axis=1).astype(BF16)
            q, k, v, u, fl = _norm_matmul(x2, norm_mix[l], w_in, (w, w, w, 2 * w, LANES))
            qh, kt, vh, c = _prep_odd(seq(q), seq(k), seq(v), seq(fl),
                                      _pad_cols(od_b_f[j].reshape(1, N_HEADS), LANES),
                                      tile_gain(od_q_norm[j]), tile_gain(od_k_norm[j]), head_ones)
            c = c[:, :, 0:N_HEADS].reshape(b, s, N_PAIRS, 2)
            ya = _fox(qh, kt, vh, jnp.transpose(c, (0, 2, 1, 3)), jnp.transpose(c, (0, 2, 3, 1)))
            yb = _convmod(seq(u), od_conv_w[j], od_conv_b[j], od_ln_g[j], od_ln_b[j])
            w_out = od_w_out[j]
        x2 = _out_ffn(x2, ya.reshape(b * s, w), yb.reshape(b * s, w), w_out.astype(BF16),
                      norm_ffn[l], ffn_w_gu[l].astype(BF16), ffn_w_down[l].astype(BF16))
    return x2.reshape(b, s, d)
```

```python
import functools

import jax
import jax.numpy as jnp
from jax import lax
from jax.experimental import pallas as pl
from jax.experimental.pallas import tpu as pltpu

F32 = jnp.float32
BF16 = jnp.bfloat16

HEAD_DIM = 64
N_HEADS = 8
N_PAIRS = N_HEADS // 2
LANES = 128
CHUNK = 64
CHUNK_SHIFT = 6
TOPK_MAX = 256
ROPE_THETA = 10000.0
NORM_EPS = 1e-6
LRU_C = 8.0
LRU_CONV = 4
CONV_KERNEL = 31
MASKED = -1e30
LOG2E = 1.4426950408889634
QK_SCALE_LOG2 = HEAD_DIM ** -0.5 * LOG2E
VMEM_LIMIT = 56 * 1024 * 1024

assert CHUNK == 1 << CHUNK_SHIFT


def _params(*sem):
    return pltpu.CompilerParams(dimension_semantics=sem, vmem_limit_bytes=VMEM_LIMIT)


def _resident(shape, index_map):
    return pl.BlockSpec(shape, index_map, pipeline_mode=pl.Buffered(1))


def _norm_matmul_kernel(x_ref, g_ref, w_ref, *o_refs, widths):
    x = x_ref[...]
    ms = jnp.mean(x * x, axis=-1, keepdims=True)
    h = (x * lax.rsqrt(ms + NORM_EPS) * g_ref[...]).astype(BF16)
    off = 0
    for o_ref, wd in zip(o_refs, widths):
        o_ref[...] = jnp.dot(h, w_ref[:, off:off + wd], preferred_element_type=F32)
        off += wd


def _norm_matmul(x2, g, w, widths, tm=512):
    n, d = x2.shape
    assert n % tm == 0 and sum(widths) == w.shape[1]
    return pl.pallas_call(
        functools.partial(_norm_matmul_kernel, widths=widths),
        grid=(n // tm,),
        in_specs=[pl.BlockSpec((tm, d), lambda i: (i, 0)),
                  _resident((1, d), lambda i: (0, 0)),
                  _resident(w.shape, lambda i: (0, 0))],
        out_specs=[pl.BlockSpec((tm, wd), lambda i: (i, 0)) for wd in widths],
        out_shape=[jax.ShapeDtypeStruct((n, wd), F32) for wd in widths],
        compiler_params=_params("parallel"),
        name="norm_matmul",
    )(x2, g.reshape(1, d), w)


def _shift_rows(x, d, fill):
    row = lax.broadcasted_iota(jnp.int32, x.shape, 0)
    return jnp.where(row >= d, pltpu.roll(x, d, 0), fill)


def _linear_scan_rows(a, u):
    d = 1
    while d < a.shape[0]:
        u = a * _shift_rows(u, d, 0.0) + u
        a = a * _shift_rows(a, d, 1.0)
        d *= 2
    return a, u


def _cumsum_rows(x):
    d = 1
    while d < x.shape[0]:
        x = x + _shift_rows(x, d, 0.0)
        d *= 2
    return x


def _softplus(x):
    return jnp.maximum(x, 0.0) + jnp.log1p(jnp.exp(-jnp.abs(x)))


def _lru_kernel(xg_ref, cw_ref, cb_ref, wr_ref, br_ref, wi_ref, bi_ref, lam_ref, o_ref,
                xbuf_ref, h_ref, *, ts, width):
    @pl.when(pl.program_id(1) == 0)
    def _():
        xbuf_ref[0:8, :] = jnp.zeros((8, width), F32)
        h_ref[...] = jnp.zeros_like(h_ref)

    xa = xg_ref[0, :, 0:width]
    ga = xg_ref[0, :, width:2 * width]
    xbuf_ref[8:8 + ts, :] = xa
    xc = cb_ref[...] + cw_ref[0:1, :] * xbuf_ref[pl.ds(8 - (LRU_CONV - 1), ts), :]
    for j in range(1, LRU_CONV):
        xc = xc + cw_ref[j:j + 1, :] * xbuf_ref[pl.ds(8 - (LRU_CONV - 1) + j, ts), :]
    xbuf_ref[0:8, :] = xa[ts - 8:ts, :]

    xb = xc.astype(BF16)
    r = jax.nn.sigmoid(jnp.dot(xb, wr_ref[...], preferred_element_type=F32) + br_ref[...])
    gate = jax.nn.sigmoid(jnp.dot(xb, wi_ref[...], preferred_element_type=F32) + bi_ref[...])
    log_a = -LRU_C * r * _softplus(-lam_ref[...])
    a = jnp.exp(log_a)
    th = jnp.tanh(log_a)
    u = jnp.sqrt(-2.0 * th / (1.0 - th)) * (gate * xc)
    a_cum, h = _linear_scan_rows(a, u)
    h = h + a_cum * h_ref[...]
    h_ref[...] = h[ts - 1:ts, :]
    o_ref[0] = (h * jax.nn.gelu(ga, approximate=True)).astype(o_ref.dtype)


def _lru(xg, conv_w, conv_b, w_r, b_r, w_i, b_i, lam, ts=256):
    b, s, w2 = xg.shape
    width = w2 // 2
    assert s % ts == 0
    row = lambda v: v.reshape(1, width)
    const = lambda shape: _resident(shape, lambda bi, si: (0, 0))
    return pl.pallas_call(
        functools.partial(_lru_kernel, ts=ts, width=width),
        grid=(b, s // ts),
        in_specs=[pl.BlockSpec((1, ts, w2), lambda bi, si: (bi, si, 0)),
                  const((LRU_CONV, width)), const((1, width)),
                  const((width, width)), const((1, width)),
                  const((width, width)), const((1, width)), const((1, width))],
        out_specs=pl.BlockSpec((1, ts, width), lambda bi, si: (bi, si, 0)),
        out_shape=jax.ShapeDtypeStruct((b, s, width), BF16),
        scratch_shapes=[pltpu.VMEM((ts + 8, width), F32), pltpu.VMEM((1, width), F32)],
        compiler_params=_params("parallel", "arbitrary"),
        name="rg_lru",
    )(xg, conv_w, row(conv_b), w_r, row(b_r), w_i, row(b_i), row(lam))


def _head_mean_square(x, bd_ref):
    x2 = x * x
    hi = x2.astype(BF16)
    r1 = x2 - hi.astype(F32)
    mid = r1.astype(BF16)
    lo = (r1 - mid.astype(F32)).astype(BF16)
    bd = bd_ref[...]
    tot = (jnp.dot(hi, bd, preferred_element_type=F32) + jnp.dot(mid, bd, preferred_element_type=F32)
           + jnp.dot(lo, bd, preferred_element_type=F32))
    return tot * (1.0 / HEAD_DIM)


def _head_rms_norm(x, g, bd_ref):
    return x * lax.rsqrt(_head_mean_square(x, bd_ref) + NORM_EPS) * g


def _rope(x, cos, sin_signed):
    n = x.shape[1]
    lane = lax.broadcasted_iota(jnp.int32, x.shape, 1)
    first_half = (lane & (HEAD_DIM - 1)) < HEAD_DIM // 2
    partner = jnp.where(first_half, pltpu.roll(x, n - HEAD_DIM // 2, 1), pltpu.roll(x, HEAD_DIM // 2, 1))
    return x * cos + partner * sin_signed


def _prep_even_kernel(q_ref, k_ref, v_ref, qi_ref, kw_ref, cos_ref, sin_ref, gq_ref, gk_ref, bd_ref,
                      qo_ref, kto_ref, vo_ref, qio_ref, kito_ref):
    cos = cos_ref[...]
    sin = sin_ref[...]
    q = _rope(_head_rms_norm(q_ref[0], gq_ref[...], bd_ref), cos, sin)
    qo_ref[0] = (q * QK_SCALE_LOG2).astype(BF16)
    k = _rope(_head_rms_norm(k_ref[0], gk_ref[...], bd_ref), cos, sin)
    kto_ref[0] = k.T.astype(BF16)
    vo_ref[0] = v_ref[0].astype(BF16)
    qio_ref[0] = _rope(qi_ref[0], cos, sin).astype(BF16)
    ki = _rope(kw_ref[0], cos[:, 0:LANES], sin[:, 0:LANES])
    lane = lax.broadcasted_iota(jnp.int32, ki.shape, 1)
    ki2 = jnp.where(lane < HEAD_DIM, ki, pltpu.roll(ki, HEAD_DIM, 1))
    kito_ref[0] = ki2.T.astype(BF16)


def _prep_even(q, k, v, qi, kw, cos, sin, gq, gk, bd, ts=512):
    b, s, w = q.shape
    assert s % ts == 0
    tile = pl.BlockSpec((1, ts, w), lambda bi, si: (bi, si, 0))
    tab = pl.BlockSpec((ts, w), lambda bi, si: (si, 0))
    const = lambda shape: _resident(shape, lambda bi, si: (0, 0))
    return pl.pallas_call(
        _prep_even_kernel,
        grid=(b, s // ts),
        in_specs=[tile, tile, tile, tile, pl.BlockSpec((1, ts, LANES), lambda bi, si: (bi, si, 0)),
                  tab, tab, const((1, w)), const((1, w)), const((w, w))],
        out_specs=[tile, pl.BlockSpec((1, w, ts), lambda bi, si: (bi, 0, si)), tile, tile,
                   pl.BlockSpec((1, LANES, ts), lambda bi, si: (bi, 0, si))],
        out_shape=[jax.ShapeDtypeStruct((b, s, w), BF16), jax.ShapeDtypeStruct((b, w, s), BF16),
                   jax.ShapeDtypeStruct((b, s, w), BF16), jax.ShapeDtypeStruct((b, s, w), BF16),
                   jax.ShapeDtypeStruct((b, LANES, s), BF16)],
        compiler_params=_params("parallel", "parallel"),
        name="prep_even",
    )(q, k, v, qi, kw, cos, sin, gq, gk, bd)


def _prep_odd_kernel(q_ref, k_ref, v_ref, fl_ref, bf_ref, gq_ref, gk_ref, bd_ref,
                     qo_ref, kto_ref, vo_ref, carry_ref):
    @pl.when(pl.program_id(1) == 0)
    def _():
        carry_ref[...] = jnp.zeros_like(carry_ref)

    q = _head_rms_norm(q_ref[0], gq_ref[...], bd_ref) * QK_SCALE_LOG2
    k = _head_rms_norm(k_ref[0], gk_ref[...], bd_ref)
    v = v_ref[0]
    log_f = -_softplus(-(fl_ref[0] + bf_ref[...]))
    c = _cumsum_rows(log_f) + carry_ref[...]
    carry_ref[...] = c[c.shape[0] - 1:, :]
    c2 = c * LOG2E
    lane = lax.broadcasted_iota(jnp.int32, (c.shape[0], LANES), 1)
    for h in range(N_HEADS):
        p, e = divmod(h, 2)
        pick = lambda t: t[:, p * LANES:(p + 1) * LANES] if e == 0 else pltpu.roll(t[:, p * LANES:(p + 1) * LANES], HEAD_DIM, 1)
        cb = jnp.broadcast_to(c2[:, h:h + 1], (c.shape[0], LANES))
        hi = cb.astype(BF16).astype(F32)
        mid = (cb - hi).astype(BF16).astype(F32)
        lo = cb - hi - mid
        q_bias = jnp.where(lane == HEAD_DIM, hi, jnp.where(lane == HEAD_DIM + 1, mid, jnp.where(
            lane == HEAD_DIM + 2, lo, jnp.where(lane < HEAD_DIM + 6, 1.0, 0.0))))
        k_bias = jnp.where(lane < HEAD_DIM + 3, 1.0, jnp.where(lane == HEAD_DIM + 3, -hi, jnp.where(
            lane == HEAD_DIM + 4, -mid, jnp.where(lane == HEAD_DIM + 5, -lo, 0.0))))
        qo_ref[0, h] = jnp.where(lane < HEAD_DIM, pick(q), q_bias).astype(BF16)
        kto_ref[0, h] = jnp.where(lane < HEAD_DIM, pick(k), k_bias).T.astype(BF16)
        vo_ref[0, h] = jnp.where(lane < HEAD_DIM, pick(v), 1.0).astype(BF16)


def _prep_odd(q, k, v, fl, bf, gq, gk, bd, ts=512):
    b, s, w = q.shape
    assert s % ts == 0
    tile = pl.BlockSpec((1, ts, w), lambda bi, si: (bi, si, 0))
    narrow = pl.BlockSpec((1, ts, LANES), lambda bi, si: (bi, si, 0))
    heads = pl.BlockSpec((1, N_HEADS, ts, LANES), lambda bi, si: (bi, 0, si, 0))
    const = lambda shape: _resident(shape, lambda bi, si: (0, 0))
    return pl.pallas_call(
        _prep_odd_kernel,
        grid=(b, s // ts),
        in_specs=[tile, tile, tile, narrow, const((1, LANES)), const((1, w)), const((1, w)), const((w, w))],
        out_specs=[heads, pl.BlockSpec((1, N_HEADS, LANES, ts), lambda bi, si: (bi, 0, 0, si)), heads],
        out_shape=[jax.ShapeDtypeStruct((b, N_HEADS, s, LANES), BF16),
                   jax.ShapeDtypeStruct((b, N_HEADS, LANES, s), BF16),
                   jax.ShapeDtypeStruct((b, N_HEADS, s, LANES), BF16)],
        scratch_shapes=[pltpu.VMEM((1, LANES), F32)],
        compiler_params=_params("parallel", "arbitrary"),
        name="prep_odd",
    )(q, k, v, fl, bf, gq, gk, bd)


def _head_halves(x_pair):
    lane = lax.broadcasted_iota(jnp.int32, x_pair.shape, 1)
    zero = jnp.zeros_like(x_pair)
    return jnp.where(lane < HEAD_DIM, x_pair, zero), jnp.where(lane >= HEAD_DIM, x_pair, zero)


def _fold_lanes(x, op):
    out = x[:, 0:LANES]
    for c in range(1, x.shape[1] // LANES):
        out = op(out, x[:, c * LANES:(c + 1) * LANES])
    return out


def _online_softmax_block(scores, values, m_ref, l_ref, acc_ref):
    probs, alphas = [], []
    for h, s in enumerate(scores):
        m_old = m_ref[h]
        m_new = jnp.maximum(m_old, jnp.max(_fold_lanes(s, jnp.maximum), axis=1, keepdims=True))
        p = jnp.exp2(s - jnp.tile(m_new, (1, s.shape[1] // LANES)))
        alpha = jnp.exp2(m_old - m_new)
        if l_ref is not None:
            l_ref[h] = alpha * l_ref[h] + jnp.sum(_fold_lanes(p, jnp.add), axis=1, keepdims=True)
        m_ref[h] = m_new
        probs.append(p.astype(BF16))
        alphas.append(alpha)
    for h, (p, alpha) in enumerate(zip(probs, alphas)):
        acc_ref[h] = alpha * acc_ref[h] + jnp.dot(p, values[h], preferred_element_type=F32)


def _merge_pair(acc_ref, l_ref, i0, i1):
    lane = lax.broadcasted_iota(jnp.int32, acc_ref.shape[1:], 1)
    return jnp.where(lane < HEAD_DIM, acc_ref[i0] / l_ref[i0], acc_ref[i1] / l_ref[i1])


def _dsa_kernel(qi_ref, kw_ref, kit_ref, q_ref, kt_ref, v_ref, o_ref,
                sc_ref, m_ref, l_ref, acc_ref, *, tq, tk, topk, max_iters):
    t0 = pl.program_id(1) * tq
    nb = (t0 + tq + tk - 1) // tk
    row_chunk = (t0 + lax.broadcasted_iota(jnp.int32, (tq, 1), 0)) >> CHUNK_SHIFT
    n_visible = (row_chunk + 1) * CHUNK
    kf = float(topk)

    def cols(j):
        return pl.ds(pl.multiple_of(j * tk, tk), tk)

    qi_heads = []
    for p in range(N_PAIRS):
        qi_heads.extend(_head_halves(qi_ref[0, :, p * LANES:(p + 1) * LANES]))
    kw = kw_ref[0]

    def score_block(j, carry):
        kit = kit_ref[0, :, cols(j)]
        score = jnp.zeros((tq, tk), F32)
        for h in range(N_HEADS):
            rel = jnp.maximum(jnp.dot(qi_heads[h], kit, preferred_element_type=F32), 0.0)
            score = score + kw[:, HEAD_DIM + h:HEAD_DIM + h + 1] * rel
        key_chunk = (j * tk + lax.broadcasted_iota(jnp.int32, (1, tk), 1)) >> CHUNK_SHIFT
        sc_ref[:, cols(j)] = jnp.where(key_chunk <= row_chunk, score, -jnp.inf)
        return carry

    lax.fori_loop(0, nb, score_block, 0)

    def count(pred):
        def body(j, cnt):
            return cnt + _fold_lanes(jnp.where(pred(sc_ref[:, cols(j)]), 1.0, 0.0), jnp.add)
        return jnp.sum(lax.fori_loop(0, nb, body, jnp.zeros((tq, LANES), F32)), axis=1, keepdims=True)

    def range_body(j, carry):
        lo, hi = carry
        blk = sc_ref[:, cols(j)]
        lo = jnp.minimum(lo, _fold_lanes(jnp.where(blk == -jnp.inf, jnp.inf, blk), jnp.minimum))
        return lo, jnp.maximum(hi, _fold_lanes(blk, jnp.maximum))

    lo_w, hi_w = lax.fori_loop(0, nb, range_body,
                               (jnp.full((tq, LANES), jnp.inf, F32), jnp.full((tq, LANES), -jnp.inf, F32)))
    lo0 = jnp.min(lo_w, axis=1, keepdims=True)
    hi0 = jnp.max(hi_w, axis=1, keepdims=True)
    c_lo0 = n_visible.astype(F32)
    c_hi0 = count(lambda blk: blk >= hi0)
    short_row = n_visible <= topk
    top_tied = c_hi0 >= kf
    lo0 = jnp.where(top_tied, hi0, lo0)
    c_lo0 = jnp.where(top_tied, c_hi0, c_lo0)
    log_k = jnp.log(kf + 0.5)

    def search_cond(state):
        return jnp.logical_and(state[0] < max_iters, state[5] > 0.0)

    def search_body(state):
        it, lo, hi, c_lo, c_hi, _ = state
        half = lo + (hi - lo) * 0.5
        log_lo = jnp.log(c_lo)
        frac = (log_lo - log_k) / (log_lo - jnp.log(jnp.maximum(c_hi, 0.5)))
        guess = lo + (hi - lo) * jnp.clip(frac, 1.0 / 64, 63.0 / 64)
        inside = jnp.logical_and(guess > lo, guess < hi)
        mid = jnp.where((it & 1) == 0, jnp.where(inside, guess, half), half)
        c_mid = count(lambda blk: blk >= mid)
        enough = c_mid >= kf
        lo, c_lo = jnp.where(enough, mid, lo), jnp.where(enough, c_mid, c_lo)
        hi, c_hi = jnp.where(enough, hi, mid), jnp.where(enough, c_hi, c_mid)
        half = lo + (hi - lo) * 0.5
        finished = jnp.logical_or(jnp.logical_or(c_lo == kf, short_row),
                                  jnp.logical_or(half <= lo, half >= hi))
        return it + 1, lo, hi, c_lo, c_hi, jnp.max(jnp.where(finished, 0.0, 1.0))

    _, lo, _, n_ge, _, _ = lax.while_loop(
        search_cond, search_body, (jnp.int32(0), lo0, hi0, c_lo0, c_hi0, jnp.float32(1.0)))
    thr = jnp.where(short_row, jnp.finfo(F32).min, lo)

    has_extra = jnp.max(jnp.where(jnp.logical_and(n_ge > kf, jnp.logical_not(short_row)), 1.0, 0.0))

    @pl.when(has_extra > 0.0)
    def _():
        keep = kf - count(lambda blk: blk > thr)
        upper = (lax.broadcasted_iota(jnp.int32, (tk, tk), 0)
                 <= lax.broadcasted_iota(jnp.int32, (tk, tk), 1))
        prefix = jnp.where(upper, 1.0, 0.0).astype(BF16)

        def body(j, seen):
            blk = sc_ref[:, cols(j)]
            tie = jnp.where(blk == thr, 1.0, 0.0)
            rank = seen + jnp.dot(tie.astype(BF16), prefix, preferred_element_type=F32)
            drop = jnp.where(rank > keep, tie, 0.0)
            sc_ref[:, cols(j)] = jnp.where(drop > 0.0, -jnp.inf, blk)
            return seen + jnp.sum(tie, axis=1, keepdims=True)

        lax.fori_loop(0, nb, body, jnp.zeros((tq, 1), F32))

    m_ref[...] = jnp.full_like(m_ref, MASKED)
    l_ref[...] = jnp.zeros_like(l_ref)
    acc_ref[...] = jnp.zeros_like(acc_ref)
    q_heads = []
    for p in range(N_PAIRS):
        q_heads.extend(_head_halves(q_ref[0, :, p * LANES:(p + 1) * LANES]))

    def attend_block(j, carry):
        selected = sc_ref[:, cols(j)] >= thr
        scores, values = [], []
        for h in range(N_HEADS):
            p = h // 2
            kt = kt_ref[0, p * LANES:(p + 1) * LANES, cols(j)]
            scores.append(jnp.where(selected, jnp.dot(q_heads[h], kt, preferred_element_type=F32), MASKED))
            values.append(v_ref[0, cols(j), p * LANES:(p + 1) * LANES])
        _online_softmax_block(scores, values, m_ref, l_ref, acc_ref)
        return carry

    lax.fori_loop(0, nb, attend_block, 0)
    for p in range(N_PAIRS):
        o_ref[0, :, p * LANES:(p + 1) * LANES] = _merge_pair(acc_ref, l_ref, 2 * p, 2 * p + 1).astype(o_ref.dtype)


def _dsa(qi, kw, kit, q, kt, v, tq=128, tk=512):
    b, s, w = q.shape
    tk = min(tk, s)
    assert s % tk == 0 and s % tq == 0 and tq % CHUNK == 0
    topk = min(TOPK_MAX, s // 4)
    qtile = pl.BlockSpec((1, tq, w), lambda bi, i: (bi, i, 0))
    return pl.pallas_call(
        functools.partial(_dsa_kernel, tq=tq, tk=tk, topk=topk, max_iters=640),
        grid=(b, s // tq),
        in_specs=[qtile, pl.BlockSpec((1, tq, LANES), lambda bi, i: (bi, i, 0)),
                  _resident((1, LANES, s), lambda bi, i: (bi, 0, 0)),
                  qtile,
                  _resident((1, w, s), lambda bi, i: (bi, 0, 0)),
                  _resident((1, s, w), lambda bi, i: (bi, 0, 0))],
        out_specs=qtile,
        out_shape=jax.ShapeDtypeStruct((b, s, w), BF16),
        scratch_shapes=[pltpu.VMEM((tq, s), F32),
                        pltpu.VMEM((N_HEADS, tq, LANES), F32), pltpu.VMEM((N_HEADS, tq, LANES), F32),
                        pltpu.VMEM((N_HEADS, tq, LANES), F32)],
        compiler_params=_params("parallel", "arbitrary"),
        name="dsa",
    )(qi, kw, kit, q, kt, v)


def _fox_kernel(q_ref, kt_ref, v_ref, o_ref, m_ref, acc_ref, *, tq):
    i = pl.program_id(2)
    m_ref[...] = jnp.full_like(m_ref, MASKED)
    acc_ref[...] = jnp.zeros_like(acc_ref)
    causal = (lax.broadcasted_iota(jnp.int32, (tq, tq), 1) <= lax.broadcasted_iota(jnp.int32, (tq, tq), 0))

    def block(j, diagonal):
        cols = pl.ds(pl.multiple_of(j * tq, tq), tq)
        scores = []
        for e in range(2):
            s = jnp.dot(q_ref[0, e], kt_ref[0, e, :, cols], preferred_element_type=F32)
            scores.append(jnp.where(causal, s, MASKED) if diagonal else s)
        _online_softmax_block(scores, [v_ref[0, e, cols, :] for e in range(2)], m_ref, None, acc_ref)

    def body(j, carry):
        block(j, False)
        return carry

    lax.fori_loop(0, i, body, 0)
    block(i, True)
    lane = lax.broadcasted_iota(jnp.int32, (tq, LANES), 1)
    a0, a1 = acc_ref[0], acc_ref[1]
    o_ref[0] = jnp.where(lane < HEAD_DIM, a0 / pltpu.roll(a0, HEAD_DIM, 1),
                         pltpu.roll(a1, HEAD_DIM, 1) / a1).astype(o_ref.dtype)


def _fox(q, kt, v, tq=512):
    b, nh, s, _ = q.shape
    assert s % tq == 0 and nh == N_HEADS
    return pl.pallas_call(
        functools.partial(_fox_kernel, tq=tq),
        grid=(b, N_PAIRS, s // tq),
        in_specs=[pl.BlockSpec((1, 2, tq, LANES), lambda bi, p, i: (bi, p, i, 0)),
                  pl.BlockSpec((1, 2, LANES, s), lambda bi, p, i: (bi, p, 0, 0)),
                  pl.BlockSpec((1, 2, s, LANES), lambda bi, p, i: (bi, p, 0, 0))],
        out_specs=pl.BlockSpec((1, tq, LANES), lambda bi, p, i: (bi, i, p)),
        out_shape=jax.ShapeDtypeStruct((b, s, N_HEADS * HEAD_DIM), BF16),
        scratch_shapes=[pltpu.VMEM((2, tq, LANES), F32), pltpu.VMEM((2, tq, LANES), F32)],
        compiler_params=_params("parallel", "parallel", "arbitrary"),
        name="fox",
    )(q, kt, v)


def _convmod_kernel(u_ref, cw_ref, cb_ref, g_ref, b_ref, o_ref, xbuf_ref, *, ts, width, halo):
    @pl.when(pl.program_id(1) == 0)
    def _():
        xbuf_ref[0:halo, :] = jnp.zeros((halo, width), F32)

    x = u_ref[0, :, 0:width] * jax.nn.sigmoid(u_ref[0, :, width:2 * width])
    xbuf_ref[halo:halo + ts, :] = x
    base = halo - (CONV_KERNEL - 1)
    y = cb_ref[...] + cw_ref[0:1, :] * xbuf_ref[pl.ds(base, ts), :]
    for j in range(1, CONV_KERNEL):
        y = y + cw_ref[j:j + 1, :] * xbuf_ref[pl.ds(base + j, ts), :]
    xbuf_ref[0:halo, :] = x[ts - halo:ts, :]
    mu = jnp.mean(y, axis=-1, keepdims=True)
    var = jnp.mean(jnp.square(y - mu), axis=-1, keepdims=True)
    z = (y - mu) * lax.rsqrt(var + NORM_EPS) * g_ref[...] + b_ref[...]
    o_ref[0] = (z * jax.nn.sigmoid(z)).astype(o_ref.dtype)


def _convmod(u, conv_w, conv_b, ln_g, ln_b, ts=512, halo=32):
    b, s, w2 = u.shape
    width = w2 // 2
    assert s % ts == 0 and halo >= CONV_KERNEL - 1
    row = lambda v: v.reshape(1, width)
    const = lambda shape: _resident(shape, lambda bi, si: (0, 0))
    return pl.pallas_call(
        functools.partial(_convmod_kernel, ts=ts, width=width, halo=halo),
        grid=(b, s // ts),
        in_specs=[pl.BlockSpec((1, ts, w2), lambda bi, si: (bi, si, 0)),
                  const((CONV_KERNEL, width)), const((1, width)), const((1, width)), const((1, width))],
        out_specs=pl.BlockSpec((1, ts, width), lambda bi, si: (bi, si, 0)),
        out_shape=jax.ShapeDtypeStruct((b, s, width), BF16),
        scratch_shapes=[pltpu.VMEM((ts + halo, width), F32)],
        compiler_params=_params("parallel", "arbitrary"),
        name="conv_module",
    )(u, conv_w, row(conv_b), row(ln_g), row(ln_b))


def _out_ffn_kernel(x_ref, ya_ref, yb_ref, wo_ref, g_ref, wgu_ref, wd_ref, o_ref, *, hidden, th):
    half = ya_ref.shape[1]
    x = x_ref[...] + (jnp.dot(ya_ref[...], wo_ref[0:half, :], preferred_element_type=F32)
                      + jnp.dot(yb_ref[...], wo_ref[half:2 * half, :], preferred_element_type=F32))
    ms = jnp.mean(x * x, axis=-1, keepdims=True)
    h = (x * lax.rsqrt(ms + NORM_EPS) * g_ref[...]).astype(BF16)
    y = x
    for c in range(0, hidden, th):
        gate = jnp.dot(h, wgu_ref[:, c:c + th], preferred_element_type=F32)
        up = jnp.dot(h, wgu_ref[:, hidden + c:hidden + c + th], preferred_element_type=F32)
        act = (gate * jax.nn.sigmoid(gate) * up).astype(BF16)
        y = y + jnp.dot(act, wd_ref[c:c + th, :], preferred_element_type=F32)
    o_ref[...] = y


def _out_ffn(x2, ya, yb, w_out, g, w_gu, w_down, tm=512, th=256):
    n, d = x2.shape
    half = ya.shape[1]
    hidden = w_down.shape[0]
    assert n % tm == 0 and hidden % th == 0
    const = lambda shape: _resident(shape, lambda i: (0, 0))
    return pl.pallas_call(
        functools.partial(_out_ffn_kernel, hidden=hidden, th=th),
        grid=(n // tm,),
        in_specs=[pl.BlockSpec((tm, d), lambda i: (i, 0)),
                  pl.BlockSpec((tm, half), lambda i: (i, 0)), pl.BlockSpec((tm, half), lambda i: (i, 0)),
                  const(w_out.shape), const((1, d)), const(w_gu.shape), const(w_down.shape)],
        out_specs=pl.BlockSpec((tm, d), lambda i: (i, 0)),
        out_shape=jax.ShapeDtypeStruct((n, d), F32),
        compiler_params=_params("parallel"),
        name="out_ffn",
    )(x2, ya, yb, w_out, g.reshape(1, d), w_gu, w_down)


def _rope_tables(s, width):
    inv = ROPE_THETA ** (-jnp.arange(0, HEAD_DIM, 2, dtype=F32) / HEAD_DIM)
    ang = jnp.arange(s, dtype=F32)[:, None] * inv[None, :]
    cos, sin = jnp.cos(ang), jnp.sin(ang)
    reps = width // HEAD_DIM
    return (jnp.tile(jnp.concatenate([cos, cos], axis=-1), (1, reps)),
            jnp.tile(jnp.concatenate([-sin, sin], axis=-1), (1, reps)))


def _block_diag(blocks):
    n, d, _ = blocks.shape
    eye = jnp.eye(n, dtype=blocks.dtype)
    return jnp.einsum('nde,nm->ndme', blocks, eye).reshape(n * d, n * d)


def _pad_cols(w, total):
    return jnp.pad(w, ((0, 0), (0, total - w.shape[1])))


def kernel(x, norm_mix, norm_ffn,
           ev_w_in, ev_conv_w, ev_conv_b, ev_w_r, ev_b_r, ev_w_i, ev_b_i, ev_lam,
           ev_q_norm, ev_k_norm, ev_w_out,
           od_w_in, od_b_f, od_q_norm, od_k_norm, od_conv_w, od_conv_b, od_ln_g, od_ln_b,
           od_w_out, ffn_w_gu, ffn_w_down):
    b, s, d = x.shape
    depth = norm_mix.shape[0]
    w = N_HEADS * HEAD_DIM
    cos, sin = _rope_tables(s, w)
    head_ones = _block_diag(jnp.ones((N_HEADS, HEAD_DIM, HEAD_DIM), BF16))
    tile_gain = lambda g: jnp.tile(g, N_HEADS).reshape(1, w)
    x2 = x.reshape(b * s, d)
    seq = lambda t: t.reshape(b, s, t.shape[-1])
    for l in range(depth):
        j = l // 2
        if l % 2 == 0:
            w_in = _pad_cols(ev_w_in[j], 2 * w + 4 * w + LANES).astype(BF16)
            xg, q, k, v, qi, kw = _norm_matmul(x2, norm_mix[l], w_in, (2 * w, w, w, w, w, LANES))
            ya = _lru(seq(xg), ev_conv_w[j], ev_conv_b[j],
                      _block_diag(ev_w_r[j]).astype(BF16), ev_b_r[j],
                      _block_diag(ev_w_i[j]).astype(BF16), ev_b_i[j], ev_lam[j])
            qh, kt, vh, qih, kit = _prep_even(seq(q), seq(k), seq(v), seq(qi), seq(kw), cos, sin,
                                              tile_gain(ev_q_norm[j]), tile_gain(ev_k_norm[j]), head_ones)
            yb = _dsa(qih, seq(kw), kit, qh, kt, vh)
            w_out = ev_w_out[j]
        else:
            wi = od_w_in[j]
            w_in = jnp.concatenate([wi[:, 0:3 * w], wi[:, 3 * w + N_HEADS:],
                                    _pad_cols(wi[:, 3 * w:3 * w + N_HEADS], LANES)], axis=1).astype(BF16)
            q, k, v, u, fl = _norm_matmul(x2, norm_mix[l], w_in, (w, w, w, 2 * w, LANES))
            qh, kt, vh = _prep_odd(seq(q), seq(k), seq(v), seq(fl),
                                   _pad_cols(od_b_f[j].reshape(1, N_HEADS), LANES),
                                   tile_gain(od_q_norm[j]), tile_gain(od_k_norm[j]), head_ones)
            ya = _fox(qh, kt, vh, tq=min(512, s))
            yb = _convmod(seq(u), od_conv_w[j], od_conv_b[j], od_ln_g[j], od_ln_b[j])
            w_out = od_w_out[j]
        x2 = _out_ffn(x2, ya.reshape(b * s, w), yb.reshape(b * s, w), w_out.astype(BF16),
                      norm_ffn[l], ffn_w_gu[l].astype(BF16), ffn_w_down[l].astype(BF16))
    return x2.reshape(b, s, d)
```

```python
import functools

import jax
import jax.numpy as jnp
from jax import lax
from jax.experimental import pallas as pl
from jax.experimental.pallas import tpu as pltpu

F32 = jnp.float32
BF16 = jnp.bfloat16

HEAD_DIM = 64
N_HEADS = 8
N_PAIRS = N_HEADS // 2
LANES = 128
CHUNK = 64
CHUNK_SHIFT = 6
TOPK_MAX = 256
ROPE_THETA = 10000.0
NORM_EPS = 1e-6
LRU_C = 8.0
LRU_CONV = 4
CONV_KERNEL = 31
MASKED = -1e30
LOG2E = 1.4426950408889634
QK_SCALE_LOG2 = HEAD_DIM ** -0.5 * LOG2E
VMEM_LIMIT = 56 * 1024 * 1024

assert CHUNK == 1 << CHUNK_SHIFT


def _params(*sem):
    return pltpu.CompilerParams(dimension_semantics=sem, vmem_limit_bytes=VMEM_LIMIT)


def _resident(shape, index_map):
    return pl.BlockSpec(shape, index_map, pipeline_mode=pl.Buffered(1))


def _norm_matmul_kernel(x_ref, g_ref, w_ref, *o_refs, widths):
    x = x_ref[...]
    ms = jnp.mean(x * x, axis=-1, keepdims=True)
    h = (x * lax.rsqrt(ms + NORM_EPS) * g_ref[...]).astype(BF16)
    off = 0
    for o_ref, wd in zip(o_refs, widths):
        o_ref[...] = jnp.dot(h, w_ref[:, off:off + wd], preferred_element_type=F32)
        off += wd


def _norm_matmul(x2, g, w, widths, tm=512):
    n, d = x2.shape
    assert n % tm == 0 and sum(widths) == w.shape[1]
    return pl.pallas_call(
        functools.partial(_norm_matmul_kernel, widths=widths),
        grid=(n // tm,),
        in_specs=[pl.BlockSpec((tm, d), lambda i: (i, 0)),
                  _resident((1, d), lambda i: (0, 0)),
                  _resident(w.shape, lambda i: (0, 0))],
        out_specs=[pl.BlockSpec((tm, wd), lambda i: (i, 0)) for wd in widths],
        out_shape=[jax.ShapeDtypeStruct((n, wd), F32) for wd in widths],
        compiler_params=_params("parallel"),
        name="norm_matmul",
    )(x2, g.reshape(1, d), w)


def _shift_rows(x, d, fill):
    row = lax.broadcasted_iota(jnp.int32, x.shape, 0)
    return jnp.where(row >= d, pltpu.roll(x, d, 0), fill)


def _linear_scan_rows(a, u):
    d = 1
    while d < a.shape[0]:
        u = a * _shift_rows(u, d, 0.0) + u
        a = a * _shift_rows(a, d, 1.0)
        d *= 2
    return a, u


def _cumsum_rows(x):
    d = 1
    while d < x.shape[0]:
        x = x + _shift_rows(x, d, 0.0)
        d *= 2
    return x


def _softplus(x):
    return jnp.maximum(x, 0.0) + jnp.log1p(jnp.exp(-jnp.abs(x)))


def _lru_kernel(xg_ref, cw_ref, cb_ref, wr_ref, br_ref, wi_ref, bi_ref, lam_ref, o_ref,
                xbuf_ref, h_ref, *, ts, width):
    @pl.when(pl.program_id(1) == 0)
    def _():
        xbuf_ref[0:8, :] = jnp.zeros((8, width), F32)
        h_ref[...] = jnp.zeros_like(h_ref)

    xa = xg_ref[0, :, 0:width]
    ga = xg_ref[0, :, width:2 * width]
    xbuf_ref[8:8 + ts, :] = xa
    xc = cb_ref[...] + cw_ref[0:1, :] * xbuf_ref[pl.ds(8 - (LRU_CONV - 1), ts), :]
    for j in range(1, LRU_CONV):
        xc = xc + cw_ref[j:j + 1, :] * xbuf_ref[pl.ds(8 - (LRU_CONV - 1) + j, ts), :]
    xbuf_ref[0:8, :] = xa[ts - 8:ts, :]

    xb = xc.astype(BF16)
    r = jax.nn.sigmoid(jnp.dot(xb, wr_ref[...], preferred_element_type=F32) + br_ref[...])
    gate = jax.nn.sigmoid(jnp.dot(xb, wi_ref[...], preferred_element_type=F32) + bi_ref[...])
    log_a = -LRU_C * r * _softplus(-lam_ref[...])
    a = jnp.exp(log_a)
    th = jnp.tanh(log_a)
    u = jnp.sqrt(-2.0 * th / (1.0 - th)) * (gate * xc)
    a_cum, h = _linear_scan_rows(a, u)
    h = h + a_cum * h_ref[...]
    h_ref[...] = h[ts - 1:ts, :]
    o_ref[0] = (h * jax.nn.gelu(ga, approximate=True)).astype(o_ref.dtype)


def _lru(xg, conv_w, conv_b, w_r, b_r, w_i, b_i, lam, ts=256):
    b, s, w2 = xg.shape
    width = w2 // 2
    assert s % ts == 0
    row = lambda v: v.reshape(1, width)
    const = lambda shape: _resident(shape, lambda bi, si: (0, 0))
    return pl.pallas_call(
        functools.partial(_lru_kernel, ts=ts, width=width),
        grid=(b, s // ts),
        in_specs=[pl.BlockSpec((1, ts, w2), lambda bi, si: (bi, si, 0)),
                  const((LRU_CONV, width)), const((1, width)),
                  const((width, width)), const((1, width)),
                  const((width, width)), const((1, width)), const((1, width))],
        out_specs=pl.BlockSpec((1, ts, width), lambda bi, si: (bi, si, 0)),
        out_shape=jax.ShapeDtypeStruct((b, s, width), BF16),
        scratch_shapes=[pltpu.VMEM((ts + 8, width), F32), pltpu.VMEM((1, width), F32)],
        compiler_params=_params("parallel", "arbitrary"),
        name="rg_lru",
    )(xg, conv_w, row(conv_b), w_r, row(b_r), w_i, row(b_i), row(lam))


def _head_mean_square(x, bd_ref):
    x2 = x * x
    hi = x2.astype(BF16)
    r1 = x2 - hi.astype(F32)
    mid = r1.astype(BF16)
    lo = (r1 - mid.astype(F32)).astype(BF16)
    bd = bd_ref[...]
    tot = (jnp.dot(hi, bd, preferred_element_type=F32) + jnp.dot(mid, bd, preferred_element_type=F32)
           + jnp.dot(lo, bd, preferred_element_type=F32))
    return tot * (1.0 / HEAD_DIM)


def _head_rms_norm(x, g, bd_ref):
    return x * lax.rsqrt(_head_mean_square(x, bd_ref) + NORM_EPS) * g


def _rope(x, cos, sin_signed):
    n = x.shape[1]
    lane = lax.broadcasted_iota(jnp.int32, x.shape, 1)
    first_half = (lane & (HEAD_DIM - 1)) < HEAD_DIM // 2
    partner = jnp.where(first_half, pltpu.roll(x, n - HEAD_DIM // 2, 1), pltpu.roll(x, HEAD_DIM // 2, 1))
    return x * cos + partner * sin_signed


def _prep_even_kernel(q_ref, k_ref, v_ref, qi_ref, kw_ref, cos_ref, sin_ref, gq_ref, gk_ref, bd_ref,
                      qo_ref, kto_ref, vo_ref, qio_ref, kito_ref):
    cos = cos_ref[...]
    sin = sin_ref[...]
    q = _rope(_head_rms_norm(q_ref[0], gq_ref[...], bd_ref), cos, sin)
    qo_ref[0] = (q * QK_SCALE_LOG2).astype(BF16)
    k = _rope(_head_rms_norm(k_ref[0], gk_ref[...], bd_ref), cos, sin)
    kto_ref[0] = k.T.astype(BF16)
    vo_ref[0] = v_ref[0].astype(BF16)
    qio_ref[0] = _rope(qi_ref[0], cos, sin).astype(BF16)
    ki = _rope(kw_ref[0], cos[:, 0:LANES], sin[:, 0:LANES])
    lane = lax.broadcasted_iota(jnp.int32, ki.shape, 1)
    ki2 = jnp.where(lane < HEAD_DIM, ki, pltpu.roll(ki, HEAD_DIM, 1))
    kito_ref[0] = ki2.T.astype(BF16)


def _prep_even(q, k, v, qi, kw, cos, sin, gq, gk, bd, ts=512):
    b, s, w = q.shape
    assert s % ts == 0
    tile = pl.BlockSpec((1, ts, w), lambda bi, si: (bi, si, 0))
    tab = pl.BlockSpec((ts, w), lambda bi, si: (si, 0))
    const = lambda shape: _resident(shape, lambda bi, si: (0, 0))
    return pl.pallas_call(
        _prep_even_kernel,
        grid=(b, s // ts),
        in_specs=[tile, tile, tile, tile, pl.BlockSpec((1, ts, LANES), lambda bi, si: (bi, si, 0)),
                  tab, tab, const((1, w)), const((1, w)), const((w, w))],
        out_specs=[tile, pl.BlockSpec((1, w, ts), lambda bi, si: (bi, 0, si)), tile, tile,
                   pl.BlockSpec((1, LANES, ts), lambda bi, si: (bi, 0, si))],
        out_shape=[jax.ShapeDtypeStruct((b, s, w), BF16), jax.ShapeDtypeStruct((b, w, s), BF16),
                   jax.ShapeDtypeStruct((b, s, w), BF16), jax.ShapeDtypeStruct((b, s, w), BF16),
                   jax.ShapeDtypeStruct((b, LANES, s), BF16)],
        compiler_params=_params("parallel", "parallel"),
        name="prep_even",
    )(q, k, v, qi, kw, cos, sin, gq, gk, bd)


def _prep_odd_kernel(q_ref, k_ref, v_ref, fl_ref, bf_ref, gq_ref, gk_ref, bd_ref,
                     qo_ref, kto_ref, vo_ref, carry_ref):
    @pl.when(pl.program_id(1) == 0)
    def _():
        carry_ref[...] = jnp.zeros_like(carry_ref)

    q = _head_rms_norm(q_ref[0], gq_ref[...], bd_ref) * QK_SCALE_LOG2
    k = _head_rms_norm(k_ref[0], gk_ref[...], bd_ref)
    v = v_ref[0]
    log_f = -_softplus(-(fl_ref[0] + bf_ref[...]))
    c = _cumsum_rows(log_f) + carry_ref[...]
    carry_ref[...] = c[c.shape[0] - 1:, :]
    c2 = c * LOG2E
    lane = lax.broadcasted_iota(jnp.int32, (c.shape[0], LANES), 1)
    for h in range(N_HEADS):
        p, e = divmod(h, 2)
        pick = lambda t: t[:, p * LANES:(p + 1) * LANES] if e == 0 else pltpu.roll(t[:, p * LANES:(p + 1) * LANES], HEAD_DIM, 1)
        cb = jnp.broadcast_to(c2[:, h:h + 1], (c.shape[0], LANES))
        hi = cb.astype(BF16).astype(F32)
        mid = (cb - hi).astype(BF16).astype(F32)
        lo = cb - hi - mid
        q_bias = jnp.where(lane == HEAD_DIM, hi, jnp.where(lane == HEAD_DIM + 1, mid, jnp.where(
            lane == HEAD_DIM + 2, lo, jnp.where(lane < HEAD_DIM + 6, 1.0, 0.0))))
        k_bias = jnp.where(lane < HEAD_DIM + 3, 1.0, jnp.where(lane == HEAD_DIM + 3, -hi, jnp.where(
            lane == HEAD_DIM + 4, -mid, jnp.where(lane == HEAD_DIM + 5, -lo, 0.0))))
        qo_ref[0, h] = jnp.where(lane < HEAD_DIM, pick(q), q_bias).astype(BF16)
        kto_ref[0, h] = jnp.where(lane < HEAD_DIM, pick(k), k_bias).T.astype(BF16)
        vo_ref[0, h] = jnp.where(lane < HEAD_DIM, pick(v), 1.0).astype(BF16)


def _prep_odd(q, k, v, fl, bf, gq, gk, bd, ts=512):
    b, s, w = q.shape
    assert s % ts == 0
    tile = pl.BlockSpec((1, ts, w), lambda bi, si: (bi, si, 0))
    narrow = pl.BlockSpec((1, ts, LANES), lambda bi, si: (bi, si, 0))
    heads = pl.BlockSpec((1, N_HEADS, ts, LANES), lambda bi, si: (bi, 0, si, 0))
    const = lambda shape: _resident(shape, lambda bi, si: (0, 0))
    return pl.pallas_call(
        _prep_odd_kernel,
        grid=(b, s // ts),
        in_specs=[tile, tile, tile, narrow, const((1, LANES)), const((1, w)), const((1, w)), const((w, w))],
        out_specs=[heads, pl.BlockSpec((1, N_HEADS, LANES, ts), lambda bi, si: (bi, 0, 0, si)), heads],
        out_shape=[jax.ShapeDtypeStruct((b, N_HEADS, s, LANES), BF16),
                   jax.ShapeDtypeStruct((b, N_HEADS, LANES, s), BF16),
                   jax.ShapeDtypeStruct((b, N_HEADS, s, LANES), BF16)],
        scratch_shapes=[pltpu.VMEM((1, LANES), F32)],
        compiler_params=_params("parallel", "arbitrary"),
        name="prep_odd",
    )(q, k, v, fl, bf, gq, gk, bd)


def _head_halves(x_pair):
    lane = lax.broadcasted_iota(jnp.int32, x_pair.shape, 1)
    zero = jnp.zeros_like(x_pair)
    return jnp.where(lane < HEAD_DIM, x_pair, zero), jnp.where(lane >= HEAD_DIM, x_pair, zero)


def _fold_lanes(x, op):
    out = x[:, 0:LANES]
    for c in range(1, x.shape[1] // LANES):
        out = op(out, x[:, c * LANES:(c + 1) * LANES])
    return out


def _online_softmax_block(scores, values, m_ref, l_ref, acc_ref):
    probs, alphas = [], []
    for h, s in enumerate(scores):
        m_old = m_ref[h]
        m_new = jnp.maximum(m_old, jnp.max(_fold_lanes(s, jnp.maximum), axis=1, keepdims=True))
        p = jnp.exp2(s - jnp.tile(m_new, (1, s.shape[1] // LANES)))
        alpha = jnp.exp2(m_old - m_new)
        if l_ref is not None:
            l_ref[h] = alpha * l_ref[h] + jnp.sum(_fold_lanes(p, jnp.add), axis=1, keepdims=True)
        m_ref[h] = m_new
        probs.append(p.astype(BF16))
        alphas.append(alpha)
    for h, (p, alpha) in enumerate(zip(probs, alphas)):
        acc_ref[h] = alpha * acc_ref[h] + jnp.dot(p, values[h], preferred_element_type=F32)


def _to_ordinal(x):
    bits = lax.bitcast_convert_type(x, jnp.int32)
    return jnp.where(bits < 0, bits ^ jnp.int32(0x7FFFFFFF), bits)


def _from_ordinal(o):
    return lax.bitcast_convert_type(jnp.where(o < 0, o ^ jnp.int32(0x7FFFFFFF), o), F32)


def _ordinal_midpoint(lo, hi):
    a, b = _to_ordinal(lo), _to_ordinal(hi)
    return _from_ordinal((a >> 1) + (b >> 1) + (a & b & 1))


def _next_float_up(x):
    return _from_ordinal(_to_ordinal(x) + 1)


def _merge_pair(acc_ref, l_ref, i0, i1):
    lane = lax.broadcasted_iota(jnp.int32, acc_ref.shape[1:], 1)
    return jnp.where(lane < HEAD_DIM, acc_ref[i0] / l_ref[i0], acc_ref[i1] / l_ref[i1])


def _dsa_kernel(qi_ref, kw_ref, kit_ref, q_ref, kt_ref, v_ref, o_ref,
                sc_ref, m_ref, l_ref, acc_ref, *, tq, tk, topk, max_iters):
    t0 = pl.program_id(1) * tq
    nb = (t0 + tq + tk - 1) // tk
    row_chunk = (t0 + lax.broadcasted_iota(jnp.int32, (tq, 1), 0)) >> CHUNK_SHIFT
    n_visible = (row_chunk + 1) * CHUNK
    kf = float(topk)

    def cols(j):
        return pl.ds(pl.multiple_of(j * tk, tk), tk)

    qi_heads = []
    for p in range(N_PAIRS):
        qi_heads.extend(_head_halves(qi_ref[0, :, p * LANES:(p + 1) * LANES]))
    kw = kw_ref[0]

    def score_block(j, carry):
        kit = kit_ref[0, :, cols(j)]
        score = jnp.zeros((tq, tk), F32)
        for h in range(N_HEADS):
            rel = jnp.maximum(jnp.dot(qi_heads[h], kit, preferred_element_type=F32), 0.0)
            score = score + kw[:, HEAD_DIM + h:HEAD_DIM + h + 1] * rel
        key_chunk = (j * tk + lax.broadcasted_iota(jnp.int32, (1, tk), 1)) >> CHUNK_SHIFT
        sc_ref[:, cols(j)] = jnp.where(key_chunk <= row_chunk, score, -jnp.inf)
        return carry

    lax.fori_loop(0, nb, score_block, 0)

    def count(pred):
        def body(j, cnt):
            return cnt + _fold_lanes(jnp.where(pred(sc_ref[:, cols(j)]), 1.0, 0.0), jnp.add)
        return jnp.sum(lax.fori_loop(0, nb, body, jnp.zeros((tq, LANES), F32)), axis=1, keepdims=True)

    def range_body(j, carry):
        lo, hi = carry
        blk = sc_ref[:, cols(j)]
        lo = jnp.minimum(lo, _fold_lanes(jnp.where(blk == -jnp.inf, jnp.inf, blk), jnp.minimum))
        return lo, jnp.maximum(hi, _fold_lanes(blk, jnp.maximum))

    lo_w, hi_w = lax.fori_loop(0, nb, range_body,
                               (jnp.full((tq, LANES), jnp.inf, F32), jnp.full((tq, LANES), -jnp.inf, F32)))
    lo0 = jnp.min(lo_w, axis=1, keepdims=True)
    hi0 = _next_float_up(jnp.max(hi_w, axis=1, keepdims=True))
    short_row = n_visible <= topk
    log_k = jnp.log(kf + 0.5)

    def open_rows(lo, hi, c_lo):
        finished = jnp.logical_or(jnp.logical_or(c_lo == kf, short_row),
                                  _to_ordinal(_ordinal_midpoint(lo, hi)) == _to_ordinal(lo))
        return jnp.max(jnp.where(finished, 0.0, 1.0))

    def search_cond(state):
        return jnp.logical_and(state[0] < max_iters, state[-1] > 0.0)

    def search_body(state):
        it, lo, hi, c_lo, f_lo, f_hi, side, _ = state
        guess = lo + (hi - lo) * jnp.clip(f_lo / (f_lo - f_hi), 1.0 / 64, 63.0 / 64)
        inside = jnp.logical_and(guess > lo, guess < hi)
        half = _ordinal_midpoint(lo, hi)
        mid = jnp.where(it % 3 == 2, half, jnp.where(inside, guess, half))
        c_mid = count(lambda blk: blk >= mid)
        enough = c_mid >= kf
        f_mid = jnp.log(jnp.maximum(c_mid, 0.5)) - log_k
        f_hi = jnp.where(enough, jnp.where(side > 0.0, f_hi * 0.5, f_hi), f_mid)
        f_lo = jnp.where(enough, f_mid, jnp.where(side < 0.0, f_lo * 0.5, f_lo))
        lo, c_lo = jnp.where(enough, mid, lo), jnp.where(enough, c_mid, c_lo)
        hi = jnp.where(enough, hi, mid)
        return it + 1, lo, hi, c_lo, f_lo, f_hi, jnp.where(enough, 1.0, -1.0), open_rows(lo, hi, c_lo)

    c_lo0 = n_visible.astype(F32)
    state0 = (jnp.int32(0), lo0, hi0, c_lo0, jnp.log(c_lo0) - log_k, jnp.log(0.5) - log_k + jnp.zeros_like(lo0),
              jnp.zeros_like(lo0), open_rows(lo0, hi0, c_lo0))
    _, lo, _, n_ge, _, _, _, _ = lax.while_loop(search_cond, search_body, state0)
    thr = jnp.where(short_row, jnp.finfo(F32).min, lo)

    has_extra = jnp.max(jnp.where(jnp.logical_and(n_ge > kf, jnp.logical_not(short_row)), 1.0, 0.0))

    @pl.when(has_extra > 0.0)
    def _():
        keep = kf - count(lambda blk: blk > thr)
        upper = (lax.broadcasted_iota(jnp.int32, (tk, tk), 0)
                 <= lax.broadcasted_iota(jnp.int32, (tk, tk), 1))
        prefix = jnp.where(upper, 1.0, 0.0).astype(BF16)

        def body(j, seen):
            blk = sc_ref[:, cols(j)]
            tie = jnp.where(blk == thr, 1.0, 0.0)
            rank = seen + jnp.dot(tie.astype(BF16), prefix, preferred_element_type=F32)
            drop = jnp.where(rank > keep, tie, 0.0)
            sc_ref[:, cols(j)] = jnp.where(drop > 0.0, -jnp.inf, blk)
            return seen + jnp.sum(tie, axis=1, keepdims=True)

        lax.fori_loop(0, nb, body, jnp.zeros((tq, 1), F32))

    m_ref[...] = jnp.full_like(m_ref, MASKED)
    l_ref[...] = jnp.zeros_like(l_ref)
    acc_ref[...] = jnp.zeros_like(acc_ref)
    q_heads = []
    for p in range(N_PAIRS):
        q_heads.extend(_head_halves(q_ref[0, :, p * LANES:(p + 1) * LANES]))

    def attend_block(j, carry):
        bias = jnp.where(sc_ref[:, cols(j)] >= thr, 0.0, MASKED)
        scores, values = [], []
        for h in range(N_HEADS):
            p = h // 2
            kt = kt_ref[0, p * LANES:(p + 1) * LANES, cols(j)]
            scores.append(jnp.dot(q_heads[h], kt, preferred_element_type=F32) + bias)
            values.append(v_ref[0, cols(j), p * LANES:(p + 1) * LANES])
        _online_softmax_block(scores, values, m_ref, l_ref, acc_ref)
        return carry

    lax.fori_loop(0, nb, attend_block, 0)
    for p in range(N_PAIRS):
        o_ref[0, :, p * LANES:(p + 1) * LANES] = _merge_pair(acc_ref, l_ref, 2 * p, 2 * p + 1).astype(o_ref.dtype)


def _dsa(qi, kw, kit, q, kt, v, tq=128, tk=512):
    b, s, w = q.shape
    tk = min(tk, s)
    assert s % tk == 0 and s % tq == 0 and tq % CHUNK == 0
    topk = min(TOPK_MAX, s // 4)
    qtile = pl.BlockSpec((1, tq, w), lambda bi, i: (bi, i, 0))
    return pl.pallas_call(
        functools.partial(_dsa_kernel, tq=tq, tk=tk, topk=topk, max_iters=640),
        grid=(b, s // tq),
        in_specs=[qtile, pl.BlockSpec((1, tq, LANES), lambda bi, i: (bi, i, 0)),
                  _resident((1, LANES, s), lambda bi, i: (bi, 0, 0)),
                  qtile,
                  _resident((1, w, s), lambda bi, i: (bi, 0, 0)),
                  _resident((1, s, w), lambda bi, i: (bi, 0, 0))],
        out_specs=qtile,
        out_shape=jax.ShapeDtypeStruct((b, s, w), BF16),
        scratch_shapes=[pltpu.VMEM((tq, s), F32),
                        pltpu.VMEM((N_HEADS, tq, LANES), F32), pltpu.VMEM((N_HEADS, tq, LANES), F32),
                        pltpu.VMEM((N_HEADS, tq, LANES), F32)],
        compiler_params=_params("parallel", "arbitrary"),
        name="dsa",
    )(qi, kw, kit, q, kt, v)


def _fox_kernel(q_ref, kt_ref, v_ref, o_ref, m_ref, acc_ref, *, tq, heads):
    i = pl.program_id(2)
    m_ref[...] = jnp.full_like(m_ref, MASKED)
    acc_ref[...] = jnp.zeros_like(acc_ref)
    causal = (lax.broadcasted_iota(jnp.int32, (tq, tq), 1) <= lax.broadcasted_iota(jnp.int32, (tq, tq), 0))

    def block(j, diagonal):
        cols = pl.ds(pl.multiple_of(j * tq, tq), tq)
        scores = []
        for e in range(heads):
            s = jnp.dot(q_ref[0, e], kt_ref[0, e, :, cols], preferred_element_type=F32)
            scores.append(jnp.where(causal, s, MASKED) if diagonal else s)
        _online_softmax_block(scores, [v_ref[0, e, cols, :] for e in range(heads)], m_ref, None, acc_ref)

    def body(j, carry):
        block(j, False)
        return carry

    lax.fori_loop(0, i, body, 0)
    block(i, True)
    lane = lax.broadcasted_iota(jnp.int32, (tq, LANES), 1)
    for p in range(heads // 2):
        a0, a1 = acc_ref[2 * p], acc_ref[2 * p + 1]
        o_ref[0, :, p * LANES:(p + 1) * LANES] = jnp.where(
            lane < HEAD_DIM, a0 / pltpu.roll(a0, HEAD_DIM, 1), pltpu.roll(a1, HEAD_DIM, 1) / a1).astype(o_ref.dtype)


def _fox(q, kt, v, tq=512, heads=4):
    b, nh, s, _ = q.shape
    assert s % tq == 0 and nh == N_HEADS and nh % heads == 0 and heads % 2 == 0
    out_w = heads * HEAD_DIM
    return pl.pallas_call(
        functools.partial(_fox_kernel, tq=tq, heads=heads),
        grid=(b, nh // heads, s // tq),
        in_specs=[pl.BlockSpec((1, heads, tq, LANES), lambda bi, g, i: (bi, g, i, 0)),
                  _resident((1, heads, LANES, s), lambda bi, g, i: (bi, g, 0, 0)),
                  _resident((1, heads, s, LANES), lambda bi, g, i: (bi, g, 0, 0))],
        out_specs=pl.BlockSpec((1, tq, out_w), lambda bi, g, i: (bi, i, g)),
        out_shape=jax.ShapeDtypeStruct((b, s, N_HEADS * HEAD_DIM), BF16),
        scratch_shapes=[pltpu.VMEM((heads, tq, LANES), F32), pltpu.VMEM((heads, tq, LANES), F32)],
        compiler_params=_params("parallel", "parallel", "arbitrary"),
        name="fox",
    )(q, kt, v)


def _convmod_kernel(u_ref, cw_ref, cb_ref, g_ref, b_ref, o_ref, xbuf_ref, *, ts, width, halo):
    @pl.when(pl.program_id(1) == 0)
    def _():
        xbuf_ref[0:halo, :] = jnp.zeros((halo, width), F32)

    x = u_ref[0, :, 0:width] * jax.nn.sigmoid(u_ref[0, :, width:2 * width])
    xbuf_ref[halo:halo + ts, :] = x
    base = halo - (CONV_KERNEL - 1)
    y = cb_ref[...] + cw_ref[0:1, :] * xbuf_ref[pl.ds(base, ts), :]
    for j in range(1, CONV_KERNEL):
        y = y + cw_ref[j:j + 1, :] * xbuf_ref[pl.ds(base + j, ts), :]
    xbuf_ref[0:halo, :] = x[ts - halo:ts, :]
    mu = jnp.mean(y, axis=-1, keepdims=True)
    var = jnp.mean(jnp.square(y - mu), axis=-1, keepdims=True)
    z = (y - mu) * lax.rsqrt(var + NORM_EPS) * g_ref[...] + b_ref[...]
    o_ref[0] = (z * jax.nn.sigmoid(z)).astype(o_ref.dtype)


def _convmod(u, conv_w, conv_b, ln_g, ln_b, ts=512, halo=32):
    b, s, w2 = u.shape
    width = w2 // 2
    assert s % ts == 0 and halo >= CONV_KERNEL - 1
    row = lambda v: v.reshape(1, width)
    const = lambda shape: _resident(shape, lambda bi, si: (0, 0))
    return pl.pallas_call(
        functools.partial(_convmod_kernel, ts=ts, width=width, halo=halo),
        grid=(b, s // ts),
        in_specs=[pl.BlockSpec((1, ts, w2), lambda bi, si: (bi, si, 0)),
                  const((CONV_KERNEL, width)), const((1, width)), const((1, width)), const((1, width))],
        out_specs=pl.BlockSpec((1, ts, width), lambda bi, si: (bi, si, 0)),
        out_shape=jax.ShapeDtypeStruct((b, s, width), BF16),
        scratch_shapes=[pltpu.VMEM((ts + halo, width), F32)],
        compiler_params=_params("parallel", "arbitrary"),
        name="conv_module",
    )(u, conv_w, row(conv_b), row(ln_g), row(ln_b))


def _out_ffn_kernel(x_ref, ya_ref, yb_ref, wo_ref, g_ref, wgu_ref, wd_ref, o_ref, *, hidden, th):
    half = ya_ref.shape[1]
    x = x_ref[...] + (jnp.dot(ya_ref[...], wo_ref[0:half, :], preferred_element_type=F32)
                      + jnp.dot(yb_ref[...], wo_ref[half:2 * half, :], preferred_element_type=F32))
    ms = jnp.mean(x * x, axis=-1, keepdims=True)
    h = (x * lax.rsqrt(ms + NORM_EPS) * g_ref[...]).astype(BF16)
    y = x
    for c in range(0, hidden, th):
        gate = jnp.dot(h, wgu_ref[:, c:c + th], preferred_element_type=F32)
        up = jnp.dot(h, wgu_ref[:, hidden + c:hidden + c + th], preferred_element_type=F32)
        act = (gate * jax.nn.sigmoid(gate) * up).astype(BF16)
        y = y + jnp.dot(act, wd_ref[c:c + th, :], preferred_element_type=F32)
    o_ref[...] = y


def _out_ffn(x2, ya, yb, w_out, g, w_gu, w_down, tm=512, th=256):
    n, d = x2.shape
    half = ya.shape[1]
    hidden = w_down.shape[0]
    assert n % tm == 0 and hidden % th == 0
    const = lambda shape: _resident(shape, lambda i: (0, 0))
    return pl.pallas_call(
        functools.partial(_out_ffn_kernel, hidden=hidden, th=th),
        grid=(n // tm,),
        in_specs=[pl.BlockSpec((tm, d), lambda i: (i, 0)),
                  pl.BlockSpec((tm, half), lambda i: (i, 0)), pl.BlockSpec((tm, half), lambda i: (i, 0)),
                  const(w_out.shape), const((1, d)), const(w_gu.shape), const(w_down.shape)],
        out_specs=pl.BlockSpec((tm, d), lambda i: (i, 0)),
        out_shape=jax.ShapeDtypeStruct((n, d), F32),
        compiler_params=_params("parallel"),
        name="out_ffn",
    )(x2, ya, yb, w_out, g.reshape(1, d), w_gu, w_down)


def _rope_tables(s, width):
    inv = ROPE_THETA ** (-jnp.arange(0, HEAD_DIM, 2, dtype=F32) / HEAD_DIM)
    ang = jnp.arange(s, dtype=F32)[:, None] * inv[None, :]
    cos, sin = jnp.cos(ang), jnp.sin(ang)
    reps = width // HEAD_DIM
    return (jnp.tile(jnp.concatenate([cos, cos], axis=-1), (1, reps)),
            jnp.tile(jnp.concatenate([-sin, sin], axis=-1), (1, reps)))


def _block_diag(blocks):
    n, d, _ = blocks.shape
    eye = jnp.eye(n, dtype=blocks.dtype)
    return jnp.einsum('nde,nm->ndme', blocks, eye).reshape(n * d, n * d)


def _pad_cols(w, total):
    return jnp.pad(w, ((0, 0), (0, total - w.shape[1])))


def kernel(x, norm_mix, norm_ffn,
           ev_w_in, ev_conv_w, ev_conv_b, ev_w_r, ev_b_r, ev_w_i, ev_b_i, ev_lam,
           ev_q_norm, ev_k_norm, ev_w_out,
           od_w_in, od_b_f, od_q_norm, od_k_norm, od_conv_w, od_conv_b, od_ln_g, od_ln_b,
           od_w_out, ffn_w_gu, ffn_w_down):
    b, s, d = x.shape
    depth = norm_mix.shape[0]
    w = N_HEADS * HEAD_DIM
    cos, sin = _rope_tables(s, w)
    head_ones = _block_diag(jnp.ones((N_HEADS, HEAD_DIM, HEAD_DIM), BF16))
    tile_gain = lambda g: jnp.tile(g, N_HEADS).reshape(1, w)
    x2 = x.reshape(b * s, d)
    seq = lambda t: t.reshape(b, s, t.shape[-1])
    for l in range(depth):
        j = l // 2
        if l % 2 == 0:
            w_in = _pad_cols(ev_w_in[j], 2 * w + 4 * w + LANES).astype(BF16)
            xg, q, k, v, qi, kw = _norm_matmul(x2, norm_mix[l], w_in, (2 * w, w, w, w, w, LANES))
            ya = _lru(seq(xg), ev_conv_w[j], ev_conv_b[j],
                      _block_diag(ev_w_r[j]).astype(BF16), ev_b_r[j],
                      _block_diag(ev_w_i[j]).astype(BF16), ev_b_i[j], ev_lam[j])
            qh, kt, vh, qih, kit = _prep_even(seq(q), seq(k), seq(v), seq(qi), seq(kw), cos, sin,
                                              tile_gain(ev_q_norm[j]), tile_gain(ev_k_norm[j]), head_ones)
            yb = _dsa(qih, seq(kw), kit, qh, kt, vh)
            w_out = ev_w_out[j]
        else:
            wi = od_w_in[j]
            w_in = jnp.concatenate([wi[:, 0:3 * w], wi[:, 3 * w + N_HEADS:],
                                    _pad_cols(wi[:, 3 * w:3 * w + N_HEADS], LANES)], axis=1).astype(BF16)
            q, k, v, u, fl = _norm_matmul(x2, norm_mix[l], w_in, (w, w, w, 2 * w, LANES))
            qh, kt, vh = _prep_odd(seq(q), seq(k), seq(v), seq(fl),
                                   _pad_cols(od_b_f[j].reshape(1, N_HEADS), LANES),
                                   tile_gain(od_q_norm[j]), tile_gain(od_k_norm[j]), head_ones)
            ya = _fox(qh, kt, vh, tq=min(512, s))
            yb = _convmod(seq(u), od_conv_w[j], od_conv_b[j], od_ln_g[j], od_ln_b[j])
            w_out = od_w_out[j]
        x2 = _out_ffn(x2, ya.reshape(b * s, w), yb.reshape(b * s, w), w_out.astype(BF16),
                      norm_ffn[l], ffn_w_gu[l].astype(BF16), ffn_w_down[l].astype(BF16))
    return x2.reshape(b, s, d)
```

```python
import functools

import jax
import jax.numpy as jnp
from jax import lax
from jax.experimental import pallas as pl
from jax.experimental.pallas import tpu as pltpu

F32 = jnp.float32
BF16 = jnp.bfloat16

HEAD_DIM = 64
N_HEADS = 8
N_PAIRS = N_HEADS // 2
LANES = 128
CHUNK = 64
CHUNK_SHIFT = 6
TOPK_MAX = 256
ROPE_THETA = 10000.0
NORM_EPS = 1e-6
LRU_C = 8.0
LRU_CONV = 4
CONV_KERNEL = 31
MASKED = -1e30
LOG2E = 1.4426950408889634
QK_SCALE_LOG2 = HEAD_DIM ** -0.5 * LOG2E
VMEM_LIMIT = 56 * 1024 * 1024

assert CHUNK == 1 << CHUNK_SHIFT


def _params(*sem):
    return pltpu.CompilerParams(dimension_semantics=sem, vmem_limit_bytes=VMEM_LIMIT)


def _resident(shape, index_map):
    return pl.BlockSpec(shape, index_map, pipeline_mode=pl.Buffered(1))


def _norm_matmul_kernel(x_ref, g_ref, w_ref, *o_refs, widths):
    x = x_ref[...]
    ms = jnp.mean(x * x, axis=-1, keepdims=True)
    h = (x * lax.rsqrt(ms + NORM_EPS) * g_ref[...]).astype(BF16)
    off = 0
    for o_ref, wd in zip(o_refs, widths):
        o_ref[...] = jnp.dot(h, w_ref[:, off:off + wd], preferred_element_type=F32)
        off += wd


def _norm_matmul(x2, g, w, widths, tm=512):
    n, d = x2.shape
    assert n % tm == 0 and sum(widths) == w.shape[1]
    return pl.pallas_call(
        functools.partial(_norm_matmul_kernel, widths=widths),
        grid=(n // tm,),
        in_specs=[pl.BlockSpec((tm, d), lambda i: (i, 0)),
                  _resident((1, d), lambda i: (0, 0)),
                  _resident(w.shape, lambda i: (0, 0))],
        out_specs=[pl.BlockSpec((tm, wd), lambda i: (i, 0)) for wd in widths],
        out_shape=[jax.ShapeDtypeStruct((n, wd), F32) for wd in widths],
        compiler_params=_params("parallel"),
        name="norm_matmul",
    )(x2, g.reshape(1, d), w)


def _shift_rows(x, d, fill):
    row = lax.broadcasted_iota(jnp.int32, x.shape, 0)
    return jnp.where(row >= d, pltpu.roll(x, d, 0), fill)


def _linear_scan_rows(a, u):
    d = 1
    while d < a.shape[0]:
        u = a * _shift_rows(u, d, 0.0) + u
        a = a * _shift_rows(a, d, 1.0)
        d *= 2
    return a, u


def _cumsum_rows(x):
    d = 1
    while d < x.shape[0]:
        x = x + _shift_rows(x, d, 0.0)
        d *= 2
    return x


def _softplus(x):
    return jnp.maximum(x, 0.0) + jnp.log1p(jnp.exp(-jnp.abs(x)))


def _lru_kernel(xg_ref, cw_ref, cb_ref, wr_ref, br_ref, wi_ref, bi_ref, lam_ref, o_ref,
                xbuf_ref, h_ref, *, ts, width):
    @pl.when(pl.program_id(1) == 0)
    def _():
        xbuf_ref[0:8, :] = jnp.zeros((8, width), F32)
        h_ref[...] = jnp.zeros_like(h_ref)

    xa = xg_ref[0, :, 0:width]
    ga = xg_ref[0, :, width:2 * width]
    xbuf_ref[8:8 + ts, :] = xa
    xc = cb_ref[...] + cw_ref[0:1, :] * xbuf_ref[pl.ds(8 - (LRU_CONV - 1), ts), :]
    for j in range(1, LRU_CONV):
        xc = xc + cw_ref[j:j + 1, :] * xbuf_ref[pl.ds(8 - (LRU_CONV - 1) + j, ts), :]
    xbuf_ref[0:8, :] = xa[ts - 8:ts, :]

    xb = xc.astype(BF16)
    r = jax.nn.sigmoid(jnp.dot(xb, wr_ref[...], preferred_element_type=F32) + br_ref[...])
    gate = jax.nn.sigmoid(jnp.dot(xb, wi_ref[...], preferred_element_type=F32) + bi_ref[...])
    log_a = -LRU_C * r * _softplus(-lam_ref[...])
    a = jnp.exp(log_a)
    th = jnp.tanh(log_a)
    u = jnp.sqrt(-2.0 * th / (1.0 - th)) * (gate * xc)
    a_cum, h = _linear_scan_rows(a, u)
    h = h + a_cum * h_ref[...]
    h_ref[...] = h[ts - 1:ts, :]
    o_ref[0] = (h * jax.nn.gelu(ga, approximate=True)).astype(o_ref.dtype)


def _lru(xg, conv_w, conv_b, w_r, b_r, w_i, b_i, lam, ts=256):
    b, s, w2 = xg.shape
    width = w2 // 2
    assert s % ts == 0
    row = lambda v: v.reshape(1, width)
    const = lambda shape: _resident(shape, lambda bi, si: (0, 0))
    return pl.pallas_call(
        functools.partial(_lru_kernel, ts=ts, width=width),
        grid=(b, s // ts),
        in_specs=[pl.BlockSpec((1, ts, w2), lambda bi, si: (bi, si, 0)),
                  const((LRU_CONV, width)), const((1, width)),
                  const((width, width)), const((1, width)),
                  const((width, width)), const((1, width)), const((1, width))],
        out_specs=pl.BlockSpec((1, ts, width), lambda bi, si: (bi, si, 0)),
        out_shape=jax.ShapeDtypeStruct((b, s, width), BF16),
        scratch_shapes=[pltpu.VMEM((ts + 8, width), F32), pltpu.VMEM((1, width), F32)],
        compiler_params=_params("parallel", "arbitrary"),
        name="rg_lru",
    )(xg, conv_w, row(conv_b), w_r, row(b_r), w_i, row(b_i), row(lam))


def _head_mean_square(x, bd_ref):
    x2 = x * x
    hi = x2.astype(BF16)
    r1 = x2 - hi.astype(F32)
    mid = r1.astype(BF16)
    lo = (r1 - mid.astype(F32)).astype(BF16)
    bd = bd_ref[...]
    tot = (jnp.dot(hi, bd, preferred_element_type=F32) + jnp.dot(mid, bd, preferred_element_type=F32)
           + jnp.dot(lo, bd, preferred_element_type=F32))
    return tot * (1.0 / HEAD_DIM)


def _head_rms_norm(x, g, bd_ref):
    return x * lax.rsqrt(_head_mean_square(x, bd_ref) + NORM_EPS) * g


def _rope(x, cos, sin_signed):
    n = x.shape[1]
    lane = lax.broadcasted_iota(jnp.int32, x.shape, 1)
    first_half = (lane & (HEAD_DIM - 1)) < HEAD_DIM // 2
    partner = jnp.where(first_half, pltpu.roll(x, n - HEAD_DIM // 2, 1), pltpu.roll(x, HEAD_DIM // 2, 1))
    return x * cos + partner * sin_signed


def _prep_even_kernel(q_ref, k_ref, v_ref, qi_ref, kw_ref, cos_ref, sin_ref, gq_ref, gk_ref, bd_ref,
                      qo_ref, kto_ref, vo_ref, qio_ref, kito_ref):
    cos = cos_ref[...]
    sin = sin_ref[...]
    q = _rope(_head_rms_norm(q_ref[0], gq_ref[...], bd_ref), cos, sin)
    qo_ref[0] = (q * QK_SCALE_LOG2).astype(BF16)
    k = _rope(_head_rms_norm(k_ref[0], gk_ref[...], bd_ref), cos, sin)
    kto_ref[0] = k.T.astype(BF16)
    vo_ref[0] = v_ref[0].astype(BF16)
    qio_ref[0] = _rope(qi_ref[0], cos, sin).astype(BF16)
    ki = _rope(kw_ref[0], cos[:, 0:LANES], sin[:, 0:LANES])
    lane = lax.broadcasted_iota(jnp.int32, ki.shape, 1)
    ki2 = jnp.where(lane < HEAD_DIM, ki, pltpu.roll(ki, HEAD_DIM, 1))
    kito_ref[0] = ki2.T.astype(BF16)


def _prep_even(q, k, v, qi, kw, cos, sin, gq, gk, bd, ts=512):
    b, s, w = q.shape
    assert s % ts == 0
    tile = pl.BlockSpec((1, ts, w), lambda bi, si: (bi, si, 0))
    tab = pl.BlockSpec((ts, w), lambda bi, si: (si, 0))
    const = lambda shape: _resident(shape, lambda bi, si: (0, 0))
    return pl.pallas_call(
        _prep_even_kernel,
        grid=(b, s // ts),
        in_specs=[tile, tile, tile, tile, pl.BlockSpec((1, ts, LANES), lambda bi, si: (bi, si, 0)),
                  tab, tab, const((1, w)), const((1, w)), const((w, w))],
        out_specs=[tile, pl.BlockSpec((1, w, ts), lambda bi, si: (bi, 0, si)), tile, tile,
                   pl.BlockSpec((1, LANES, ts), lambda bi, si: (bi, 0, si))],
        out_shape=[jax.ShapeDtypeStruct((b, s, w), BF16), jax.ShapeDtypeStruct((b, w, s), BF16),
                   jax.ShapeDtypeStruct((b, s, w), BF16), jax.ShapeDtypeStruct((b, s, w), BF16),
                   jax.ShapeDtypeStruct((b, LANES, s), BF16)],
        compiler_params=_params("parallel", "parallel"),
        name="prep_even",
    )(q, k, v, qi, kw, cos, sin, gq, gk, bd)


def _prep_odd_kernel(q_ref, k_ref, v_ref, fl_ref, bf_ref, gq_ref, gk_ref, bd_ref,
                     qo_ref, kto_ref, vo_ref, carry_ref):
    @pl.when(pl.program_id(1) == 0)
    def _():
        carry_ref[...] = jnp.zeros_like(carry_ref)

    q = _head_rms_norm(q_ref[0], gq_ref[...], bd_ref) * QK_SCALE_LOG2
    k = _head_rms_norm(k_ref[0], gk_ref[...], bd_ref)
    v = v_ref[0]
    log_f = -_softplus(-(fl_ref[0] + bf_ref[...]))
    c = _cumsum_rows(log_f) + carry_ref[...]
    carry_ref[...] = c[c.shape[0] - 1:, :]
    c2 = c * LOG2E
    lane = lax.broadcasted_iota(jnp.int32, (c.shape[0], LANES), 1)
    for h in range(N_HEADS):
        p, e = divmod(h, 2)
        pick = lambda t: t[:, p * LANES:(p + 1) * LANES] if e == 0 else pltpu.roll(t[:, p * LANES:(p + 1) * LANES], HEAD_DIM, 1)
        cb = jnp.broadcast_to(c2[:, h:h + 1], (c.shape[0], LANES))
        hi = cb.astype(BF16).astype(F32)
        mid = (cb - hi).astype(BF16).astype(F32)
        lo = cb - hi - mid
        q_bias = jnp.where(lane == HEAD_DIM, hi, jnp.where(lane == HEAD_DIM + 1, mid, jnp.where(
            lane == HEAD_DIM + 2, lo, jnp.where(lane < HEAD_DIM + 6, 1.0, 0.0))))
        k_bias = jnp.where(lane < HEAD_DIM + 3, 1.0, jnp.where(lane == HEAD_DIM + 3, -hi, jnp.where(
            lane == HEAD_DIM + 4, -mid, jnp.where(lane == HEAD_DIM + 5, -lo, 0.0))))
        qo_ref[0, h] = jnp.where(lane < HEAD_DIM, pick(q), q_bias).astype(BF16)
        kto_ref[0, h] = jnp.where(lane < HEAD_DIM, pick(k), k_bias).T.astype(BF16)
        vo_ref[0, h] = jnp.where(lane < HEAD_DIM, pick(v), 1.0).astype(BF16)


def _prep_odd(q, k, v, fl, bf, gq, gk, bd, ts=512):
    b, s, w = q.shape
    assert s % ts == 0
    tile = pl.BlockSpec((1, ts, w), lambda bi, si: (bi, si, 0))
    narrow = pl.BlockSpec((1, ts, LANES), lambda bi, si: (bi, si, 0))
    heads = pl.BlockSpec((1, N_HEADS, ts, LANES), lambda bi, si: (bi, 0, si, 0))
    const = lambda shape: _resident(shape, lambda bi, si: (0, 0))
    return pl.pallas_call(
        _prep_odd_kernel,
        grid=(b, s // ts),
        in_specs=[tile, tile, tile, narrow, const((1, LANES)), const((1, w)), const((1, w)), const((w, w))],
        out_specs=[heads, pl.BlockSpec((1, N_HEADS, LANES, ts), lambda bi, si: (bi, 0, 0, si)), heads],
        out_shape=[jax.ShapeDtypeStruct((b, N_HEADS, s, LANES), BF16),
                   jax.ShapeDtypeStruct((b, N_HEADS, LANES, s), BF16),
                   jax.ShapeDtypeStruct((b, N_HEADS, s, LANES), BF16)],
        scratch_shapes=[pltpu.VMEM((1, LANES), F32)],
        compiler_params=_params("parallel", "arbitrary"),
        name="prep_odd",
    )(q, k, v, fl, bf, gq, gk, bd)


def _head_halves(x_pair):
    lane = lax.broadcasted_iota(jnp.int32, x_pair.shape, 1)
    zero = jnp.zeros_like(x_pair)
    return jnp.where(lane < HEAD_DIM, x_pair, zero), jnp.where(lane >= HEAD_DIM, x_pair, zero)


def _fold_lanes(x, op):
    out = x[:, 0:LANES]
    for c in range(1, x.shape[1] // LANES):
        out = op(out, x[:, c * LANES:(c + 1) * LANES])
    return out


def _online_softmax_block(scores, values, m_ref, l_ref, acc_ref):
    probs, alphas = [], []
    for h, s in enumerate(scores):
        m_old = m_ref[h]
        m_new = jnp.maximum(m_old, jnp.max(_fold_lanes(s, jnp.maximum), axis=1, keepdims=True))
        p = jnp.exp2(s - jnp.tile(m_new, (1, s.shape[1] // LANES)))
        alpha = jnp.exp2(m_old - m_new)
        if l_ref is not None:
            l_ref[h] = alpha * l_ref[h] + jnp.sum(_fold_lanes(p, jnp.add), axis=1, keepdims=True)
        m_ref[h] = m_new
        probs.append(p.astype(BF16))
        alphas.append(alpha)
    for h, (p, alpha) in enumerate(zip(probs, alphas)):
        acc_ref[h] = alpha * acc_ref[h] + jnp.dot(p, values[h], preferred_element_type=F32)


def _to_ordinal(x):
    bits = lax.bitcast_convert_type(x, jnp.int32)
    return jnp.where(bits < 0, bits ^ jnp.int32(0x7FFFFFFF), bits)


def _from_ordinal(o):
    return lax.bitcast_convert_type(jnp.where(o < 0, o ^ jnp.int32(0x7FFFFFFF), o), F32)


def _ordinal_midpoint(lo, hi):
    a, b = _to_ordinal(lo), _to_ordinal(hi)
    return _from_ordinal((a >> 1) + (b >> 1) + (a & b & 1))


def _next_float_up(x):
    return _from_ordinal(_to_ordinal(x) + 1)


def _merge_pair(acc_ref, l_ref, i0, i1):
    lane = lax.broadcasted_iota(jnp.int32, acc_ref.shape[1:], 1)
    return jnp.where(lane < HEAD_DIM, acc_ref[i0] / l_ref[i0], acc_ref[i1] / l_ref[i1])


def _dsa_index_kernel(qi_ref, kw_ref, kit_ref, mask_ref, sc_ref, *, tq, tk, topk, max_iters):
    t0 = pl.program_id(1) * tq
    nb = (t0 + tq + tk - 1) // tk
    row_chunk = (t0 + lax.broadcasted_iota(jnp.int32, (tq, 1), 0)) >> CHUNK_SHIFT
    kf = float(topk)

    def cols(j):
        return pl.ds(pl.multiple_of(j * tk, tk), tk)

    qi_heads = []
    for p in range(N_PAIRS):
        qi_heads.extend(_head_halves(qi_ref[0, :, p * LANES:(p + 1) * LANES]))
    kw = kw_ref[0]

    def score_block(j, carry):
        kit = kit_ref[0, :, cols(j)]
        score = jnp.zeros((tq, tk), F32)
        for h in range(N_HEADS):
            rel = jnp.maximum(jnp.dot(qi_heads[h], kit, preferred_element_type=F32), 0.0)
            score = score + kw[:, HEAD_DIM + h:HEAD_DIM + h + 1] * rel
        key_chunk = (j * tk + lax.broadcasted_iota(jnp.int32, (1, tk), 1)) >> CHUNK_SHIFT
        sc_ref[:, cols(j)] = jnp.where(key_chunk <= row_chunk, score, -jnp.inf)
        return carry

    lax.fori_loop(0, nb, score_block, 0)

    reps = tk // LANES

    def to_col(row):
        return jnp.broadcast_to(row, (LANES, tq)).T

    def to_row(wide, reduce):
        return reduce(wide.T, axis=0, keepdims=True)

    def count(row, strict=False):
        col = to_col(row)
        parts = []
        for r in range(0, tq, LANES):
            bound = jnp.tile(col[r:r + LANES], (1, reps))

            def body(j, cnt):
                blk = sc_ref[r:r + LANES, cols(j)]
                hit = blk > bound if strict else blk >= bound
                return cnt + _fold_lanes(jnp.where(hit, 1.0, 0.0), jnp.add)

            parts.append(lax.fori_loop(0, nb, body, jnp.zeros((LANES, LANES), F32)))
        return to_row(jnp.concatenate(parts, axis=0), jnp.sum)

    def range_body(j, carry):
        lo, hi = carry
        blk = sc_ref[:, cols(j)]
        lo = jnp.minimum(lo, _fold_lanes(jnp.where(blk == -jnp.inf, jnp.inf, blk), jnp.minimum))
        return lo, jnp.maximum(hi, _fold_lanes(blk, jnp.maximum))

    lo_w, hi_w = lax.fori_loop(0, nb, range_body,
                               (jnp.full((tq, LANES), jnp.inf, F32), jnp.full((tq, LANES), -jnp.inf, F32)))
    lo0 = to_row(lo_w, jnp.min)
    hi0 = _next_float_up(to_row(hi_w, jnp.max))
    n_visible = (((t0 + lax.broadcasted_iota(jnp.int32, (1, tq), 1)) >> CHUNK_SHIFT) + 1) * CHUNK
    short_row = n_visible <= topk
    log_k = jnp.log(kf + 0.5)

    def open_rows(lo, hi, c_lo):
        finished = jnp.logical_or(jnp.logical_or(c_lo == kf, short_row),
                                  _to_ordinal(_ordinal_midpoint(lo, hi)) == _to_ordinal(lo))
        return jnp.max(jnp.where(finished, 0.0, 1.0))

    def search_cond(state):
        return jnp.logical_and(state[0] < max_iters, state[-1] > 0.0)

    def search_body(state):
        it, lo, hi, c_lo, f_lo, f_hi, side, _ = state
        guess = lo + (hi - lo) * jnp.clip(f_lo / (f_lo - f_hi), 1.0 / 64, 63.0 / 64)
        inside = jnp.logical_and(guess > lo, guess < hi)
        half = _ordinal_midpoint(lo, hi)
        mid = jnp.where(it % 3 == 2, half, jnp.where(inside, guess, half))
        c_mid = count(mid)
        enough = c_mid >= kf
        f_mid = jnp.log(jnp.maximum(c_mid, 0.5)) - log_k
        f_hi = jnp.where(enough, jnp.where(side > 0.0, f_hi * 0.5, f_hi), f_mid)
        f_lo = jnp.where(enough, f_mid, jnp.where(side < 0.0, f_lo * 0.5, f_lo))
        lo, c_lo = jnp.where(enough, mid, lo), jnp.where(enough, c_mid, c_lo)
        hi = jnp.where(enough, hi, mid)
        return it + 1, lo, hi, c_lo, f_lo, f_hi, jnp.where(enough, 1.0, -1.0), open_rows(lo, hi, c_lo)

    c_lo0 = n_visible.astype(F32)
    state0 = (jnp.int32(0), lo0, hi0, c_lo0, jnp.log(c_lo0) - log_k, jnp.log(0.5) - log_k + jnp.zeros_like(lo0),
              jnp.zeros_like(lo0), open_rows(lo0, hi0, c_lo0))
    _, lo, _, n_ge, _, _, _, _ = lax.while_loop(search_cond, search_body, state0)
    thr_row = jnp.where(short_row, jnp.finfo(F32).min, lo)
    thr = jnp.tile(to_col(thr_row), (1, reps))

    has_extra = jnp.max(jnp.where(jnp.logical_and(n_ge > kf, jnp.logical_not(short_row)), 1.0, 0.0))

    @pl.when(has_extra > 0.0)
    def _():
        keep = to_col(kf - count(thr_row, strict=True))[:, 0:1]
        upper = (lax.broadcasted_iota(jnp.int32, (tk, tk), 0)
                 <= lax.broadcasted_iota(jnp.int32, (tk, tk), 1))
        prefix = jnp.where(upper, 1.0, 0.0).astype(BF16)

        def body(j, seen):
            blk = sc_ref[:, cols(j)]
            tie = jnp.where(blk == thr, 1.0, 0.0)
            rank = seen + jnp.dot(tie.astype(BF16), prefix, preferred_element_type=F32)
            drop = jnp.where(rank > keep, tie, 0.0)
            sc_ref[:, cols(j)] = jnp.where(drop > 0.0, -jnp.inf, blk)
            return seen + jnp.sum(tie, axis=1, keepdims=True)

        lax.fori_loop(0, nb, body, jnp.zeros((tq, 1), F32))

    def mask_block(j, carry):
        mask_ref[0, :, cols(j)] = jnp.where(sc_ref[:, cols(j)] >= thr, 1, 0).astype(mask_ref.dtype)
        return carry

    def clear_block(j, carry):
        mask_ref[0, :, cols(j)] = jnp.zeros((tq, tk), mask_ref.dtype)
        return carry

    lax.fori_loop(0, nb, mask_block, 0)
    lax.fori_loop(nb, mask_ref.shape[2] // tk, clear_block, 0)


def _dsa_index(qi, kw, kit, tq, tk):
    b, s, w = qi.shape
    assert s % tk == 0 and s % tq == 0 and tq % CHUNK == 0
    topk = min(TOPK_MAX, s // 4)
    return pl.pallas_call(
        functools.partial(_dsa_index_kernel, tq=tq, tk=tk, topk=topk, max_iters=640),
        grid=(b, s // tq),
        in_specs=[pl.BlockSpec((1, tq, w), lambda bi, i: (bi, i, 0)),
                  pl.BlockSpec((1, tq, LANES), lambda bi, i: (bi, i, 0)),
                  _resident((1, LANES, s), lambda bi, i: (bi, 0, 0))],
        out_specs=pl.BlockSpec((1, tq, s), lambda bi, i: (bi, i, 0)),
        out_shape=jax.ShapeDtypeStruct((b, s, s), jnp.int8),
        scratch_shapes=[pltpu.VMEM((tq, s), F32)],
        compiler_params=_params("parallel", "arbitrary"),
        name="dsa_index",
    )(qi, kw, kit)


def _dsa_attend_kernel(q_ref, kt_ref, v_ref, mask_ref, o_ref, m_ref, l_ref, acc_ref, *, tq, tk):
    nb = (pl.program_id(1) * tq + tq + tk - 1) // tk
    m_ref[...] = jnp.full_like(m_ref, MASKED)
    l_ref[...] = jnp.zeros_like(l_ref)
    acc_ref[...] = jnp.zeros_like(acc_ref)
    q_heads = []
    for p in range(N_PAIRS):
        q_heads.extend(_head_halves(q_ref[0, :, p * LANES:(p + 1) * LANES]))

    def attend_block(j, carry):
        cols = pl.ds(pl.multiple_of(j * tk, tk), tk)
        bias = jnp.where(mask_ref[0, :, cols].astype(jnp.int32) > 0, 0.0, MASKED)
        scores, values = [], []
        for h in range(N_HEADS):
            p = h // 2
            kt = kt_ref[0, p * LANES:(p + 1) * LANES, cols]
            scores.append(jnp.dot(q_heads[h], kt, preferred_element_type=F32) + bias)
            values.append(v_ref[0, cols, p * LANES:(p + 1) * LANES])
        _online_softmax_block(scores, values, m_ref, l_ref, acc_ref)
        return carry

    lax.fori_loop(0, nb, attend_block, 0)
    for p in range(N_PAIRS):
        o_ref[0, :, p * LANES:(p + 1) * LANES] = _merge_pair(acc_ref, l_ref, 2 * p, 2 * p + 1).astype(o_ref.dtype)


def _dsa_attend(q, kt, v, mask, tq, tk):
    b, s, w = q.shape
    assert s % tk == 0 and s % tq == 0
    qtile = pl.BlockSpec((1, tq, w), lambda bi, i: (bi, i, 0))
    return pl.pallas_call(
        functools.partial(_dsa_attend_kernel, tq=tq, tk=tk),
        grid=(b, s // tq),
        in_specs=[qtile,
                  _resident((1, w, s), lambda bi, i: (bi, 0, 0)),
                  _resident((1, s, w), lambda bi, i: (bi, 0, 0)),
                  pl.BlockSpec((1, tq, s), lambda bi, i: (bi, i, 0))],
        out_specs=qtile,
        out_shape=jax.ShapeDtypeStruct((b, s, w), BF16),
        scratch_shapes=[pltpu.VMEM((N_HEADS, tq, LANES), F32), pltpu.VMEM((N_HEADS, tq, LANES), F32),
                        pltpu.VMEM((N_HEADS, tq, LANES), F32)],
        compiler_params=_params("parallel", "arbitrary"),
        name="dsa_attend",
    )(q, kt, v, mask)


def _dsa(qi, kw, kit, q, kt, v, tq_index=256, tq_attend=256, tk=512):
    s = q.shape[1]
    tk = min(tk, s)
    mask = _dsa_index(qi, kw, kit, min(tq_index, s), tk)
    return _dsa_attend(q, kt, v, mask, min(tq_attend, s), tk)


def _fox_kernel(q_ref, kt_ref, v_ref, o_ref, m_ref, acc_ref, *, tq, heads):
    i = pl.program_id(2)
    m_ref[...] = jnp.full_like(m_ref, MASKED)
    acc_ref[...] = jnp.zeros_like(acc_ref)
    causal = (lax.broadcasted_iota(jnp.int32, (tq, tq), 1) <= lax.broadcasted_iota(jnp.int32, (tq, tq), 0))

    def block(j, diagonal):
        cols = pl.ds(pl.multiple_of(j * tq, tq), tq)
        scores = []
        for e in range(heads):
            s = jnp.dot(q_ref[0, e], kt_ref[0, e, :, cols], preferred_element_type=F32)
            scores.append(jnp.where(causal, s, MASKED) if diagonal else s)
        _online_softmax_block(scores, [v_ref[0, e, cols, :] for e in range(heads)], m_ref, None, acc_ref)

    def body(j, carry):
        block(j, False)
        return carry

    lax.fori_loop(0, i, body, 0)
    block(i, True)
    lane = lax.broadcasted_iota(jnp.int32, (tq, LANES), 1)
    for p in range(heads // 2):
        a0, a1 = acc_ref[2 * p], acc_ref[2 * p + 1]
        o_ref[0, :, p * LANES:(p + 1) * LANES] = jnp.where(
            lane < HEAD_DIM, a0 / pltpu.roll(a0, HEAD_DIM, 1), pltpu.roll(a1, HEAD_DIM, 1) / a1).astype(o_ref.dtype)


def _fox(q, kt, v, tq=512, heads=4):
    b, nh, s, _ = q.shape
    assert s % tq == 0 and nh == N_HEADS and nh % heads == 0 and heads % 2 == 0
    out_w = heads * HEAD_DIM
    return pl.pallas_call(
        functools.partial(_fox_kernel, tq=tq, heads=heads),
        grid=(b, nh // heads, s // tq),
        in_specs=[pl.BlockSpec((1, heads, tq, LANES), lambda bi, g, i: (bi, g, i, 0)),
                  _resident((1, heads, LANES, s), lambda bi, g, i: (bi, g, 0, 0)),
                  _resident((1, heads, s, LANES), lambda bi, g, i: (bi, g, 0, 0))],
        out_specs=pl.BlockSpec((1, tq, out_w), lambda bi, g, i: (bi, i, g)),
        out_shape=jax.ShapeDtypeStruct((b, s, N_HEADS * HEAD_DIM), BF16),
        scratch_shapes=[pltpu.VMEM((heads, tq, LANES), F32), pltpu.VMEM((heads, tq, LANES), F32)],
        compiler_params=_params("parallel", "parallel", "arbitrary"),
        name="fox",
    )(q, kt, v)


def _convmod_kernel(u_ref, cw_ref, cb_ref, g_ref, b_ref, o_ref, xbuf_ref, *, ts, width, halo):
    @pl.when(pl.program_id(1) == 0)
    def _():
        xbuf_ref[0:halo, :] = jnp.zeros((halo, width), F32)

    x = u_ref[0, :, 0:width] * jax.nn.sigmoid(u_ref[0, :, width:2 * width])
    xbuf_ref[halo:halo + ts, :] = x
    base = halo - (CONV_KERNEL - 1)
    y = cb_ref[...] + cw_ref[0:1, :] * xbuf_ref[pl.ds(base, ts), :]
    for j in range(1, CONV_KERNEL):
        y = y + cw_ref[j:j + 1, :] * xbuf_ref[pl.ds(base + j, ts), :]
    xbuf_ref[0:halo, :] = x[ts - halo:ts, :]
    mu = jnp.mean(y, axis=-1, keepdims=True)
    var = jnp.mean(jnp.square(y - mu), axis=-1, keepdims=True)
    z = (y - mu) * lax.rsqrt(var + NORM_EPS) * g_ref[...] + b_ref[...]
    o_ref[0] = (z * jax.nn.sigmoid(z)).astype(o_ref.dtype)


def _convmod(u, conv_w, conv_b, ln_g, ln_b, ts=512, halo=32):
    b, s, w2 = u.shape
    width = w2 // 2
    assert s % ts == 0 and halo >= CONV_KERNEL - 1
    row = lambda v: v.reshape(1, width)
    const = lambda shape: _resident(shape, lambda bi, si: (0, 0))
    return pl.pallas_call(
        functools.partial(_convmod_kernel, ts=ts, width=width, halo=halo),
        grid=(b, s // ts),
        in_specs=[pl.BlockSpec((1, ts, w2), lambda bi, si: (bi, si, 0)),
                  const((CONV_KERNEL, width)), const((1, width)), const((1, width)), const((1, width))],
        out_specs=pl.BlockSpec((1, ts, width), lambda bi, si: (bi, si, 0)),
        out_shape=jax.ShapeDtypeStruct((b, s, width), BF16),
        scratch_shapes=[pltpu.VMEM((ts + halo, width), F32)],
        compiler_params=_params("parallel", "arbitrary"),
        name="conv_module",
    )(u, conv_w, row(conv_b), row(ln_g), row(ln_b))


def _out_ffn_kernel(x_ref, ya_ref, yb_ref, wo_ref, g_ref, wgu_ref, wd_ref, o_ref, *, hidden, th):
    half = ya_ref.shape[1]
    x = x_ref[...] + (jnp.dot(ya_ref[...], wo_ref[0:half, :], preferred_element_type=F32)
                      + jnp.dot(yb_ref[...], wo_ref[half:2 * half, :], preferred_element_type=F32))
    ms = jnp.mean(x * x, axis=-1, keepdims=True)
    h = (x * lax.rsqrt(ms + NORM_EPS) * g_ref[...]).astype(BF16)
    y = x
    for c in range(0, hidden, th):
        gate = jnp.dot(h, wgu_ref[:, c:c + th], preferred_element_type=F32)
        up = jnp.dot(h, wgu_ref[:, hidden + c:hidden + c + th], preferred_element_type=F32)
        act = (gate * jax.nn.sigmoid(gate) * up).astype(BF16)
        y = y + jnp.dot(act, wd_ref[c:c + th, :], preferred_element_type=F32)
    o_ref[...] = y


def _out_ffn(x2, ya, yb, w_out, g, w_gu, w_down, tm=512, th=256):
    n, d = x2.shape
    half = ya.shape[1]
    hidden = w_down.shape[0]
    assert n % tm == 0 and hidden % th == 0
    const = lambda shape: _resident(shape, lambda i: (0, 0))
    return pl.pallas_call(
        functools.partial(_out_ffn_kernel, hidden=hidden, th=th),
        grid=(n // tm,),
        in_specs=[pl.BlockSpec((tm, d), lambda i: (i, 0)),
                  pl.BlockSpec((tm, half), lambda i: (i, 0)), pl.BlockSpec((tm, half), lambda i: (i, 0)),
                  const(w_out.shape), const((1, d)), const(w_gu.shape), const(w_down.shape)],
        out_specs=pl.BlockSpec((tm, d), lambda i: (i, 0)),
        out_shape=jax.ShapeDtypeStruct((n, d), F32),
        compiler_params=_params("parallel"),
        name="out_ffn",
    )(x2, ya, yb, w_out, g.reshape(1, d), w_gu, w_down)


def _rope_tables(s, width):
    inv = ROPE_THETA ** (-jnp.arange(0, HEAD_DIM, 2, dtype=F32) / HEAD_DIM)
    ang = jnp.arange(s, dtype=F32)[:, None] * inv[None, :]
    cos, sin = jnp.cos(ang), jnp.sin(ang)
    reps = width // HEAD_DIM
    return (jnp.tile(jnp.concatenate([cos, cos], axis=-1), (1, reps)),
            jnp.tile(jnp.concatenate([-sin, sin], axis=-1), (1, reps)))


def _block_diag(blocks):
    n, d, _ = blocks.shape
    eye = jnp.eye(n, dtype=blocks.dtype)
    return jnp.einsum('nde,nm->ndme', blocks, eye).reshape(n * d, n * d)


def _pad_cols(w, total):
    return jnp.pad(w, ((0, 0), (0, total - w.shape[1])))


def kernel(x, norm_mix, norm_ffn,
           ev_w_in, ev_conv_w, ev_conv_b, ev_w_r, ev_b_r, ev_w_i, ev_b_i, ev_lam,
           ev_q_norm, ev_k_norm, ev_w_out,
           od_w_in, od_b_f, od_q_norm, od_k_norm, od_conv_w, od_conv_b, od_ln_g, od_ln_b,
           od_w_out, ffn_w_gu, ffn_w_down):
    b, s, d = x.shape
    depth = norm_mix.shape[0]
    w = N_HEADS * HEAD_DIM
    cos, sin = _rope_tables(s, w)
    head_ones = _block_diag(jnp.ones((N_HEADS, HEAD_DIM, HEAD_DIM), BF16))
    tile_gain = lambda g: jnp.tile(g, N_HEADS).reshape(1, w)
    x2 = x.reshape(b * s, d)
    seq = lambda t: t.reshape(b, s, t.shape[-1])
    for l in range(depth):
        j = l // 2
        if l % 2 == 0:
            w_in = _pad_cols(ev_w_in[j], 2 * w + 4 * w + LANES).astype(BF16)
            xg, q, k, v, qi, kw = _norm_matmul(x2, norm_mix[l], w_in, (2 * w, w, w, w, w, LANES))
            ya = _lru(seq(xg), ev_conv_w[j], ev_conv_b[j],
                      _block_diag(ev_w_r[j]).astype(BF16), ev_b_r[j],
                      _block_diag(ev_w_i[j]).astype(BF16), ev_b_i[j], ev_lam[j])
            qh, kt, vh, qih, kit = _prep_even(seq(q), seq(k), seq(v), seq(qi), seq(kw), cos, sin,
                                              tile_gain(ev_q_norm[j]), tile_gain(ev_k_norm[j]), head_ones)
            yb = _dsa(qih, seq(kw), kit, qh, kt, vh)
            w_out = ev_w_out[j]
        else:
            wi = od_w_in[j]
            w_in = jnp.concatenate([wi[:, 0:3 * w], wi[:, 3 * w + N_HEADS:],
                                    _pad_cols(wi[:, 3 * w:3 * w + N_HEADS], LANES)], axis=1).astype(BF16)
            q, k, v, u, fl = _norm_matmul(x2, norm_mix[l], w_in, (w, w, w, 2 * w, LANES))
            qh, kt, vh = _prep_odd(seq(q), seq(k), seq(v), seq(fl),
                                   _pad_cols(od_b_f[j].reshape(1, N_HEADS), LANES),
                                   tile_gain(od_q_norm[j]), tile_gain(od_k_norm[j]), head_ones)
            ya = _fox(qh, kt, vh, tq=min(512, s))
            yb = _convmod(seq(u), od_conv_w[j], od_conv_b[j], od_ln_g[j], od_ln_b[j])
            w_out = od_w_out[j]
        x2 = _out_ffn(x2, ya.reshape(b * s, w), yb.reshape(b * s, w), w_out.astype(BF16),
                      norm_ffn[l], ffn_w_gu[l].astype(BF16), ffn_w_down[l].astype(BF16))
    return x2.reshape(b, s, d)
```

```python
import functools

import jax
import jax.numpy as jnp
from jax import lax
from jax.experimental import pallas as pl
from jax.experimental.pallas import tpu as pltpu

F32 = jnp.float32
BF16 = jnp.bfloat16

HEAD_DIM = 64
N_HEADS = 8
N_PAIRS = N_HEADS // 2
LANES = 128
CHUNK = 64
CHUNK_SHIFT = 6
TOPK_MAX = 256
ROPE_THETA = 10000.0
NORM_EPS = 1e-6
LRU_C = 8.0
LRU_CONV = 4
CONV_KERNEL = 31
MASKED = -1e30
LOG2E = 1.4426950408889634
QK_SCALE_LOG2 = HEAD_DIM ** -0.5 * LOG2E
VMEM_LIMIT = 56 * 1024 * 1024

assert CHUNK == 1 << CHUNK_SHIFT


def _params(*sem):
    return pltpu.CompilerParams(dimension_semantics=sem, vmem_limit_bytes=VMEM_LIMIT)


def _resident(shape, index_map):
    return pl.BlockSpec(shape, index_map, pipeline_mode=pl.Buffered(1))


def _norm_matmul_kernel(x_ref, g_ref, w_ref, *o_refs, widths):
    x = x_ref[...]
    ms = jnp.mean(x * x, axis=-1, keepdims=True)
    h = (x * lax.rsqrt(ms + NORM_EPS) * g_ref[...]).astype(BF16)
    off = 0
    for o_ref, wd in zip(o_refs, widths):
        o_ref[...] = jnp.dot(h, w_ref[:, off:off + wd], preferred_element_type=F32)
        off += wd


def _norm_matmul(x2, g, w, widths, tm=512):
    n, d = x2.shape
    assert n % tm == 0 and sum(widths) == w.shape[1]
    return pl.pallas_call(
        functools.partial(_norm_matmul_kernel, widths=widths),
        grid=(n // tm,),
        in_specs=[pl.BlockSpec((tm, d), lambda i: (i, 0)),
                  _resident((1, d), lambda i: (0, 0)),
                  _resident(w.shape, lambda i: (0, 0))],
        out_specs=[pl.BlockSpec((tm, wd), lambda i: (i, 0)) for wd in widths],
        out_shape=[jax.ShapeDtypeStruct((n, wd), F32) for wd in widths],
        compiler_params=_params("parallel"),
        name="norm_matmul",
    )(x2, g.reshape(1, d), w)


def _shift_rows(x, d, fill):
    row = lax.broadcasted_iota(jnp.int32, x.shape, 0)
    return jnp.where(row >= d, pltpu.roll(x, d, 0), fill)


def _linear_scan_rows(a, u):
    d = 1
    while d < a.shape[0]:
        u = a * _shift_rows(u, d, 0.0) + u
        a = a * _shift_rows(a, d, 1.0)
        d *= 2
    return a, u


def _cumsum_rows(x):
    d = 1
    while d < x.shape[0]:
        x = x + _shift_rows(x, d, 0.0)
        d *= 2
    return x


def _softplus(x):
    return jnp.maximum(x, 0.0) + jnp.log1p(jnp.exp(-jnp.abs(x)))


def _lru_kernel(xg_ref, cw_ref, cb_ref, wr_ref, br_ref, wi_ref, bi_ref, lam_ref, o_ref,
                xbuf_ref, h_ref, *, ts, width):
    @pl.when(pl.program_id(1) == 0)
    def _():
        xbuf_ref[0:8, :] = jnp.zeros((8, width), F32)
        h_ref[...] = jnp.zeros_like(h_ref)

    xa = xg_ref[0, :, 0:width]
    ga = xg_ref[0, :, width:2 * width]
    xbuf_ref[8:8 + ts, :] = xa
    xc = cb_ref[...] + cw_ref[0:1, :] * xbuf_ref[pl.ds(8 - (LRU_CONV - 1), ts), :]
    for j in range(1, LRU_CONV):
        xc = xc + cw_ref[j:j + 1, :] * xbuf_ref[pl.ds(8 - (LRU_CONV - 1) + j, ts), :]
    xbuf_ref[0:8, :] = xa[ts - 8:ts, :]

    xb = xc.astype(BF16)
    r = jax.nn.sigmoid(jnp.dot(xb, wr_ref[...], preferred_element_type=F32) + br_ref[...])
    gate = jax.nn.sigmoid(jnp.dot(xb, wi_ref[...], preferred_element_type=F32) + bi_ref[...])
    log_a = -LRU_C * r * _softplus(-lam_ref[...])
    a = jnp.exp(log_a)
    th = jnp.tanh(log_a)
    u = jnp.sqrt(-2.0 * th / (1.0 - th)) * (gate * xc)
    a_cum, h = _linear_scan_rows(a, u)
    h = h + a_cum * h_ref[...]
    h_ref[...] = h[ts - 1:ts, :]
    o_ref[0] = (h * jax.nn.gelu(ga, approximate=True)).astype(o_ref.dtype)


def _lru(xg, conv_w, conv_b, w_r, b_r, w_i, b_i, lam, ts=256):
    b, s, w2 = xg.shape
    width = w2 // 2
    assert s % ts == 0
    row = lambda v: v.reshape(1, width)
    const = lambda shape: _resident(shape, lambda bi, si: (0, 0))
    return pl.pallas_call(
        functools.partial(_lru_kernel, ts=ts, width=width),
        grid=(b, s // ts),
        in_specs=[pl.BlockSpec((1, ts, w2), lambda bi, si: (bi, si, 0)),
                  const((LRU_CONV, width)), const((1, width)),
                  const((width, width)), const((1, width)),
                  const((width, width)), const((1, width)), const((1, width))],
        out_specs=pl.BlockSpec((1, ts, width), lambda bi, si: (bi, si, 0)),
        out_shape=jax.ShapeDtypeStruct((b, s, width), BF16),
        scratch_shapes=[pltpu.VMEM((ts + 8, width), F32), pltpu.VMEM((1, width), F32)],
        compiler_params=_params("parallel", "arbitrary"),
        name="rg_lru",
    )(xg, conv_w, row(conv_b), w_r, row(b_r), w_i, row(b_i), row(lam))


def _head_mean_square(x, bd_ref):
    x2 = x * x
    hi = x2.astype(BF16)
    r1 = x2 - hi.astype(F32)
    mid = r1.astype(BF16)
    lo = (r1 - mid.astype(F32)).astype(BF16)
    bd = bd_ref[...]
    tot = (jnp.dot(hi, bd, preferred_element_type=F32) + jnp.dot(mid, bd, preferred_element_type=F32)
           + jnp.dot(lo, bd, preferred_element_type=F32))
    return tot * (1.0 / HEAD_DIM)


def _head_rms_norm(x, g, bd_ref):
    return x * lax.rsqrt(_head_mean_square(x, bd_ref) + NORM_EPS) * g


def _rope(x, cos, sin_signed):
    n = x.shape[1]
    lane = lax.broadcasted_iota(jnp.int32, x.shape, 1)
    first_half = (lane & (HEAD_DIM - 1)) < HEAD_DIM // 2
    partner = jnp.where(first_half, pltpu.roll(x, n - HEAD_DIM // 2, 1), pltpu.roll(x, HEAD_DIM // 2, 1))
    return x * cos + partner * sin_signed


def _prep_even_kernel(q_ref, k_ref, v_ref, qi_ref, kw_ref, cos_ref, sin_ref, gq_ref, gk_ref, bd_ref,
                      qo_ref, kto_ref, vo_ref, qio_ref, kito_ref):
    cos = cos_ref[...]
    sin = sin_ref[...]
    q = _rope(_head_rms_norm(q_ref[0], gq_ref[...], bd_ref), cos, sin)
    qo_ref[0] = (q * QK_SCALE_LOG2).astype(BF16)
    k = _rope(_head_rms_norm(k_ref[0], gk_ref[...], bd_ref), cos, sin)
    kto_ref[0] = k.T.astype(BF16)
    vo_ref[0] = v_ref[0].astype(BF16)
    qio_ref[0] = _rope(qi_ref[0], cos, sin).astype(BF16)
    ki = _rope(kw_ref[0], cos[:, 0:LANES], sin[:, 0:LANES])
    lane = lax.broadcasted_iota(jnp.int32, ki.shape, 1)
    ki2 = jnp.where(lane < HEAD_DIM, ki, pltpu.roll(ki, HEAD_DIM, 1))
    kito_ref[0] = ki2.T.astype(BF16)


def _prep_even(q, k, v, qi, kw, cos, sin, gq, gk, bd, ts=512):
    b, s, w = q.shape
    assert s % ts == 0
    tile = pl.BlockSpec((1, ts, w), lambda bi, si: (bi, si, 0))
    tab = pl.BlockSpec((ts, w), lambda bi, si: (si, 0))
    const = lambda shape: _resident(shape, lambda bi, si: (0, 0))
    return pl.pallas_call(
        _prep_even_kernel,
        grid=(b, s // ts),
        in_specs=[tile, tile, tile, tile, pl.BlockSpec((1, ts, LANES), lambda bi, si: (bi, si, 0)),
                  tab, tab, const((1, w)), const((1, w)), const((w, w))],
        out_specs=[tile, pl.BlockSpec((1, w, ts), lambda bi, si: (bi, 0, si)), tile, tile,
                   pl.BlockSpec((1, LANES, ts), lambda bi, si: (bi, 0, si))],
        out_shape=[jax.ShapeDtypeStruct((b, s, w), BF16), jax.ShapeDtypeStruct((b, w, s), BF16),
                   jax.ShapeDtypeStruct((b, s, w), BF16), jax.ShapeDtypeStruct((b, s, w), BF16),
                   jax.ShapeDtypeStruct((b, LANES, s), BF16)],
        compiler_params=_params("parallel", "parallel"),
        name="prep_even",
    )(q, k, v, qi, kw, cos, sin, gq, gk, bd)


def _prep_odd_kernel(q_ref, k_ref, v_ref, fl_ref, bf_ref, gq_ref, gk_ref, bd_ref,
                     qo_ref, kto_ref, vo_ref, carry_ref):
    @pl.when(pl.program_id(1) == 0)
    def _():
        carry_ref[...] = jnp.zeros_like(carry_ref)

    q = _head_rms_norm(q_ref[0], gq_ref[...], bd_ref) * QK_SCALE_LOG2
    k = _head_rms_norm(k_ref[0], gk_ref[...], bd_ref)
    v = v_ref[0]
    log_f = -_softplus(-(fl_ref[0] + bf_ref[...]))
    c = _cumsum_rows(log_f) + carry_ref[...]
    carry_ref[...] = c[c.shape[0] - 1:, :]
    c2 = c * LOG2E
    lane = lax.broadcasted_iota(jnp.int32, (c.shape[0], LANES), 1)
    for h in range(N_HEADS):
        p, e = divmod(h, 2)
        pick = lambda t: t[:, p * LANES:(p + 1) * LANES] if e == 0 else pltpu.roll(t[:, p * LANES:(p + 1) * LANES], HEAD_DIM, 1)
        cb = jnp.broadcast_to(c2[:, h:h + 1], (c.shape[0], LANES))
        hi = cb.astype(BF16).astype(F32)
        mid = (cb - hi).astype(BF16).astype(F32)
        lo = cb - hi - mid
        q_bias = jnp.where(lane == HEAD_DIM, hi, jnp.where(lane == HEAD_DIM + 1, mid, jnp.where(
            lane == HEAD_DIM + 2, lo, jnp.where(lane < HEAD_DIM + 6, 1.0, 0.0))))
        k_bias = jnp.where(lane < HEAD_DIM + 3, 1.0, jnp.where(lane == HEAD_DIM + 3, -hi, jnp.where(
            lane == HEAD_DIM + 4, -mid, jnp.where(lane == HEAD_DIM + 5, -lo, 0.0))))
        qo_ref[0, h] = jnp.where(lane < HEAD_DIM, pick(q), q_bias).astype(BF16)
        kto_ref[0, h] = jnp.where(lane < HEAD_DIM, pick(k), k_bias).T.astype(BF16)
        vo_ref[0, h] = jnp.where(lane < HEAD_DIM, pick(v), 1.0).astype(BF16)


def _prep_odd(q, k, v, fl, bf, gq, gk, bd, ts=512):
    b, s, w = q.shape
    assert s % ts == 0
    tile = pl.BlockSpec((1, ts, w), lambda bi, si: (bi, si, 0))
    narrow = pl.BlockSpec((1, ts, LANES), lambda bi, si: (bi, si, 0))
    heads = pl.BlockSpec((1, N_HEADS, ts, LANES), lambda bi, si: (bi, 0, si, 0))
    const = lambda shape: _resident(shape, lambda bi, si: (0, 0))
    return pl.pallas_call(
        _prep_odd_kernel,
        grid=(b, s // ts),
        in_specs=[tile, tile, tile, narrow, const((1, LANES)), const((1, w)), const((1, w)), const((w, w))],
        out_specs=[heads, pl.BlockSpec((1, N_HEADS, LANES, ts), lambda bi, si: (bi, 0, 0, si)), heads],
        out_shape=[jax.ShapeDtypeStruct((b, N_HEADS, s, LANES), BF16),
                   jax.ShapeDtypeStruct((b, N_HEADS, LANES, s), BF16),
                   jax.ShapeDtypeStruct((b, N_HEADS, s, LANES), BF16)],
        scratch_shapes=[pltpu.VMEM((1, LANES), F32)],
        compiler_params=_params("parallel", "arbitrary"),
        name="prep_odd",
    )(q, k, v, fl, bf, gq, gk, bd)


def _head_halves(x_pair):
    lane = lax.broadcasted_iota(jnp.int32, x_pair.shape, 1)
    zero = jnp.zeros_like(x_pair)
    return jnp.where(lane < HEAD_DIM, x_pair, zero), jnp.where(lane >= HEAD_DIM, x_pair, zero)


def _fold_lanes(x, op):
    out = x[:, 0:LANES]
    for c in range(1, x.shape[1] // LANES):
        out = op(out, x[:, c * LANES:(c + 1) * LANES])
    return out


def _online_softmax_block(scores, values, m_ref, l_ref, acc_ref):
    probs, alphas = [], []
    for h, s in enumerate(scores):
        m_old = m_ref[h]
        m_new = jnp.maximum(m_old, jnp.max(_fold_lanes(s, jnp.maximum), axis=1, keepdims=True))
        p = jnp.exp2(s - jnp.tile(m_new, (1, s.shape[1] // LANES)))
        alpha = jnp.exp2(m_old - m_new)
        if l_ref is not None:
            l_ref[h] = alpha * l_ref[h] + jnp.sum(_fold_lanes(p, jnp.add), axis=1, keepdims=True)
        m_ref[h] = m_new
        probs.append(p.astype(BF16))
        alphas.append(alpha)
    for h, (p, alpha) in enumerate(zip(probs, alphas)):
        acc_ref[h] = alpha * acc_ref[h] + jnp.dot(p, values[h], preferred_element_type=F32)


def _to_ordinal(x):
    bits = lax.bitcast_convert_type(x, jnp.int32)
    return jnp.where(bits < 0, bits ^ jnp.int32(0x7FFFFFFF), bits)


def _from_ordinal(o):
    return lax.bitcast_convert_type(jnp.where(o < 0, o ^ jnp.int32(0x7FFFFFFF), o), F32)


def _ordinal_midpoint(lo, hi):
    a, b = _to_ordinal(lo), _to_ordinal(hi)
    return _from_ordinal((a >> 1) + (b >> 1) + (a & b & 1))


def _next_float_up(x):
    return _from_ordinal(_to_ordinal(x) + 1)


def _floor_to_bf16(x):
    bits = lax.bitcast_convert_type(x, jnp.int32)
    return lax.bitcast_convert_type((bits + ((bits >> 31) & jnp.int32(0xFFFF))) & jnp.int32(-65536), F32)


def _bf16_index(v):
    return _to_ordinal(v) >> 16


def _from_bf16_index(g):
    return _from_ordinal(jnp.where(g < 0, (g << 16) | jnp.int32(0xFFFF), g << 16))


def _bf16_midpoint(lo, hi):
    a, b = _bf16_index(lo), _bf16_index(hi)
    return _from_bf16_index((a >> 1) + (b >> 1) + (a & b & 1))


def _merge_pair(acc_ref, l_ref, i0, i1):
    lane = lax.broadcasted_iota(jnp.int32, acc_ref.shape[1:], 1)
    return jnp.where(lane < HEAD_DIM, acc_ref[i0] / l_ref[i0], acc_ref[i1] / l_ref[i1])


def _dsa_index_kernel(qi_ref, kw_ref, kit_ref, mask_ref, sc_ref, sb_ref, *, tq, tk, topk, max_iters):
    t0 = pl.program_id(1) * tq
    nb = (t0 + tq + tk - 1) // tk
    row_chunk = (t0 + lax.broadcasted_iota(jnp.int32, (tq, 1), 0)) >> CHUNK_SHIFT
    kf = float(topk)

    def cols(j):
        return pl.ds(pl.multiple_of(j * tk, tk), tk)

    qi_heads = []
    for p in range(N_PAIRS):
        qi_heads.extend(_head_halves(qi_ref[0, :, p * LANES:(p + 1) * LANES]))
    kw = kw_ref[0]

    def score_block(j, carry):
        kit = kit_ref[0, :, cols(j)]
        score = jnp.zeros((tq, tk), F32)
        for h in range(N_HEADS):
            rel = jnp.maximum(jnp.dot(qi_heads[h], kit, preferred_element_type=F32), 0.0)
            score = score + kw[:, HEAD_DIM + h:HEAD_DIM + h + 1] * rel
        key_chunk = (j * tk + lax.broadcasted_iota(jnp.int32, (1, tk), 1)) >> CHUNK_SHIFT
        score = jnp.where(key_chunk <= row_chunk, score, -jnp.inf)
        sc_ref[:, cols(j)] = score
        sb_ref[:, cols(j)] = _floor_to_bf16(score).astype(BF16)
        return carry

    lax.fori_loop(0, nb, score_block, 0)

    reps = tk // LANES

    def to_col(row):
        return jnp.broadcast_to(row, (LANES, tq)).T

    def to_row(wide, reduce):
        return reduce(wide.T, axis=0, keepdims=True)

    def count(row, strict=False, coarse=False):
        col = to_col(row)
        src, dt = (sb_ref, BF16) if coarse else (sc_ref, F32)
        acc_dt = dt if (sc_ref.shape[1] // LANES) <= 256 else F32
        one, zero = jnp.ones((), dt), jnp.zeros((), dt)
        parts = []
        for r in range(0, tq, LANES):
            bound = jnp.tile(col[r:r + LANES].astype(dt), (1, reps))

            def body(j, cnt):
                blk = src[r:r + LANES, cols(j)]
                hit = blk > bound if strict else blk >= bound
                return cnt + _fold_lanes(jnp.where(hit, one, zero), jnp.add).astype(acc_dt)

            parts.append(lax.fori_loop(0, nb, body, jnp.zeros((LANES, LANES), acc_dt)).astype(F32))
        return to_row(jnp.concatenate(parts, axis=0), jnp.sum)

    def range_body(j, carry):
        lo, hi = carry
        blk = sc_ref[:, cols(j)]
        lo = jnp.minimum(lo, _fold_lanes(jnp.where(blk == -jnp.inf, jnp.inf, blk), jnp.minimum))
        return lo, jnp.maximum(hi, _fold_lanes(blk, jnp.maximum))

    lo_w, hi_w = lax.fori_loop(0, nb, range_body,
                               (jnp.full((tq, LANES), jnp.inf, F32), jnp.full((tq, LANES), -jnp.inf, F32)))
    lo0 = to_row(lo_w, jnp.min)
    hi0 = _next_float_up(to_row(hi_w, jnp.max))
    n_visible = (((t0 + lax.broadcasted_iota(jnp.int32, (1, tq), 1)) >> CHUNK_SHIFT) + 1) * CHUNK
    short_row = n_visible <= topk
    log_k = jnp.log(kf + 0.5)

    def search(lo, hi, c_lo, f_lo, f_hi, side, coarse):
        midpoint = _bf16_midpoint if coarse else _ordinal_midpoint

        def open_rows(lo, hi, c_lo):
            finished = jnp.logical_or(jnp.logical_or(c_lo == kf, short_row),
                                      _to_ordinal(midpoint(lo, hi)) == _to_ordinal(lo))
            return jnp.max(jnp.where(finished, 0.0, 1.0))

        def cond(state):
            return jnp.logical_and(state[0] < max_iters, state[-1] > 0.0)

        def body(state):
            it, lo, hi, c_lo, f_lo, f_hi, side, _ = state
            guess = lo + (hi - lo) * jnp.clip(f_lo / (f_lo - f_hi), 1.0 / 64, 63.0 / 64)
            if coarse:
                guess = _floor_to_bf16(guess)
            inside = jnp.logical_and(guess > lo, guess < hi)
            half = midpoint(lo, hi)
            mid = jnp.where(it % 3 == 2, half, jnp.where(inside, guess, half))
            c_mid = count(mid, coarse=coarse)
            enough = c_mid >= kf
            f_mid = jnp.log(jnp.maximum(c_mid, 0.5)) - log_k
            f_hi = jnp.where(enough, jnp.where(side > 0.0, f_hi * 0.5, f_hi), f_mid)
            f_lo = jnp.where(enough, f_mid, jnp.where(side < 0.0, f_lo * 0.5, f_lo))
            lo, c_lo = jnp.where(enough, mid, lo), jnp.where(enough, c_mid, c_lo)
            hi = jnp.where(enough, hi, mid)
            return it + 1, lo, hi, c_lo, f_lo, f_hi, jnp.where(enough, 1.0, -1.0), open_rows(lo, hi, c_lo)

        state = (jnp.int32(0), lo, hi, c_lo, f_lo, f_hi, side, open_rows(lo, hi, c_lo))
        return lax.while_loop(cond, body, state)[1:7]

    c_lo0 = n_visible.astype(F32)
    hi0 = _from_bf16_index(_bf16_index(_floor_to_bf16(hi0)) + 1)
    state = search(_floor_to_bf16(lo0), hi0, c_lo0, jnp.log(c_lo0) - log_k,
                   jnp.log(0.5) - log_k + jnp.zeros_like(lo0), jnp.zeros_like(lo0), coarse=True)
    lo, _, n_ge, _, _, _ = search(*state, coarse=False)
    thr_row = jnp.where(short_row, jnp.finfo(F32).min, lo)
    thr = jnp.tile(to_col(thr_row), (1, reps))

    has_extra = jnp.max(jnp.where(jnp.logical_and(n_ge > kf, jnp.logical_not(short_row)), 1.0, 0.0))

    @pl.when(has_extra > 0.0)
    def _():
        keep = to_col(kf - count(thr_row, strict=True))[:, 0:1]
        upper = (lax.broadcasted_iota(jnp.int32, (tk, tk), 0)
                 <= lax.broadcasted_iota(jnp.int32, (tk, tk), 1))
        prefix = jnp.where(upper, 1.0, 0.0).astype(BF16)

        def body(j, seen):
            blk = sc_ref[:, cols(j)]
            tie = jnp.where(blk == thr, 1.0, 0.0)
            rank = seen + jnp.dot(tie.astype(BF16), prefix, preferred_element_type=F32)
            drop = jnp.where(rank > keep, tie, 0.0)
            sc_ref[:, cols(j)] = jnp.where(drop > 0.0, -jnp.inf, blk)
            return seen + jnp.sum(tie, axis=1, keepdims=True)

        lax.fori_loop(0, nb, body, jnp.zeros((tq, 1), F32))

    def mask_block(j, carry):
        mask_ref[0, :, cols(j)] = jnp.where(sc_ref[:, cols(j)] >= thr, 1, 0).astype(mask_ref.dtype)
        return carry

    def clear_block(j, carry):
        mask_ref[0, :, cols(j)] = jnp.zeros((tq, tk), mask_ref.dtype)
        return carry

    lax.fori_loop(0, nb, mask_block, 0)
    lax.fori_loop(nb, mask_ref.shape[2] // tk, clear_block, 0)


def _dsa_index(qi, kw, kit, tq, tk):
    b, s, w = qi.shape
    assert s % tk == 0 and s % tq == 0 and tq % CHUNK == 0
    topk = min(TOPK_MAX, s // 4)
    return pl.pallas_call(
        functools.partial(_dsa_index_kernel, tq=tq, tk=tk, topk=topk, max_iters=640),
        grid=(b, s // tq),
        in_specs=[pl.BlockSpec((1, tq, w), lambda bi, i: (bi, i, 0)),
                  pl.BlockSpec((1, tq, LANES), lambda bi, i: (bi, i, 0)),
                  _resident((1, LANES, s), lambda bi, i: (bi, 0, 0))],
        out_specs=pl.BlockSpec((1, tq, s), lambda bi, i: (bi, i, 0)),
        out_shape=jax.ShapeDtypeStruct((b, s, s), jnp.int8),
        scratch_shapes=[pltpu.VMEM((tq, s), F32), pltpu.VMEM((tq, s), BF16)],
        compiler_params=_params("parallel", "arbitrary"),
        name="dsa_index",
    )(qi, kw, kit)


def _dsa_attend_kernel(q_ref, kt_ref, v_ref, mask_ref, o_ref, m_ref, l_ref, acc_ref, *, tq, tk):
    nb = (pl.program_id(1) * tq + tq + tk - 1) // tk
    m_ref[...] = jnp.full_like(m_ref, MASKED)
    l_ref[...] = jnp.zeros_like(l_ref)
    acc_ref[...] = jnp.zeros_like(acc_ref)
    q_heads = []
    for p in range(N_PAIRS):
        q_heads.extend(_head_halves(q_ref[0, :, p * LANES:(p + 1) * LANES]))

    def attend_block(j, carry):
        cols = pl.ds(pl.multiple_of(j * tk, tk), tk)
        bias = jnp.where(mask_ref[0, :, cols].astype(jnp.int32) > 0, 0.0, MASKED)
        scores, values = [], []
        for h in range(N_HEADS):
            p = h // 2
            kt = kt_ref[0, p * LANES:(p + 1) * LANES, cols]
            scores.append(jnp.dot(q_heads[h], kt, preferred_element_type=F32) + bias)
            values.append(v_ref[0, cols, p * LANES:(p + 1) * LANES])
        _online_softmax_block(scores, values, m_ref, l_ref, acc_ref)
        return carry

    lax.fori_loop(0, nb, attend_block, 0)
    for p in range(N_PAIRS):
        o_ref[0, :, p * LANES:(p + 1) * LANES] = _merge_pair(acc_ref, l_ref, 2 * p, 2 * p + 1).astype(o_ref.dtype)


def _dsa_attend(q, kt, v, mask, tq, tk):
    b, s, w = q.shape
    assert s % tk == 0 and s % tq == 0
    qtile = pl.BlockSpec((1, tq, w), lambda bi, i: (bi, i, 0))
    return pl.pallas_call(
        functools.partial(_dsa_attend_kernel, tq=tq, tk=tk),
        grid=(b, s // tq),
        in_specs=[qtile,
                  _resident((1, w, s), lambda bi, i: (bi, 0, 0)),
                  _resident((1, s, w), lambda bi, i: (bi, 0, 0)),
                  pl.BlockSpec((1, tq, s), lambda bi, i: (bi, i, 0))],
        out_specs=qtile,
        out_shape=jax.ShapeDtypeStruct((b, s, w), BF16),
        scratch_shapes=[pltpu.VMEM((N_HEADS, tq, LANES), F32), pltpu.VMEM((N_HEADS, tq, LANES), F32),
                        pltpu.VMEM((N_HEADS, tq, LANES), F32)],
        compiler_params=_params("parallel", "arbitrary"),
        name="dsa_attend",
    )(q, kt, v, mask)


def _dsa(qi, kw, kit, q, kt, v, tq_index=256, tq_attend=256, tk=512):
    s = q.shape[1]
    tk = min(tk, s)
    mask = _dsa_index(qi, kw, kit, min(tq_index, s), tk)
    return _dsa_attend(q, kt, v, mask, min(tq_attend, s), tk)


def _fox_kernel(q_ref, kt_ref, v_ref, o_ref, m_ref, acc_ref, *, tq, heads):
    i = pl.program_id(2)
    m_ref[...] = jnp.full_like(m_ref, MASKED)
    acc_ref[...] = jnp.zeros_like(acc_ref)
    causal = (lax.broadcasted_iota(jnp.int32, (tq, tq), 1) <= lax.broadcasted_iota(jnp.int32, (tq, tq), 0))

    def block(j, diagonal):
        cols = pl.ds(pl.multiple_of(j * tq, tq), tq)
        scores = []
        for e in range(heads):
            s = jnp.dot(q_ref[0, e], kt_ref[0, e, :, cols], preferred_element_type=F32)
            scores.append(jnp.where(causal, s, MASKED) if diagonal else s)
        _online_softmax_block(scores, [v_ref[0, e, cols, :] for e in range(heads)], m_ref, None, acc_ref)

    def body(j, carry):
        block(j, False)
        return carry

    lax.fori_loop(0, i, body, 0)
    block(i, True)
    lane = lax.broadcasted_iota(jnp.int32, (tq, LANES), 1)
    for p in range(heads // 2):
        a0, a1 = acc_ref[2 * p], acc_ref[2 * p + 1]
        o_ref[0, :, p * LANES:(p + 1) * LANES] = jnp.where(
            lane < HEAD_DIM, a0 / pltpu.roll(a0, HEAD_DIM, 1), pltpu.roll(a1, HEAD_DIM, 1) / a1).astype(o_ref.dtype)


def _fox(q, kt, v, tq=512, heads=4):
    b, nh, s, _ = q.shape
    assert s % tq == 0 and nh == N_HEADS and nh % heads == 0 and heads % 2 == 0
    out_w = heads * HEAD_DIM
    return pl.pallas_call(
        functools.partial(_fox_kernel, tq=tq, heads=heads),
        grid=(b, nh // heads, s // tq),
        in_specs=[pl.BlockSpec((1, heads, tq, LANES), lambda bi, g, i: (bi, g, i, 0)),
                  _resident((1, heads, LANES, s), lambda bi, g, i: (bi, g, 0, 0)),
                  _resident((1, heads, s, LANES), lambda bi, g, i: (bi, g, 0, 0))],
        out_specs=pl.BlockSpec((1, tq, out_w), lambda bi, g, i: (bi, i, g)),
        out_shape=jax.ShapeDtypeStruct((b, s, N_HEADS * HEAD_DIM), BF16),
        scratch_shapes=[pltpu.VMEM((heads, tq, LANES), F32), pltpu.VMEM((heads, tq, LANES), F32)],
        compiler_params=_params("parallel", "parallel", "arbitrary"),
        name="fox",
    )(q, kt, v)


def _convmod_kernel(u_ref, cw_ref, cb_ref, g_ref, b_ref, o_ref, xbuf_ref, *, ts, width, halo):
    @pl.when(pl.program_id(1) == 0)
    def _():
        xbuf_ref[0:halo, :] = jnp.zeros((halo, width), F32)

    x = u_ref[0, :, 0:width] * jax.nn.sigmoid(u_ref[0, :, width:2 * width])
    xbuf_ref[halo:halo + ts, :] = x
    base = halo - (CONV_KERNEL - 1)
    y = cb_ref[...] + cw_ref[0:1, :] * xbuf_ref[pl.ds(base, ts), :]
    for j in range(1, CONV_KERNEL):
        y = y + cw_ref[j:j + 1, :] * xbuf_ref[pl.ds(base + j, ts), :]
    xbuf_ref[0:halo, :] = x[ts - halo:ts, :]
    mu = jnp.mean(y, axis=-1, keepdims=True)
    var = jnp.mean(jnp.square(y - mu), axis=-1, keepdims=True)
    z = (y - mu) * lax.rsqrt(var + NORM_EPS) * g_ref[...] + b_ref[...]
    o_ref[0] = (z * jax.nn.sigmoid(z)).astype(o_ref.dtype)


def _convmod(u, conv_w, conv_b, ln_g, ln_b, ts=512, halo=32):
    b, s, w2 = u.shape
    width = w2 // 2
    assert s % ts == 0 and halo >= CONV_KERNEL - 1
    row = lambda v: v.reshape(1, width)
    const = lambda shape: _resident(shape, lambda bi, si: (0, 0))
    return pl.pallas_call(
        functools.partial(_convmod_kernel, ts=ts, width=width, halo=halo),
        grid=(b, s // ts),
        in_specs=[pl.BlockSpec((1, ts, w2), lambda bi, si: (bi, si, 0)),
                  const((CONV_KERNEL, width)), const((1, width)), const((1, width)), const((1, width))],
        out_specs=pl.BlockSpec((1, ts, width), lambda bi, si: (bi, si, 0)),
        out_shape=jax.ShapeDtypeStruct((b, s, width), BF16),
        scratch_shapes=[pltpu.VMEM((ts + halo, width), F32)],
        compiler_params=_params("parallel", "arbitrary"),
        name="conv_module",
    )(u, conv_w, row(conv_b), row(ln_g), row(ln_b))


def _out_ffn_kernel(x_ref, ya_ref, yb_ref, wo_ref, g_ref, wgu_ref, wd_ref, o_ref, *, hidden, th):
    half = ya_ref.shape[1]
    x = x_ref[...] + (jnp.dot(ya_ref[...], wo_ref[0:half, :], preferred_element_type=F32)
                      + jnp.dot(yb_ref[...], wo_ref[half:2 * half, :], preferred_element_type=F32))
    ms = jnp.mean(x * x, axis=-1, keepdims=True)
    h = (x * lax.rsqrt(ms + NORM_EPS) * g_ref[...]).astype(BF16)
    y = x
    for c in range(0, hidden, th):
        gate = jnp.dot(h, wgu_ref[:, c:c + th], preferred_element_type=F32)
        up = jnp.dot(h, wgu_ref[:, hidden + c:hidden + c + th], preferred_element_type=F32)
        act = (gate * jax.nn.sigmoid(gate) * up).astype(BF16)
        y = y + jnp.dot(act, wd_ref[c:c + th, :], preferred_element_type=F32)
    o_ref[...] = y


def _out_ffn(x2, ya, yb, w_out, g, w_gu, w_down, tm=512, th=256):
    n, d = x2.shape
    half = ya.shape[1]
    hidden = w_down.shape[0]
    assert n % tm == 0 and hidden % th == 0
    const = lambda shape: _resident(shape, lambda i: (0, 0))
    return pl.pallas_call(
        functools.partial(_out_ffn_kernel, hidden=hidden, th=th),
        grid=(n // tm,),
        in_specs=[pl.BlockSpec((tm, d), lambda i: (i, 0)),
                  pl.BlockSpec((tm, half), lambda i: (i, 0)), pl.BlockSpec((tm, half), lambda i: (i, 0)),
                  const(w_out.shape), const((1, d)), const(w_gu.shape), const(w_down.shape)],
        out_specs=pl.BlockSpec((tm, d), lambda i: (i, 0)),
        out_shape=jax.ShapeDtypeStruct((n, d), F32),
        compiler_params=_params("parallel"),
        name="out_ffn",
    )(x2, ya, yb, w_out, g.reshape(1, d), w_gu, w_down)


def _rope_tables(s, width):
    inv = ROPE_THETA ** (-jnp.arange(0, HEAD_DIM, 2, dtype=F32) / HEAD_DIM)
    ang = jnp.arange(s, dtype=F32)[:, None] * inv[None, :]
    cos, sin = jnp.cos(ang), jnp.sin(ang)
    reps = width // HEAD_DIM
    return (jnp.tile(jnp.concatenate([cos, cos], axis=-1), (1, reps)),
            jnp.tile(jnp.concatenate([-sin, sin], axis=-1), (1, reps)))


def _block_diag(blocks):
    n, d, _ = blocks.shape
    eye = jnp.eye(n, dtype=blocks.dtype)
    return jnp.einsum('nde,nm->ndme', blocks, eye).reshape(n * d, n * d)


def _pad_cols(w, total):
    return jnp.pad(w, ((0, 0), (0, total - w.shape[1])))


def kernel(x, norm_mix, norm_ffn,
           ev_w_in, ev_conv_w, ev_conv_b, ev_w_r, ev_b_r, ev_w_i, ev_b_i, ev_lam,
           ev_q_norm, ev_k_norm, ev_w_out,
           od_w_in, od_b_f, od_q_norm, od_k_norm, od_conv_w, od_conv_b, od_ln_g, od_ln_b,
           od_w_out, ffn_w_gu, ffn_w_down):
    b, s, d = x.shape
    depth = norm_mix.shape[0]
    w = N_HEADS * HEAD_DIM
    cos, sin = _rope_tables(s, w)
    head_ones = _block_diag(jnp.ones((N_HEADS, HEAD_DIM, HEAD_DIM), BF16))
    tile_gain = lambda g: jnp.tile(g, N_HEADS).reshape(1, w)
    x2 = x.reshape(b * s, d)
    seq = lambda t: t.reshape(b, s, t.shape[-1])
    for l in range(depth):
        j = l // 2
        if l % 2 == 0:
            w_in = _pad_cols(ev_w_in[j], 2 * w + 4 * w + LANES).astype(BF16)
            xg, q, k, v, qi, kw = _norm_matmul(x2, norm_mix[l], w_in, (2 * w, w, w, w, w, LANES))
            ya = _lru(seq(xg), ev_conv_w[j], ev_conv_b[j],
                      _block_diag(ev_w_r[j]).astype(BF16), ev_b_r[j],
                      _block_diag(ev_w_i[j]).astype(BF16), ev_b_i[j], ev_lam[j])
            qh, kt, vh, qih, kit = _prep_even(seq(q), seq(k), seq(v), seq(qi), seq(kw), cos, sin,
                                              tile_gain(ev_q_norm[j]), tile_gain(ev_k_norm[j]), head_ones)
            yb = _dsa(qih, seq(kw), kit, qh, kt, vh)
            w_out = ev_w_out[j]
        else:
            wi = od_w_in[j]
            w_in = jnp.concatenate([wi[:, 0:3 * w], wi[:, 3 * w + N_HEADS:],
                                    _pad_cols(wi[:, 3 * w:3 * w + N_HEADS], LANES)], axis=1).astype(BF16)
            q, k, v, u, fl = _norm_matmul(x2, norm_mix[l], w_in, (w, w, w, 2 * w, LANES))
            qh, kt, vh = _prep_odd(seq(q), seq(k), seq(v), seq(fl),
                                   _pad_cols(od_b_f[j].reshape(1, N_HEADS), LANES),
                                   tile_gain(od_q_norm[j]), tile_gain(od_k_norm[j]), head_ones)
            ya = _fox(qh, kt, vh, tq=min(512, s))
            yb = _convmod(seq(u), od_conv_w[j], od_conv_b[j], od_ln_g[j], od_ln_b[j])
            w_out = od_w_out[j]
        x2 = _out_ffn(x2, ya.reshape(b * s, w), yb.reshape(b * s, w), w_out.astype(BF16),
                      norm_ffn[l], ffn_w_gu[l].astype(BF16), ffn_w_down[l].astype(BF16))
    return x2.reshape(b, s, d)
```

```python
import functools

import jax
import jax.numpy as jnp
from jax import lax
from jax.experimental import pallas as pl
from jax.experimental.pallas import tpu as pltpu

F32 = jnp.float32
BF16 = jnp.bfloat16

HEAD_DIM = 64
N_HEADS = 8
N_PAIRS = N_HEADS // 2
LANES = 128
CHUNK = 64
CHUNK_SHIFT = 6
TOPK_MAX = 256
ROPE_THETA = 10000.0
NORM_EPS = 1e-6
LRU_C = 8.0
LRU_CONV = 4
CONV_KERNEL = 31
MASKED = -2.0 ** 100
LOG2E = 1.4426950408889634
QK_SCALE_LOG2 = HEAD_DIM ** -0.5 * LOG2E
VMEM_LIMIT = 56 * 1024 * 1024

assert CHUNK == 1 << CHUNK_SHIFT


def _params(*sem):
    return pltpu.CompilerParams(dimension_semantics=sem, vmem_limit_bytes=VMEM_LIMIT)


def _resident(shape, index_map):
    return pl.BlockSpec(shape, index_map, pipeline_mode=pl.Buffered(1))


def _norm_matmul_kernel(x_ref, g_ref, w_ref, *o_refs, widths):
    x = x_ref[...]
    ms = jnp.mean(x * x, axis=-1, keepdims=True)
    h = (x * lax.rsqrt(ms + NORM_EPS) * g_ref[...]).astype(BF16)
    off = 0
    for o_ref, wd in zip(o_refs, widths):
        o_ref[...] = jnp.dot(h, w_ref[:, off:off + wd], preferred_element_type=F32)
        off += wd


def _norm_matmul(x2, g, w, widths, tm=512):
    n, d = x2.shape
    assert n % tm == 0 and sum(widths) == w.shape[1]
    return pl.pallas_call(
        functools.partial(_norm_matmul_kernel, widths=widths),
        grid=(n // tm,),
        in_specs=[pl.BlockSpec((tm, d), lambda i: (i, 0)),
                  _resident((1, d), lambda i: (0, 0)),
                  _resident(w.shape, lambda i: (0, 0))],
        out_specs=[pl.BlockSpec((tm, wd), lambda i: (i, 0)) for wd in widths],
        out_shape=[jax.ShapeDtypeStruct((n, wd), F32) for wd in widths],
        compiler_params=_params("parallel"),
        name="norm_matmul",
    )(x2, g.reshape(1, d), w)


def _shift_rows(x, d, fill):
    row = lax.broadcasted_iota(jnp.int32, x.shape, 0)
    return jnp.where(row >= d, pltpu.roll(x, d, 0), fill)


def _linear_scan_rows(a, u):
    d = 1
    while d < a.shape[0]:
        u = a * _shift_rows(u, d, 0.0) + u
        a = a * _shift_rows(a, d, 1.0)
        d *= 2
    return a, u


def _cumsum_rows(x):
    d = 1
    while d < x.shape[0]:
        x = x + _shift_rows(x, d, 0.0)
        d *= 2
    return x


def _softplus(x):
    return jnp.maximum(x, 0.0) + jnp.log1p(jnp.exp(-jnp.abs(x)))


def _lru_kernel(xg_ref, cw_ref, cb_ref, wr_ref, br_ref, wi_ref, bi_ref, lam_ref, o_ref,
                xbuf_ref, h_ref, *, ts, width):
    @pl.when(pl.program_id(1) == 0)
    def _():
        xbuf_ref[0:8, :] = jnp.zeros((8, width), F32)
        h_ref[...] = jnp.zeros_like(h_ref)

    xa = xg_ref[0, :, 0:width]
    ga = xg_ref[0, :, width:2 * width]
    xbuf_ref[8:8 + ts, :] = xa
    xc = cb_ref[...] + cw_ref[0:1, :] * xbuf_ref[pl.ds(8 - (LRU_CONV - 1), ts), :]
    for j in range(1, LRU_CONV):
        xc = xc + cw_ref[j:j + 1, :] * xbuf_ref[pl.ds(8 - (LRU_CONV - 1) + j, ts), :]
    xbuf_ref[0:8, :] = xa[ts - 8:ts, :]

    xb = xc.astype(BF16)
    r = jax.nn.sigmoid(jnp.dot(xb, wr_ref[...], preferred_element_type=F32) + br_ref[...])
    gate = jax.nn.sigmoid(jnp.dot(xb, wi_ref[...], preferred_element_type=F32) + bi_ref[...])
    log_a = -LRU_C * r * _softplus(-lam_ref[...])
    a = jnp.exp(log_a)
    th = jnp.tanh(log_a)
    u = jnp.sqrt(-2.0 * th / (1.0 - th)) * (gate * xc)
    a_cum, h = _linear_scan_rows(a, u)
    h = h + a_cum * h_ref[...]
    h_ref[...] = h[ts - 1:ts, :]
    o_ref[0] = (h * jax.nn.gelu(ga, approximate=True)).astype(o_ref.dtype)


def _lru(xg, conv_w, conv_b, w_r, b_r, w_i, b_i, lam, ts=256):
    b, s, w2 = xg.shape
    width = w2 // 2
    assert s % ts == 0
    row = lambda v: v.reshape(1, width)
    const = lambda shape: _resident(shape, lambda bi, si: (0, 0))
    return pl.pallas_call(
        functools.partial(_lru_kernel, ts=ts, width=width),
        grid=(b, s // ts),
        in_specs=[pl.BlockSpec((1, ts, w2), lambda bi, si: (bi, si, 0)),
                  const((LRU_CONV, width)), const((1, width)),
                  const((width, width)), const((1, width)),
                  const((width, width)), const((1, width)), const((1, width))],
        out_specs=pl.BlockSpec((1, ts, width), lambda bi, si: (bi, si, 0)),
        out_shape=jax.ShapeDtypeStruct((b, s, width), BF16),
        scratch_shapes=[pltpu.VMEM((ts + 8, width), F32), pltpu.VMEM((1, width), F32)],
        compiler_params=_params("parallel", "arbitrary"),
        name="rg_lru",
    )(xg, conv_w, row(conv_b), w_r, row(b_r), w_i, row(b_i), row(lam))


def _head_mean_square(x, bd_ref):
    x2 = x * x
    hi = x2.astype(BF16)
    r1 = x2 - hi.astype(F32)
    mid = r1.astype(BF16)
    lo = (r1 - mid.astype(F32)).astype(BF16)
    bd = bd_ref[...]
    tot = (jnp.dot(hi, bd, preferred_element_type=F32) + jnp.dot(mid, bd, preferred_element_type=F32)
           + jnp.dot(lo, bd, preferred_element_type=F32))
    return tot * (1.0 / HEAD_DIM)


def _head_rms_norm(x, g, bd_ref):
    return x * lax.rsqrt(_head_mean_square(x, bd_ref) + NORM_EPS) * g


def _rope(x, cos, sin_signed):
    n = x.shape[1]
    lane = lax.broadcasted_iota(jnp.int32, x.shape, 1)
    first_half = (lane & (HEAD_DIM - 1)) < HEAD_DIM // 2
    partner = jnp.where(first_half, pltpu.roll(x, n - HEAD_DIM // 2, 1), pltpu.roll(x, HEAD_DIM // 2, 1))
    return x * cos + partner * sin_signed


def _prep_even_kernel(q_ref, k_ref, v_ref, qi_ref, kw_ref, cos_ref, sin_ref, gq_ref, gk_ref, bd_ref,
                      qo_ref, kto_ref, vo_ref, qio_ref, kito_ref):
    cos = cos_ref[...]
    sin = sin_ref[...]
    q = _rope(_head_rms_norm(q_ref[0], gq_ref[...], bd_ref), cos, sin)
    qo_ref[0] = (q * QK_SCALE_LOG2).astype(BF16)
    k = _rope(_head_rms_norm(k_ref[0], gk_ref[...], bd_ref), cos, sin)
    kto_ref[0] = k.T.astype(BF16)
    vo_ref[0] = v_ref[0].astype(BF16)
    qio_ref[0] = _rope(qi_ref[0], cos, sin).astype(BF16)
    ki = _rope(kw_ref[0], cos[:, 0:LANES], sin[:, 0:LANES])
    lane = lax.broadcasted_iota(jnp.int32, ki.shape, 1)
    ki2 = jnp.where(lane < HEAD_DIM, ki, pltpu.roll(ki, HEAD_DIM, 1))
    kito_ref[0] = ki2.T.astype(BF16)


def _prep_even(q, k, v, qi, kw, cos, sin, gq, gk, bd, ts=512):
    b, s, w = q.shape
    assert s % ts == 0
    tile = pl.BlockSpec((1, ts, w), lambda bi, si: (bi, si, 0))
    tab = pl.BlockSpec((ts, w), lambda bi, si: (si, 0))
    const = lambda shape: _resident(shape, lambda bi, si: (0, 0))
    return pl.pallas_call(
        _prep_even_kernel,
        grid=(b, s // ts),
        in_specs=[tile, tile, tile, tile, pl.BlockSpec((1, ts, LANES), lambda bi, si: (bi, si, 0)),
                  tab, tab, const((1, w)), const((1, w)), const((w, w))],
        out_specs=[tile, pl.BlockSpec((1, w, ts), lambda bi, si: (bi, 0, si)), tile, tile,
                   pl.BlockSpec((1, LANES, ts), lambda bi, si: (bi, 0, si))],
        out_shape=[jax.ShapeDtypeStruct((b, s, w), BF16), jax.ShapeDtypeStruct((b, w, s), BF16),
                   jax.ShapeDtypeStruct((b, s, w), BF16), jax.ShapeDtypeStruct((b, s, w), BF16),
                   jax.ShapeDtypeStruct((b, LANES, s), BF16)],
        compiler_params=_params("parallel", "parallel"),
        name="prep_even",
    )(q, k, v, qi, kw, cos, sin, gq, gk, bd)


def _prep_odd_kernel(q_ref, k_ref, v_ref, fl_ref, bf_ref, gq_ref, gk_ref, bd_ref,
                     qo_ref, kto_ref, vo_ref, carry_ref):
    @pl.when(pl.program_id(1) == 0)
    def _():
        carry_ref[...] = jnp.zeros_like(carry_ref)

    q = _head_rms_norm(q_ref[0], gq_ref[...], bd_ref) * QK_SCALE_LOG2
    k = _head_rms_norm(k_ref[0], gk_ref[...], bd_ref)
    v = v_ref[0]
    log_f = -_softplus(-(fl_ref[0] + bf_ref[...]))
    c = _cumsum_rows(log_f) + carry_ref[...]
    carry_ref[...] = c[c.shape[0] - 1:, :]
    c2 = c * LOG2E
    lane = lax.broadcasted_iota(jnp.int32, (c.shape[0], LANES), 1)
    for h in range(N_HEADS):
        p, e = divmod(h, 2)
        pick = lambda t: t[:, p * LANES:(p + 1) * LANES] if e == 0 else pltpu.roll(t[:, p * LANES:(p + 1) * LANES], HEAD_DIM, 1)
        cb = jnp.broadcast_to(c2[:, h:h + 1], (c.shape[0], LANES))
        hi = cb.astype(BF16).astype(F32)
        mid = (cb - hi).astype(BF16).astype(F32)
        lo = cb - hi - mid
        q_bias = jnp.where(lane == HEAD_DIM, hi, jnp.where(lane == HEAD_DIM + 1, mid, jnp.where(
            lane == HEAD_DIM + 2, lo, jnp.where(lane < HEAD_DIM + 6, 1.0, 0.0))))
        k_bias = jnp.where(lane < HEAD_DIM + 3, 1.0, jnp.where(lane == HEAD_DIM + 3, -hi, jnp.where(
            lane == HEAD_DIM + 4, -mid, jnp.where(lane == HEAD_DIM + 5, -lo, 0.0))))
        qo_ref[0, h] = jnp.where(lane < HEAD_DIM, pick(q), q_bias).astype(BF16)
        kto_ref[0, h] = jnp.where(lane < HEAD_DIM, pick(k), k_bias).T.astype(BF16)
        vo_ref[0, h] = jnp.where(lane < HEAD_DIM, pick(v), 1.0).astype(BF16)


def _prep_odd(q, k, v, fl, bf, gq, gk, bd, ts=512):
    b, s, w = q.shape
    assert s % ts == 0
    tile = pl.BlockSpec((1, ts, w), lambda bi, si: (bi, si, 0))
    narrow = pl.BlockSpec((1, ts, LANES), lambda bi, si: (bi, si, 0))
    heads = pl.BlockSpec((1, N_HEADS, ts, LANES), lambda bi, si: (bi, 0, si, 0))
    const = lambda shape: _resident(shape, lambda bi, si: (0, 0))
    return pl.pallas_call(
        _prep_odd_kernel,
        grid=(b, s // ts),
        in_specs=[tile, tile, tile, narrow, const((1, LANES)), const((1, w)), const((1, w)), const((w, w))],
        out_specs=[heads, pl.BlockSpec((1, N_HEADS, LANES, ts), lambda bi, si: (bi, 0, 0, si)), heads],
        out_shape=[jax.ShapeDtypeStruct((b, N_HEADS, s, LANES), BF16),
                   jax.ShapeDtypeStruct((b, N_HEADS, LANES, s), BF16),
                   jax.ShapeDtypeStruct((b, N_HEADS, s, LANES), BF16)],
        scratch_shapes=[pltpu.VMEM((1, LANES), F32)],
        compiler_params=_params("parallel", "arbitrary"),
        name="prep_odd",
    )(q, k, v, fl, bf, gq, gk, bd)


def _head_halves(x_pair):
    lane = lax.broadcasted_iota(jnp.int32, x_pair.shape, 1)
    zero = jnp.zeros_like(x_pair)
    return jnp.where(lane < HEAD_DIM, x_pair, zero), jnp.where(lane >= HEAD_DIM, x_pair, zero)


def _fold_lanes(x, op):
    out = x[:, 0:LANES]
    for c in range(1, x.shape[1] // LANES):
        out = op(out, x[:, c * LANES:(c + 1) * LANES])
    return out


def _online_softmax_block(scores, values, m_ref, acc_ref):
    probs, alphas = [], []
    for h, s in enumerate(scores):
        m_old = m_ref[h]
        m_new = jnp.maximum(m_old, jnp.max(_fold_lanes(s, jnp.maximum), axis=1, keepdims=True).astype(F32))
        p = jnp.exp2(s - jnp.tile(m_new.astype(s.dtype), (1, s.shape[1] // LANES)))
        m_ref[h] = m_new
        probs.append(p.astype(BF16))
        alphas.append(jnp.exp2(m_old - m_new))
    for h, (p, alpha) in enumerate(zip(probs, alphas)):
        acc_ref[h] = alpha * acc_ref[h] + jnp.dot(p, values[h], preferred_element_type=F32)


def _to_ordinal(x):
    bits = lax.bitcast_convert_type(x, jnp.int32)
    return jnp.where(bits < 0, bits ^ jnp.int32(0x7FFFFFFF), bits)


def _from_ordinal(o):
    return lax.bitcast_convert_type(jnp.where(o < 0, o ^ jnp.int32(0x7FFFFFFF), o), F32)


def _ordinal_midpoint(lo, hi):
    a, b = _to_ordinal(lo), _to_ordinal(hi)
    return _from_ordinal((a >> 1) + (b >> 1) + (a & b & 1))


def _next_float_up(x):
    return _from_ordinal(_to_ordinal(x) + 1)


def _normalise(acc):
    return acc / pltpu.roll(acc, HEAD_DIM, 1)


def _dsa_index_kernel(qi_ref, kw_ref, kit_ref, mask_ref, sc_ref, *, tq, tk, topk, max_iters):
    t0 = pl.program_id(1) * tq
    nb = (t0 + tq + tk - 1) // tk
    row_chunk = (t0 + lax.broadcasted_iota(jnp.int32, (tq, 1), 0)) >> CHUNK_SHIFT
    kf = float(topk)

    def cols(j):
        return pl.ds(pl.multiple_of(j * tk, tk), tk)

    qi_heads = []
    for p in range(N_PAIRS):
        qi_heads.extend(_head_halves(qi_ref[0, :, p * LANES:(p + 1) * LANES]))
    kw = kw_ref[0]

    def score_block(j, carry):
        kit = kit_ref[0, :, cols(j)]
        score = jnp.zeros((tq, tk), F32)
        for h in range(N_HEADS):
            rel = jnp.maximum(jnp.dot(qi_heads[h], kit, preferred_element_type=F32), 0.0)
            score = score + kw[:, HEAD_DIM + h:HEAD_DIM + h + 1] * rel
        key_chunk = (j * tk + lax.broadcasted_iota(jnp.int32, (1, tk), 1)) >> CHUNK_SHIFT
        sc_ref[:, cols(j)] = jnp.where(key_chunk <= row_chunk, score, -jnp.inf)
        return carry

    lax.fori_loop(0, nb, score_block, 0)

    reps = tk // LANES

    def to_col(row):
        return jnp.broadcast_to(row, (LANES, tq)).T

    def to_row(wide, reduce):
        return reduce(wide.T, axis=0, keepdims=True)

    def count(row, strict=False):
        col = to_col(row)
        parts = []
        for r in range(0, tq, LANES):
            bound = jnp.tile(col[r:r + LANES], (1, reps))

            def body(j, cnt):
                blk = sc_ref[r:r + LANES, cols(j)]
                hit = blk > bound if strict else blk >= bound
                return cnt + _fold_lanes(jnp.where(hit, 1.0, 0.0), jnp.add)

            parts.append(lax.fori_loop(0, nb, body, jnp.zeros((LANES, LANES), F32)))
        return to_row(jnp.concatenate(parts, axis=0), jnp.sum)

    def range_body(j, carry):
        lo, hi = carry
        blk = sc_ref[:, cols(j)]
        lo = jnp.minimum(lo, _fold_lanes(jnp.where(blk == -jnp.inf, jnp.inf, blk), jnp.minimum))
        return lo, jnp.maximum(hi, _fold_lanes(blk, jnp.maximum))

    lo_w, hi_w = lax.fori_loop(0, nb, range_body,
                               (jnp.full((tq, LANES), jnp.inf, F32), jnp.full((tq, LANES), -jnp.inf, F32)))
    lo0 = to_row(lo_w, jnp.min)
    hi0 = _next_float_up(to_row(hi_w, jnp.max))
    n_visible = (((t0 + lax.broadcasted_iota(jnp.int32, (1, tq), 1)) >> CHUNK_SHIFT) + 1) * CHUNK
    short_row = n_visible <= topk
    log_k = jnp.log(kf + 0.5)

    def open_rows(lo, hi, c_lo):
        finished = jnp.logical_or(jnp.logical_or(c_lo == kf, short_row),
                                  _to_ordinal(_ordinal_midpoint(lo, hi)) == _to_ordinal(lo))
        return jnp.max(jnp.where(finished, 0.0, 1.0))

    def search_cond(state):
        return jnp.logical_and(state[0] < max_iters, state[-1] > 0.0)

    def search_body(state):
        it, lo, hi, c_lo, f_lo, f_hi, side, _ = state
        guess = lo + (hi - lo) * jnp.clip(f_lo / (f_lo - f_hi), 1.0 / 64, 63.0 / 64)
        inside = jnp.logical_and(guess > lo, guess < hi)
        half = _ordinal_midpoint(lo, hi)
        mid = jnp.where(it % 3 == 2, half, jnp.where(inside, guess, half))
        c_mid = count(mid)
        enough = c_mid >= kf
        f_mid = jnp.log(jnp.maximum(c_mid, 0.5)) - log_k
        f_hi = jnp.where(enough, jnp.where(side > 0.0, f_hi * 0.5, f_hi), f_mid)
        f_lo = jnp.where(enough, f_mid, jnp.where(side < 0.0, f_lo * 0.5, f_lo))
        lo, c_lo = jnp.where(enough, mid, lo), jnp.where(enough, c_mid, c_lo)
        hi = jnp.where(enough, hi, mid)
        return it + 1, lo, hi, c_lo, f_lo, f_hi, jnp.where(enough, 1.0, -1.0), open_rows(lo, hi, c_lo)

    c_lo0 = n_visible.astype(F32)
    state0 = (jnp.int32(0), lo0, hi0, c_lo0, jnp.log(c_lo0) - log_k, jnp.log(0.5) - log_k + jnp.zeros_like(lo0),
              jnp.zeros_like(lo0), open_rows(lo0, hi0, c_lo0))
    _, lo, _, n_ge, _, _, _, _ = lax.while_loop(search_cond, search_body, state0)
    thr_row = jnp.where(short_row, jnp.finfo(F32).min, lo)
    thr = jnp.tile(to_col(thr_row), (1, reps))

    has_extra = jnp.max(jnp.where(jnp.logical_and(n_ge > kf, jnp.logical_not(short_row)), 1.0, 0.0))

    @pl.when(has_extra > 0.0)
    def _():
        keep = to_col(kf - count(thr_row, strict=True))[:, 0:1]
        upper = (lax.broadcasted_iota(jnp.int32, (tk, tk), 0)
                 <= lax.broadcasted_iota(jnp.int32, (tk, tk), 1))
        prefix = jnp.where(upper, 1.0, 0.0).astype(BF16)

        def body(j, seen):
            blk = sc_ref[:, cols(j)]
            tie = jnp.where(blk == thr, 1.0, 0.0)
            rank = seen + jnp.dot(tie.astype(BF16), prefix, preferred_element_type=F32)
            drop = jnp.where(rank > keep, tie, 0.0)
            sc_ref[:, cols(j)] = jnp.where(drop > 0.0, -jnp.inf, blk)
            return seen + jnp.sum(tie, axis=1, keepdims=True)

        lax.fori_loop(0, nb, body, jnp.zeros((tq, 1), F32))

    def mask_block(j, carry):
        mask_ref[0, :, cols(j)] = jnp.where(sc_ref[:, cols(j)] >= thr, 1, 0).astype(mask_ref.dtype)
        return carry

    def clear_block(j, carry):
        mask_ref[0, :, cols(j)] = jnp.zeros((tq, tk), mask_ref.dtype)
        return carry

    lax.fori_loop(0, nb, mask_block, 0)
    lax.fori_loop(nb, mask_ref.shape[2] // tk, clear_block, 0)


def _dsa_index(qi, kw, kit, tq, tk):
    b, s, w = qi.shape
    assert s % tk == 0 and s % tq == 0 and tq % CHUNK == 0
    topk = min(TOPK_MAX, s // 4)
    return pl.pallas_call(
        functools.partial(_dsa_index_kernel, tq=tq, tk=tk, topk=topk, max_iters=640),
        grid=(b, s // tq),
        in_specs=[pl.BlockSpec((1, tq, w), lambda bi, i: (bi, i, 0)),
                  pl.BlockSpec((1, tq, LANES), lambda bi, i: (bi, i, 0)),
                  _resident((1, LANES, s), lambda bi, i: (bi, 0, 0))],
        out_specs=pl.BlockSpec((1, tq, s), lambda bi, i: (bi, i, 0)),
        out_shape=jax.ShapeDtypeStruct((b, s, s), jnp.int8),
        scratch_shapes=[pltpu.VMEM((tq, s), F32)],
        compiler_params=_params("parallel", "arbitrary"),
        name="dsa_index",
    )(qi, kw, kit)


def _dsa_attend_kernel(q_ref, kt_ref, v_ref, mask_ref, o_ref, m_ref, acc_ref, *, tq, tk):
    nb = (pl.program_id(1) * tq + tq + tk - 1) // tk
    m_ref[...] = jnp.full_like(m_ref, MASKED)
    acc_ref[...] = jnp.zeros_like(acc_ref)
    q_heads = []
    for p in range(N_PAIRS):
        q_heads.extend(_head_halves(q_ref[0, :, p * LANES:(p + 1) * LANES]))
    first_half = lax.broadcasted_iota(jnp.int32, (tk, LANES), 1) < HEAD_DIM
    one = jnp.ones((), BF16)

    def attend_block(j, carry):
        cols = pl.ds(pl.multiple_of(j * tk, tk), tk)
        bias = jnp.where(mask_ref[0, :, cols].astype(jnp.int32) > 0, 0.0, MASKED).astype(BF16)
        scores, values = [], []
        for h in range(N_HEADS):
            p, e = divmod(h, 2)
            kt = kt_ref[0, p * LANES:(p + 1) * LANES, cols]
            scores.append(jnp.dot(q_heads[h], kt, preferred_element_type=F32).astype(BF16) + bias)
            v_pair = v_ref[0, cols, p * LANES:(p + 1) * LANES]
            values.append(jnp.where(first_half, v_pair, one) if e == 0 else jnp.where(first_half, one, v_pair))
        _online_softmax_block(scores, values, m_ref, acc_ref)
        return carry

    lax.fori_loop(0, nb, attend_block, 0)
    lane = lax.broadcasted_iota(jnp.int32, (tq, LANES), 1)
    for p in range(N_PAIRS):
        o_ref[0, :, p * LANES:(p + 1) * LANES] = jnp.where(
            lane < HEAD_DIM, _normalise(acc_ref[2 * p]), _normalise(acc_ref[2 * p + 1])).astype(o_ref.dtype)


def _dsa_attend(q, kt, v, mask, tq, tk):
    b, s, w = q.shape
    assert s % tk == 0 and s % tq == 0
    qtile = pl.BlockSpec((1, tq, w), lambda bi, i: (bi, i, 0))
    return pl.pallas_call(
        functools.partial(_dsa_attend_kernel, tq=tq, tk=tk),
        grid=(b, s // tq),
        in_specs=[qtile,
                  _resident((1, w, s), lambda bi, i: (bi, 0, 0)),
                  _resident((1, s, w), lambda bi, i: (bi, 0, 0)),
                  pl.BlockSpec((1, tq, s), lambda bi, i: (bi, i, 0))],
        out_specs=qtile,
        out_shape=jax.ShapeDtypeStruct((b, s, w), BF16),
        scratch_shapes=[pltpu.VMEM((N_HEADS, tq, LANES), F32), pltpu.VMEM((N_HEADS, tq, LANES), F32)],
        compiler_params=_params("parallel", "arbitrary"),
        name="dsa_attend",
    )(q, kt, v, mask)


def _dsa(qi, kw, kit, q, kt, v, tq=256, tk=512):
    s = q.shape[1]
    mask = _dsa_index(qi, kw, kit, min(tq, s), min(tk, s))
    return _dsa_attend(q, kt, v, mask, min(tq, s), min(tk, s))


def _fox_kernel(q_ref, kt_ref, v_ref, o_ref, m_ref, acc_ref, *, tq, heads):
    i = pl.program_id(2)
    m_ref[...] = jnp.full_like(m_ref, MASKED)
    acc_ref[...] = jnp.zeros_like(acc_ref)
    causal = (lax.broadcasted_iota(jnp.int32, (tq, tq), 1) <= lax.broadcasted_iota(jnp.int32, (tq, tq), 0))

    def block(j, diagonal):
        cols = pl.ds(pl.multiple_of(j * tq, tq), tq)
        scores = []
        for e in range(heads):
            s = jnp.dot(q_ref[0, e], kt_ref[0, e, :, cols], preferred_element_type=F32)
            scores.append(jnp.where(causal, s, MASKED) if diagonal else s)
        _online_softmax_block(scores, [v_ref[0, e, cols, :] for e in range(heads)], m_ref, acc_ref)

    def body(j, carry):
        block(j, False)
        return carry

    lax.fori_loop(0, i, body, 0)
    block(i, True)
    lane = lax.broadcasted_iota(jnp.int32, (tq, LANES), 1)
    for p in range(heads // 2):
        a0, a1 = acc_ref[2 * p], acc_ref[2 * p + 1]
        o_ref[0, :, p * LANES:(p + 1) * LANES] = jnp.where(
            lane < HEAD_DIM, a0 / pltpu.roll(a0, HEAD_DIM, 1), pltpu.roll(a1, HEAD_DIM, 1) / a1).astype(o_ref.dtype)


def _fox(q, kt, v, tq=512, heads=4):
    b, nh, s, _ = q.shape
    assert s % tq == 0 and nh == N_HEADS and nh % heads == 0 and heads % 2 == 0
    out_w = heads * HEAD_DIM
    return pl.pallas_call(
        functools.partial(_fox_kernel, tq=tq, heads=heads),
        grid=(b, nh // heads, s // tq),
        in_specs=[pl.BlockSpec((1, heads, tq, LANES), lambda bi, g, i: (bi, g, i, 0)),
                  _resident((1, heads, LANES, s), lambda bi, g, i: (bi, g, 0, 0)),
                  _resident((1, heads, s, LANES), lambda bi, g, i: (bi, g, 0, 0))],
        out_specs=pl.BlockSpec((1, tq, out_w), lambda bi, g, i: (bi, i, g)),
        out_shape=jax.ShapeDtypeStruct((b, s, N_HEADS * HEAD_DIM), BF16),
        scratch_shapes=[pltpu.VMEM((heads, tq, LANES), F32), pltpu.VMEM((heads, tq, LANES), F32)],
        compiler_params=_params("parallel", "parallel", "arbitrary"),
        name="fox",
    )(q, kt, v)


def _convmod_kernel(u_ref, cw_ref, cb_ref, g_ref, b_ref, o_ref, xbuf_ref, *, ts, width, halo):
    @pl.when(pl.program_id(1) == 0)
    def _():
        xbuf_ref[0:halo, :] = jnp.zeros((halo, width), F32)

    x = u_ref[0, :, 0:width] * jax.nn.sigmoid(u_ref[0, :, width:2 * width])
    xbuf_ref[halo:halo + ts, :] = x
    base = halo - (CONV_KERNEL - 1)
    y = cb_ref[...] + cw_ref[0:1, :] * xbuf_ref[pl.ds(base, ts), :]
    for j in range(1, CONV_KERNEL):
        y = y + cw_ref[j:j + 1, :] * xbuf_ref[pl.ds(base + j, ts), :]
    xbuf_ref[0:halo, :] = x[ts - halo:ts, :]
    mu = jnp.mean(y, axis=-1, keepdims=True)
    var = jnp.mean(jnp.square(y - mu), axis=-1, keepdims=True)
    z = (y - mu) * lax.rsqrt(var + NORM_EPS) * g_ref[...] + b_ref[...]
    o_ref[0] = (z * jax.nn.sigmoid(z)).astype(o_ref.dtype)


def _convmod(u, conv_w, conv_b, ln_g, ln_b, ts=512, halo=32):
    b, s, w2 = u.shape
    width = w2 // 2
    assert s % ts == 0 and halo >= CONV_KERNEL - 1
    row = lambda v: v.reshape(1, width)
    const = lambda shape: _resident(shape, lambda bi, si: (0, 0))
    return pl.pallas_call(
        functools.partial(_convmod_kernel, ts=ts, width=width, halo=halo),
        grid=(b, s // ts),
        in_specs=[pl.BlockSpec((1, ts, w2), lambda bi, si: (bi, si, 0)),
                  const((CONV_KERNEL, width)), const((1, width)), const((1, width)), const((1, width))],
        out_specs=pl.BlockSpec((1, ts, width), lambda bi, si: (bi, si, 0)),
        out_shape=jax.ShapeDtypeStruct((b, s, width), BF16),
        scratch_shapes=[pltpu.VMEM((ts + halo, width), F32)],
        compiler_params=_params("parallel", "arbitrary"),
        name="conv_module",
    )(u, conv_w, row(conv_b), row(ln_g), row(ln_b))


def _out_ffn_kernel(x_ref, ya_ref, yb_ref, wo_ref, g_ref, wgu_ref, wd_ref, o_ref, *, hidden, th):
    half = ya_ref.shape[1]
    x = x_ref[...] + (jnp.dot(ya_ref[...], wo_ref[0:half, :], preferred_element_type=F32)
                      + jnp.dot(yb_ref[...], wo_ref[half:2 * half, :], preferred_element_type=F32))
    ms = jnp.mean(x * x, axis=-1, keepdims=True)
    h = (x * lax.rsqrt(ms + NORM_EPS) * g_ref[...]).astype(BF16)
    y = x
    for c in range(0, hidden, th):
        gate = jnp.dot(h, wgu_ref[:, c:c + th], preferred_element_type=F32)
        up = jnp.dot(h, wgu_ref[:, hidden + c:hidden + c + th], preferred_element_type=F32)
        act = (gate * jax.nn.sigmoid(gate) * up).astype(BF16)
        y = y + jnp.dot(act, wd_ref[c:c + th, :], preferred_element_type=F32)
    o_ref[...] = y


def _out_ffn(x2, ya, yb, w_out, g, w_gu, w_down, tm=512, th=256):
    n, d = x2.shape
    half = ya.shape[1]
    hidden = w_down.shape[0]
    assert n % tm == 0 and hidden % th == 0
    const = lambda shape: _resident(shape, lambda i: (0, 0))
    return pl.pallas_call(
        functools.partial(_out_ffn_kernel, hidden=hidden, th=th),
        grid=(n // tm,),
        in_specs=[pl.BlockSpec((tm, d), lambda i: (i, 0)),
                  pl.BlockSpec((tm, half), lambda i: (i, 0)), pl.BlockSpec((tm, half), lambda i: (i, 0)),
                  const(w_out.shape), const((1, d)), const(w_gu.shape), const(w_down.shape)],
        out_specs=pl.BlockSpec((tm, d), lambda i: (i, 0)),
        out_shape=jax.ShapeDtypeStruct((n, d), F32),
        compiler_params=_params("parallel"),
        name="out_ffn",
    )(x2, ya, yb, w_out, g.reshape(1, d), w_gu, w_down)


def _rope_tables(s, width):
    inv = ROPE_THETA ** (-jnp.arange(0, HEAD_DIM, 2, dtype=F32) / HEAD_DIM)
    ang = jnp.arange(s, dtype=F32)[:, None] * inv[None, :]
    cos, sin = jnp.cos(ang), jnp.sin(ang)
    reps = width // HEAD_DIM
    return (jnp.tile(jnp.concatenate([cos, cos], axis=-1), (1, reps)),
            jnp.tile(jnp.concatenate([-sin, sin], axis=-1), (1, reps)))


def _block_diag(blocks):
    n, d, _ = blocks.shape
    eye = jnp.eye(n, dtype=blocks.dtype)
    return jnp.einsum('nde,nm->ndme', blocks, eye).reshape(n * d, n * d)


def _pad_cols(w, total):
    return jnp.pad(w, ((0, 0), (0, total - w.shape[1])))


def kernel(x, norm_mix, norm_ffn,
           ev_w_in, ev_conv_w, ev_conv_b, ev_w_r, ev_b_r, ev_w_i, ev_b_i, ev_lam,
           ev_q_norm, ev_k_norm, ev_w_out,
           od_w_in, od_b_f, od_q_norm, od_k_norm, od_conv_w, od_conv_b, od_ln_g, od_ln_b,
           od_w_out, ffn_w_gu, ffn_w_down):
    b, s, d = x.shape
    depth = norm_mix.shape[0]
    w = N_HEADS * HEAD_DIM
    cos, sin = _rope_tables(s, w)
    head_ones = _block_diag(jnp.ones((N_HEADS, HEAD_DIM, HEAD_DIM), BF16))
    tile_gain = lambda g: jnp.tile(g, N_HEADS).reshape(1, w)
    x2 = x.reshape(b * s, d)
    seq = lambda t: t.reshape(b, s, t.shape[-1])
    for l in range(depth):
        j = l // 2
        if l % 2 == 0:
            w_in = _pad_cols(ev_w_in[j], 2 * w + 4 * w + LANES).astype(BF16)
            xg, q, k, v, qi, kw = _norm_matmul(x2, norm_mix[l], w_in, (2 * w, w, w, w, w, LANES))
            ya = _lru(seq(xg), ev_conv_w[j], ev_conv_b[j],
                      _block_diag(ev_w_r[j]).astype(BF16), ev_b_r[j],
                      _block_diag(ev_w_i[j]).astype(BF16), ev_b_i[j], ev_lam[j])
            qh, kt, vh, qih, kit = _prep_even(seq(q), seq(k), seq(v), seq(qi), seq(kw), cos, sin,
                                              tile_gain(ev_q_norm[j]), tile_gain(ev_k_norm[j]), head_ones)
            yb = _dsa(qih, seq(kw), kit, qh, kt, vh)
            w_out = ev_w_out[j]
        else:
            wi = od_w_in[j]
            w_in = jnp.concatenate([wi[:, 0:3 * w], wi[:, 3 * w + N_HEADS:],
                                    _pad_cols(wi[:, 3 * w:3 * w + N_HEADS], LANES)], axis=1).astype(BF16)
            q, k, v, u, fl = _norm_matmul(x2, norm_mix[l], w_in, (w, w, w, 2 * w, LANES))
            qh, kt, vh = _prep_odd(seq(q), seq(k), seq(v), seq(fl),
                                   _pad_cols(od_b_f[j].reshape(1, N_HEADS), LANES),
                                   tile_gain(od_q_norm[j]), tile_gain(od_k_norm[j]), head_ones)
            ya = _fox(qh, kt, vh, tq=min(512, s))
            yb = _convmod(seq(u), od_conv_w[j], od_conv_b[j], od_ln_g[j], od_ln_b[j])
            w_out = od_w_out[j]
        x2 = _out_ffn(x2, ya.reshape(b * s, w), yb.reshape(b * s, w), w_out.astype(BF16),
                      norm_ffn[l], ffn_w_gu[l].astype(BF16), ffn_w_down[l].astype(BF16))
    return x2.reshape(b, s, d)
```

```python
import functools

import jax
import jax.numpy as jnp
from jax import lax
from jax.experimental import pallas as pl
from jax.experimental.pallas import tpu as pltpu

F32 = jnp.float32
BF16 = jnp.bfloat16

HEAD_DIM = 64
N_HEADS = 8
N_PAIRS = N_HEADS // 2
LANES = 128
CHUNK = 64
CHUNK_SHIFT = 6
TOPK_MAX = 256
ROPE_THETA = 10000.0
NORM_EPS = 1e-6
LRU_C = 8.0
LRU_CONV = 4
CONV_KERNEL = 31
MASKED = -2.0 ** 100
UNDERFLOW_BITS = 160.0
LOG2E = 1.4426950408889634
QK_SCALE_LOG2 = HEAD_DIM ** -0.5 * LOG2E
VMEM_LIMIT = 56 * 1024 * 1024

assert CHUNK == 1 << CHUNK_SHIFT


def _params(*sem):
    return pltpu.CompilerParams(dimension_semantics=sem, vmem_limit_bytes=VMEM_LIMIT)


def _resident(shape, index_map):
    return pl.BlockSpec(shape, index_map, pipeline_mode=pl.Buffered(1))


def _norm_matmul_kernel(x_ref, g_ref, w_ref, *o_refs, widths):
    x = x_ref[...]
    ms = jnp.mean(x * x, axis=-1, keepdims=True)
    h = (x * lax.rsqrt(ms + NORM_EPS) * g_ref[...]).astype(BF16)
    off = 0
    for o_ref, wd in zip(o_refs, widths):
        o_ref[...] = jnp.dot(h, w_ref[:, off:off + wd], preferred_element_type=F32)
        off += wd


def _norm_matmul(x2, g, w, widths, tm=512):
    n, d = x2.shape
    assert n % tm == 0 and sum(widths) == w.shape[1]
    return pl.pallas_call(
        functools.partial(_norm_matmul_kernel, widths=widths),
        grid=(n // tm,),
        in_specs=[pl.BlockSpec((tm, d), lambda i: (i, 0)),
                  _resident((1, d), lambda i: (0, 0)),
                  _resident(w.shape, lambda i: (0, 0))],
        out_specs=[pl.BlockSpec((tm, wd), lambda i: (i, 0)) for wd in widths],
        out_shape=[jax.ShapeDtypeStruct((n, wd), F32) for wd in widths],
        compiler_params=_params("parallel"),
        name="norm_matmul",
    )(x2, g.reshape(1, d), w)


def _shift_rows(x, d, fill):
    row = lax.broadcasted_iota(jnp.int32, x.shape, 0)
    return jnp.where(row >= d, pltpu.roll(x, d, 0), fill)


def _linear_scan_rows(a, u):
    d = 1
    while d < a.shape[0]:
        u = a * _shift_rows(u, d, 0.0) + u
        a = a * _shift_rows(a, d, 1.0)
        d *= 2
    return a, u


def _cumsum_rows(x):
    d = 1
    while d < x.shape[0]:
        x = x + _shift_rows(x, d, 0.0)
        d *= 2
    return x


def _softplus(x):
    return jnp.maximum(x, 0.0) + jnp.log1p(jnp.exp(-jnp.abs(x)))


def _lru_kernel(xg_ref, cw_ref, cb_ref, wr_ref, br_ref, wi_ref, bi_ref, lam_ref, o_ref,
                xbuf_ref, h_ref, *, ts, width):
    @pl.when(pl.program_id(1) == 0)
    def _():
        xbuf_ref[0:8, :] = jnp.zeros((8, width), F32)
        h_ref[...] = jnp.zeros_like(h_ref)

    xa = xg_ref[0, :, 0:width]
    ga = xg_ref[0, :, width:2 * width]
    xbuf_ref[8:8 + ts, :] = xa
    xc = cb_ref[...] + cw_ref[0:1, :] * xbuf_ref[pl.ds(8 - (LRU_CONV - 1), ts), :]
    for j in range(1, LRU_CONV):
        xc = xc + cw_ref[j:j + 1, :] * xbuf_ref[pl.ds(8 - (LRU_CONV - 1) + j, ts), :]
    xbuf_ref[0:8, :] = xa[ts - 8:ts, :]

    xb = xc.astype(BF16)
    r = jax.nn.sigmoid(jnp.dot(xb, wr_ref[...], preferred_element_type=F32) + br_ref[...])
    gate = jax.nn.sigmoid(jnp.dot(xb, wi_ref[...], preferred_element_type=F32) + bi_ref[...])
    log_a = -LRU_C * r * _softplus(-lam_ref[...])
    a = jnp.exp(log_a)
    th = jnp.tanh(log_a)
    u = jnp.sqrt(-2.0 * th / (1.0 - th)) * (gate * xc)
    a_cum, h = _linear_scan_rows(a, u)
    h = h + a_cum * h_ref[...]
    h_ref[...] = h[ts - 1:ts, :]
    o_ref[0] = (h * jax.nn.gelu(ga, approximate=True)).astype(o_ref.dtype)


def _lru(xg, conv_w, conv_b, w_r, b_r, w_i, b_i, lam, ts=256):
    b, s, w2 = xg.shape
    width = w2 // 2
    assert s % ts == 0
    row = lambda v: v.reshape(1, width)
    const = lambda shape: _resident(shape, lambda bi, si: (0, 0))
    return pl.pallas_call(
        functools.partial(_lru_kernel, ts=ts, width=width),
        grid=(b, s // ts),
        in_specs=[pl.BlockSpec((1, ts, w2), lambda bi, si: (bi, si, 0)),
                  const((LRU_CONV, width)), const((1, width)),
                  const((width, width)), const((1, width)),
                  const((width, width)), const((1, width)), const((1, width))],
        out_specs=pl.BlockSpec((1, ts, width), lambda bi, si: (bi, si, 0)),
        out_shape=jax.ShapeDtypeStruct((b, s, width), BF16),
        scratch_shapes=[pltpu.VMEM((ts + 8, width), F32), pltpu.VMEM((1, width), F32)],
        compiler_params=_params("parallel", "arbitrary"),
        name="rg_lru",
    )(xg, conv_w, row(conv_b), w_r, row(b_r), w_i, row(b_i), row(lam))


def _head_mean_square(x, bd_ref):
    x2 = x * x
    hi = x2.astype(BF16)
    r1 = x2 - hi.astype(F32)
    mid = r1.astype(BF16)
    lo = (r1 - mid.astype(F32)).astype(BF16)
    bd = bd_ref[...]
    tot = (jnp.dot(hi, bd, preferred_element_type=F32) + jnp.dot(mid, bd, preferred_element_type=F32)
           + jnp.dot(lo, bd, preferred_element_type=F32))
    return tot * (1.0 / HEAD_DIM)


def _head_rms_norm(x, g, bd_ref):
    return x * lax.rsqrt(_head_mean_square(x, bd_ref) + NORM_EPS) * g


def _rope(x, cos, sin_signed):
    n = x.shape[1]
    lane = lax.broadcasted_iota(jnp.int32, x.shape, 1)
    first_half = (lane & (HEAD_DIM - 1)) < HEAD_DIM // 2
    partner = jnp.where(first_half, pltpu.roll(x, n - HEAD_DIM // 2, 1), pltpu.roll(x, HEAD_DIM // 2, 1))
    return x * cos + partner * sin_signed


def _prep_even_kernel(q_ref, k_ref, v_ref, qi_ref, kw_ref, cos_ref, sin_ref, gq_ref, gk_ref, bd_ref,
                      qo_ref, kto_ref, vo_ref, qio_ref, kito_ref):
    cos = cos_ref[...]
    sin = sin_ref[...]
    q = _rope(_head_rms_norm(q_ref[0], gq_ref[...], bd_ref), cos, sin)
    qo_ref[0] = (q * QK_SCALE_LOG2).astype(BF16)
    k = _rope(_head_rms_norm(k_ref[0], gk_ref[...], bd_ref), cos, sin)
    kto_ref[0] = k.T.astype(BF16)
    vo_ref[0] = v_ref[0].astype(BF16)
    qio_ref[0] = _rope(qi_ref[0], cos, sin).astype(BF16)
    ki = _rope(kw_ref[0], cos[:, 0:LANES], sin[:, 0:LANES])
    lane = lax.broadcasted_iota(jnp.int32, ki.shape, 1)
    ki2 = jnp.where(lane < HEAD_DIM, ki, pltpu.roll(ki, HEAD_DIM, 1))
    kito_ref[0] = ki2.T.astype(BF16)


def _prep_even(q, k, v, qi, kw, cos, sin, gq, gk, bd, ts=512):
    b, s, w = q.shape
    assert s % ts == 0
    tile = pl.BlockSpec((1, ts, w), lambda bi, si: (bi, si, 0))
    tab = pl.BlockSpec((ts, w), lambda bi, si: (si, 0))
    const = lambda shape: _resident(shape, lambda bi, si: (0, 0))
    return pl.pallas_call(
        _prep_even_kernel,
        grid=(b, s // ts),
        in_specs=[tile, tile, tile, tile, pl.BlockSpec((1, ts, LANES), lambda bi, si: (bi, si, 0)),
                  tab, tab, const((1, w)), const((1, w)), const((w, w))],
        out_specs=[tile, pl.BlockSpec((1, w, ts), lambda bi, si: (bi, 0, si)), tile, tile,
                   pl.BlockSpec((1, LANES, ts), lambda bi, si: (bi, 0, si))],
        out_shape=[jax.ShapeDtypeStruct((b, s, w), BF16), jax.ShapeDtypeStruct((b, w, s), BF16),
                   jax.ShapeDtypeStruct((b, s, w), BF16), jax.ShapeDtypeStruct((b, s, w), BF16),
                   jax.ShapeDtypeStruct((b, LANES, s), BF16)],
        compiler_params=_params("parallel", "parallel"),
        name="prep_even",
    )(q, k, v, qi, kw, cos, sin, gq, gk, bd)


def _prep_odd_kernel(q_ref, k_ref, v_ref, fl_ref, bf_ref, gq_ref, gk_ref, bd_ref,
                     qo_ref, kto_ref, vo_ref, carry_ref):
    @pl.when(pl.program_id(1) == 0)
    def _():
        carry_ref[...] = jnp.zeros_like(carry_ref)

    q = _head_rms_norm(q_ref[0], gq_ref[...], bd_ref) * QK_SCALE_LOG2
    k = _head_rms_norm(k_ref[0], gk_ref[...], bd_ref)
    v = v_ref[0]
    log_f = -_softplus(-(fl_ref[0] + bf_ref[...]))
    c = _cumsum_rows(log_f) + carry_ref[...]
    carry_ref[...] = c[c.shape[0] - 1:, :]
    c2 = c * LOG2E
    lane = lax.broadcasted_iota(jnp.int32, (c.shape[0], LANES), 1)
    for h in range(N_HEADS):
        p, e = divmod(h, 2)
        pick = lambda t: t[:, p * LANES:(p + 1) * LANES] if e == 0 else pltpu.roll(t[:, p * LANES:(p + 1) * LANES], HEAD_DIM, 1)
        cb = jnp.broadcast_to(c2[:, h:h + 1], (c.shape[0], LANES))
        hi = cb.astype(BF16).astype(F32)
        mid = (cb - hi).astype(BF16).astype(F32)
        lo = cb - hi - mid
        q_bias = jnp.where(lane == HEAD_DIM, hi, jnp.where(lane == HEAD_DIM + 1, mid, jnp.where(
            lane == HEAD_DIM + 2, lo, jnp.where(lane < HEAD_DIM + 6, 1.0, 0.0))))
        k_bias = jnp.where(lane < HEAD_DIM + 3, 1.0, jnp.where(lane == HEAD_DIM + 3, -hi, jnp.where(
            lane == HEAD_DIM + 4, -mid, jnp.where(lane == HEAD_DIM + 5, -lo, 0.0))))
        qo_ref[0, h] = jnp.where(lane < HEAD_DIM, pick(q), q_bias).astype(BF16)
        kto_ref[0, h] = jnp.where(lane < HEAD_DIM, pick(k), k_bias).T.astype(BF16)
        vo_ref[0, h] = jnp.where(lane < HEAD_DIM, pick(v), 1.0).astype(BF16)


def _prep_odd(q, k, v, fl, bf, gq, gk, bd, ts=512):
    b, s, w = q.shape
    assert s % ts == 0
    tile = pl.BlockSpec((1, ts, w), lambda bi, si: (bi, si, 0))
    narrow = pl.BlockSpec((1, ts, LANES), lambda bi, si: (bi, si, 0))
    heads = pl.BlockSpec((1, N_HEADS, ts, LANES), lambda bi, si: (bi, 0, si, 0))
    const = lambda shape: _resident(shape, lambda bi, si: (0, 0))
    return pl.pallas_call(
        _prep_odd_kernel,
        grid=(b, s // ts),
        in_specs=[tile, tile, tile, narrow, const((1, LANES)), const((1, w)), const((1, w)), const((w, w))],
        out_specs=[heads, pl.BlockSpec((1, N_HEADS, LANES, ts), lambda bi, si: (bi, 0, 0, si)), heads],
        out_shape=[jax.ShapeDtypeStruct((b, N_HEADS, s, LANES), BF16),
                   jax.ShapeDtypeStruct((b, N_HEADS, LANES, s), BF16),
                   jax.ShapeDtypeStruct((b, N_HEADS, s, LANES), BF16)],
        scratch_shapes=[pltpu.VMEM((1, LANES), F32)],
        compiler_params=_params("parallel", "arbitrary"),
        name="prep_odd",
    )(q, k, v, fl, bf, gq, gk, bd)


def _head_halves(x_pair):
    lane = lax.broadcasted_iota(jnp.int32, x_pair.shape, 1)
    zero = jnp.zeros_like(x_pair)
    return jnp.where(lane < HEAD_DIM, x_pair, zero), jnp.where(lane >= HEAD_DIM, x_pair, zero)


def _fold_lanes(x, op):
    out = x[:, 0:LANES]
    for c in range(1, x.shape[1] // LANES):
        out = op(out, x[:, c * LANES:(c + 1) * LANES])
    return out


def _online_softmax_block(scores, values, m_ref, acc_ref):
    probs, alphas = [], []
    for h, s in enumerate(scores):
        m_old = m_ref[h]
        m_new = jnp.maximum(m_old, jnp.max(_fold_lanes(s, jnp.maximum), axis=1, keepdims=True).astype(F32))
        p = jnp.exp2(s - jnp.tile(m_new.astype(s.dtype), (1, s.shape[1] // LANES)))
        m_ref[h] = m_new
        probs.append(p.astype(BF16))
        alphas.append(jnp.exp2(m_old - m_new))
    for h, (p, alpha) in enumerate(zip(probs, alphas)):
        acc_ref[h] = alpha * acc_ref[h] + jnp.dot(p, values[h], preferred_element_type=F32)


def _to_ordinal(x):
    bits = lax.bitcast_convert_type(x, jnp.int32)
    return jnp.where(bits < 0, bits ^ jnp.int32(0x7FFFFFFF), bits)


def _from_ordinal(o):
    return lax.bitcast_convert_type(jnp.where(o < 0, o ^ jnp.int32(0x7FFFFFFF), o), F32)


def _ordinal_midpoint(lo, hi):
    a, b = _to_ordinal(lo), _to_ordinal(hi)
    return _from_ordinal((a >> 1) + (b >> 1) + (a & b & 1))


def _next_float_up(x):
    return _from_ordinal(_to_ordinal(x) + 1)


def _normalise(acc):
    return acc / pltpu.roll(acc, HEAD_DIM, 1)


def _dsa_index_kernel(qi_ref, kw_ref, kit_ref, mask_ref, sc_ref, *, tq, tk, topk, max_iters):
    t0 = pl.program_id(1) * tq
    nb = (t0 + tq + tk - 1) // tk
    row_chunk = (t0 + lax.broadcasted_iota(jnp.int32, (tq, 1), 0)) >> CHUNK_SHIFT
    kf = float(topk)

    def cols(j):
        return pl.ds(pl.multiple_of(j * tk, tk), tk)

    qi_heads = []
    for p in range(N_PAIRS):
        qi_heads.extend(_head_halves(qi_ref[0, :, p * LANES:(p + 1) * LANES]))
    kw = kw_ref[0]

    def score_block(j, carry):
        kit = kit_ref[0, :, cols(j)]
        score = jnp.zeros((tq, tk), F32)
        for h in range(N_HEADS):
            rel = jnp.maximum(jnp.dot(qi_heads[h], kit, preferred_element_type=F32), 0.0)
            score = score + kw[:, HEAD_DIM + h:HEAD_DIM + h + 1] * rel
        key_chunk = (j * tk + lax.broadcasted_iota(jnp.int32, (1, tk), 1)) >> CHUNK_SHIFT
        sc_ref[:, cols(j)] = jnp.where(key_chunk <= row_chunk, score, -jnp.inf)
        return carry

    lax.fori_loop(0, nb, score_block, 0)

    reps = tk // LANES

    def to_col(row):
        return jnp.broadcast_to(row, (LANES, tq)).T

    def to_row(wide, reduce):
        return reduce(wide.T, axis=0, keepdims=True)

    def count(row, strict=False):
        col = to_col(row)
        parts = []
        for r in range(0, tq, LANES):
            bound = jnp.tile(col[r:r + LANES], (1, reps))

            def body(j, cnt):
                blk = sc_ref[r:r + LANES, cols(j)]
                hit = blk > bound if strict else blk >= bound
                return cnt + _fold_lanes(jnp.where(hit, 1.0, 0.0), jnp.add)

            parts.append(lax.fori_loop(0, nb, body, jnp.zeros((LANES, LANES), F32)))
        return to_row(jnp.concatenate(parts, axis=0), jnp.sum)

    def range_body(j, carry):
        lo, hi = carry
        blk = sc_ref[:, cols(j)]
        lo = jnp.minimum(lo, _fold_lanes(jnp.where(blk == -jnp.inf, jnp.inf, blk), jnp.minimum))
        return lo, jnp.maximum(hi, _fold_lanes(blk, jnp.maximum))

    lo_w, hi_w = lax.fori_loop(0, nb, range_body,
                               (jnp.full((tq, LANES), jnp.inf, F32), jnp.full((tq, LANES), -jnp.inf, F32)))
    lo0 = to_row(lo_w, jnp.min)
    hi0 = _next_float_up(to_row(hi_w, jnp.max))
    n_visible = (((t0 + lax.broadcasted_iota(jnp.int32, (1, tq), 1)) >> CHUNK_SHIFT) + 1) * CHUNK
    short_row = n_visible <= topk
    log_k = jnp.log(kf + 0.5)

    def open_rows(lo, hi, c_lo):
        finished = jnp.logical_or(jnp.logical_or(c_lo == kf, short_row),
                                  _to_ordinal(_ordinal_midpoint(lo, hi)) == _to_ordinal(lo))
        return jnp.max(jnp.where(finished, 0.0, 1.0))

    def search_cond(state):
        return jnp.logical_and(state[0] < max_iters, state[-1] > 0.0)

    def search_body(state):
        it, lo, hi, c_lo, f_lo, f_hi, side, _ = state
        guess = lo + (hi - lo) * jnp.clip(f_lo / (f_lo - f_hi), 1.0 / 64, 63.0 / 64)
        inside = jnp.logical_and(guess > lo, guess < hi)
        half = _ordinal_midpoint(lo, hi)
        mid = jnp.where(it % 3 == 2, half, jnp.where(inside, guess, half))
        c_mid = count(mid)
        enough = c_mid >= kf
        f_mid = jnp.log(jnp.maximum(c_mid, 0.5)) - log_k
        f_hi = jnp.where(enough, jnp.where(side > 0.0, f_hi * 0.5, f_hi), f_mid)
        f_lo = jnp.where(enough, f_mid, jnp.where(side < 0.0, f_lo * 0.5, f_lo))
        lo, c_lo = jnp.where(enough, mid, lo), jnp.where(enough, c_mid, c_lo)
        hi = jnp.where(enough, hi, mid)
        return it + 1, lo, hi, c_lo, f_lo, f_hi, jnp.where(enough, 1.0, -1.0), open_rows(lo, hi, c_lo)

    c_lo0 = n_visible.astype(F32)
    state0 = (jnp.int32(0), lo0, hi0, c_lo0, jnp.log(c_lo0) - log_k, jnp.log(0.5) - log_k + jnp.zeros_like(lo0),
              jnp.zeros_like(lo0), open_rows(lo0, hi0, c_lo0))
    _, lo, _, n_ge, _, _, _, _ = lax.while_loop(search_cond, search_body, state0)
    thr_row = jnp.where(short_row, jnp.finfo(F32).min, lo)
    thr = jnp.tile(to_col(thr_row), (1, reps))

    has_extra = jnp.max(jnp.where(jnp.logical_and(n_ge > kf, jnp.logical_not(short_row)), 1.0, 0.0))

    @pl.when(has_extra > 0.0)
    def _():
        keep = to_col(kf - count(thr_row, strict=True))[:, 0:1]
        upper = (lax.broadcasted_iota(jnp.int32, (tk, tk), 0)
                 <= lax.broadcasted_iota(jnp.int32, (tk, tk), 1))
        prefix = jnp.where(upper, 1.0, 0.0).astype(BF16)

        def body(j, seen):
            blk = sc_ref[:, cols(j)]
            tie = jnp.where(blk == thr, 1.0, 0.0)
            rank = seen + jnp.dot(tie.astype(BF16), prefix, preferred_element_type=F32)
            drop = jnp.where(rank > keep, tie, 0.0)
            sc_ref[:, cols(j)] = jnp.where(drop > 0.0, -jnp.inf, blk)
            return seen + jnp.sum(tie, axis=1, keepdims=True)

        lax.fori_loop(0, nb, body, jnp.zeros((tq, 1), F32))

    def mask_block(j, carry):
        mask_ref[0, :, cols(j)] = jnp.where(sc_ref[:, cols(j)] >= thr, 1, 0).astype(mask_ref.dtype)
        return carry

    def clear_block(j, carry):
        mask_ref[0, :, cols(j)] = jnp.zeros((tq, tk), mask_ref.dtype)
        return carry

    lax.fori_loop(0, nb, mask_block, 0)
    lax.fori_loop(nb, mask_ref.shape[2] // tk, clear_block, 0)


def _dsa_index(qi, kw, kit, tq, tk):
    b, s, w = qi.shape
    assert s % tk == 0 and s % tq == 0 and tq % CHUNK == 0
    topk = min(TOPK_MAX, s // 4)
    return pl.pallas_call(
        functools.partial(_dsa_index_kernel, tq=tq, tk=tk, topk=topk, max_iters=640),
        grid=(b, s // tq),
        in_specs=[pl.BlockSpec((1, tq, w), lambda bi, i: (bi, i, 0)),
                  pl.BlockSpec((1, tq, LANES), lambda bi, i: (bi, i, 0)),
                  _resident((1, LANES, s), lambda bi, i: (bi, 0, 0))],
        out_specs=pl.BlockSpec((1, tq, s), lambda bi, i: (bi, i, 0)),
        out_shape=jax.ShapeDtypeStruct((b, s, s), jnp.int8),
        scratch_shapes=[pltpu.VMEM((tq, s), F32)],
        compiler_params=_params("parallel", "arbitrary"),
        name="dsa_index",
    )(qi, kw, kit)


def _dsa_attend_kernel(q_ref, kt_ref, v_ref, mask_ref, o_ref, m_ref, acc_ref, *, tq, tk):
    nb = (pl.program_id(1) * tq + tq + tk - 1) // tk
    m_ref[...] = jnp.full_like(m_ref, MASKED)
    acc_ref[...] = jnp.zeros_like(acc_ref)
    q_heads = []
    for p in range(N_PAIRS):
        q_heads.extend(_head_halves(q_ref[0, :, p * LANES:(p + 1) * LANES]))
    first_half = lax.broadcasted_iota(jnp.int32, (tk, LANES), 1) < HEAD_DIM
    one = jnp.ones((), BF16)

    def attend_block(j, carry):
        cols = pl.ds(pl.multiple_of(j * tk, tk), tk)
        bias = jnp.where(mask_ref[0, :, cols].astype(jnp.int32) > 0, 0.0, MASKED).astype(BF16)
        scores, values = [], []
        for h in range(N_HEADS):
            p, e = divmod(h, 2)
            kt = kt_ref[0, p * LANES:(p + 1) * LANES, cols]
            scores.append(jnp.dot(q_heads[h], kt, preferred_element_type=F32).astype(BF16) + bias)
            v_pair = v_ref[0, cols, p * LANES:(p + 1) * LANES]
            values.append(jnp.where(first_half, v_pair, one) if e == 0 else jnp.where(first_half, one, v_pair))
        _online_softmax_block(scores, values, m_ref, acc_ref)
        return carry

    lax.fori_loop(0, nb, attend_block, 0)
    lane = lax.broadcasted_iota(jnp.int32, (tq, LANES), 1)
    for p in range(N_PAIRS):
        o_ref[0, :, p * LANES:(p + 1) * LANES] = jnp.where(
            lane < HEAD_DIM, _normalise(acc_ref[2 * p]), _normalise(acc_ref[2 * p + 1])).astype(o_ref.dtype)


def _dsa_attend(q, kt, v, mask, tq, tk):
    b, s, w = q.shape
    assert s % tk == 0 and s % tq == 0
    qtile = pl.BlockSpec((1, tq, w), lambda bi, i: (bi, i, 0))
    return pl.pallas_call(
        functools.partial(_dsa_attend_kernel, tq=tq, tk=tk),
        grid=(b, s // tq),
        in_specs=[qtile,
                  _resident((1, w, s), lambda bi, i: (bi, 0, 0)),
                  _resident((1, s, w), lambda bi, i: (bi, 0, 0)),
                  pl.BlockSpec((1, tq, s), lambda bi, i: (bi, i, 0))],
        out_specs=qtile,
        out_shape=jax.ShapeDtypeStruct((b, s, w), BF16),
        scratch_shapes=[pltpu.VMEM((N_HEADS, tq, LANES), F32), pltpu.VMEM((N_HEADS, tq, LANES), F32)],
        compiler_params=_params("parallel", "arbitrary"),
        name="dsa_attend",
    )(q, kt, v, mask)


def _dsa(qi, kw, kit, q, kt, v, tq=256, tk=512):
    s = q.shape[1]
    mask = _dsa_index(qi, kw, kit, min(tq, s), min(tk, s))
    return _dsa_attend(q, kt, v, mask, min(tq, s), min(tk, s))


def _fox_kernel(bound_ref, q_ref, kt_ref, v_ref, o_ref, m_ref, acc_ref, ckey_ref, *, tq, heads):
    i = pl.program_id(2)
    m_ref[...] = jnp.full_like(m_ref, MASKED)
    acc_ref[...] = jnp.zeros_like(acc_ref)
    causal = (lax.broadcasted_iota(jnp.int32, (tq, tq), 1) <= lax.broadcasted_iota(jnp.int32, (tq, tq), 0))

    @pl.when(i == 0)
    def _():
        for e in range(heads):
            tail = kt_ref[0, e, HEAD_DIM:HEAD_DIM + 16, :].astype(F32)
            row = lax.broadcasted_iota(jnp.int32, tail.shape, 0)
            ckey_ref[e] = -jnp.sum(jnp.where(jnp.logical_and(row >= 3, row < 6), tail, 0.0), axis=0, keepdims=True)

    pos = lax.broadcasted_iota(jnp.int32, (1, ckey_ref.shape[2]), 1)
    block_end = jnp.logical_and((pos & (tq - 1)) == tq - 1, pos < i * tq)
    lane = lax.broadcasted_iota(jnp.int32, (tq, LANES), 1)
    first = None
    for e in range(heads):
        c_first = ckey_ref[e, :, pl.ds(pl.multiple_of(i * tq, tq), LANES)][:, 0:1]
        dead = jnp.logical_and(block_end, ckey_ref[e] - c_first > 2.0 * bound_ref[0] + UNDERFLOW_BITS)
        n_dead = jnp.sum(jnp.where(dead, 1, 0))
        first = n_dead if first is None else jnp.minimum(first, n_dead)

    def block(j, diagonal):
        cols = pl.ds(pl.multiple_of(j * tq, tq), tq)
        scores = []
        for e in range(heads):
            s = jnp.dot(q_ref[0, e], kt_ref[0, e, :, cols], preferred_element_type=F32)
            scores.append(jnp.where(causal, s, MASKED) if diagonal else s)
        _online_softmax_block(scores, [v_ref[0, e, cols, :] for e in range(heads)], m_ref, acc_ref)

    def body(j, carry):
        block(j, False)
        return carry

    lax.fori_loop(first, i, body, 0)
    block(i, True)
    for p in range(heads // 2):
        a0, a1 = acc_ref[2 * p], acc_ref[2 * p + 1]
        o_ref[0, :, p * LANES:(p + 1) * LANES] = jnp.where(
            lane < HEAD_DIM, a0 / pltpu.roll(a0, HEAD_DIM, 1), pltpu.roll(a1, HEAD_DIM, 1) / a1).astype(o_ref.dtype)


def _fox(qk_bound, q, kt, v, tq=512, heads=4):
    b, nh, s, _ = q.shape
    assert s % tq == 0 and tq & (tq - 1) == 0 and nh == N_HEADS and nh % heads == 0 and heads % 2 == 0
    out_w = heads * HEAD_DIM
    return pl.pallas_call(
        functools.partial(_fox_kernel, tq=tq, heads=heads),
        grid=(b, nh // heads, s // tq),
        in_specs=[pl.BlockSpec(memory_space=pltpu.SMEM),
                  pl.BlockSpec((1, heads, tq, LANES), lambda bi, g, i: (bi, g, i, 0)),
                  _resident((1, heads, LANES, s), lambda bi, g, i: (bi, g, 0, 0)),
                  _resident((1, heads, s, LANES), lambda bi, g, i: (bi, g, 0, 0))],
        out_specs=pl.BlockSpec((1, tq, out_w), lambda bi, g, i: (bi, i, g)),
        out_shape=jax.ShapeDtypeStruct((b, s, N_HEADS * HEAD_DIM), BF16),
        scratch_shapes=[pltpu.VMEM((heads, tq, LANES), F32), pltpu.VMEM((heads, tq, LANES), F32),
                        pltpu.VMEM((heads, 1, s), F32)],
        compiler_params=_params("parallel", "parallel", "arbitrary"),
        name="fox",
    )(qk_bound, q, kt, v)


def _convmod_kernel(u_ref, cw_ref, cb_ref, g_ref, b_ref, o_ref, xbuf_ref, *, ts, width, halo):
    @pl.when(pl.program_id(1) == 0)
    def _():
        xbuf_ref[0:halo, :] = jnp.zeros((halo, width), F32)

    x = u_ref[0, :, 0:width] * jax.nn.sigmoid(u_ref[0, :, width:2 * width])
    xbuf_ref[halo:halo + ts, :] = x
    base = halo - (CONV_KERNEL - 1)
    y = cb_ref[...] + cw_ref[0:1, :] * xbuf_ref[pl.ds(base, ts), :]
    for j in range(1, CONV_KERNEL):
        y = y + cw_ref[j:j + 1, :] * xbuf_ref[pl.ds(base + j, ts), :]
    xbuf_ref[0:halo, :] = x[ts - halo:ts, :]
    mu = jnp.mean(y, axis=-1, keepdims=True)
    var = jnp.mean(jnp.square(y - mu), axis=-1, keepdims=True)
    z = (y - mu) * lax.rsqrt(var + NORM_EPS) * g_ref[...] + b_ref[...]
    o_ref[0] = (z * jax.nn.sigmoid(z)).astype(o_ref.dtype)


def _convmod(u, conv_w, conv_b, ln_g, ln_b, ts=512, halo=32):
    b, s, w2 = u.shape
    width = w2 // 2
    assert s % ts == 0 and halo >= CONV_KERNEL - 1
    row = lambda v: v.reshape(1, width)
    const = lambda shape: _resident(shape, lambda bi, si: (0, 0))
    return pl.pallas_call(
        functools.partial(_convmod_kernel, ts=ts, width=width, halo=halo),
        grid=(b, s // ts),
        in_specs=[pl.BlockSpec((1, ts, w2), lambda bi, si: (bi, si, 0)),
                  const((CONV_KERNEL, width)), const((1, width)), const((1, width)), const((1, width))],
        out_specs=pl.BlockSpec((1, ts, width), lambda bi, si: (bi, si, 0)),
        out_shape=jax.ShapeDtypeStruct((b, s, width), BF16),
        scratch_shapes=[pltpu.VMEM((ts + halo, width), F32)],
        compiler_params=_params("parallel", "arbitrary"),
        name="conv_module",
    )(u, conv_w, row(conv_b), row(ln_g), row(ln_b))


def _out_ffn_kernel(x_ref, ya_ref, yb_ref, wo_ref, g_ref, wgu_ref, wd_ref, o_ref, *, hidden, th):
    half = ya_ref.shape[1]
    x = x_ref[...] + (jnp.dot(ya_ref[...], wo_ref[0:half, :], preferred_element_type=F32)
                      + jnp.dot(yb_ref[...], wo_ref[half:2 * half, :], preferred_element_type=F32))
    ms = jnp.mean(x * x, axis=-1, keepdims=True)
    h = (x * lax.rsqrt(ms + NORM_EPS) * g_ref[...]).astype(BF16)
    y = x
    for c in range(0, hidden, th):
        gate = jnp.dot(h, wgu_ref[:, c:c + th], preferred_element_type=F32)
        up = jnp.dot(h, wgu_ref[:, hidden + c:hidden + c + th], preferred_element_type=F32)
        act = (gate * jax.nn.sigmoid(gate) * up).astype(BF16)
        y = y + jnp.dot(act, wd_ref[c:c + th, :], preferred_element_type=F32)
    o_ref[...] = y


def _out_ffn(x2, ya, yb, w_out, g, w_gu, w_down, tm=512, th=256):
    n, d = x2.shape
    half = ya.shape[1]
    hidden = w_down.shape[0]
    assert n % tm == 0 and hidden % th == 0
    const = lambda shape: _resident(shape, lambda i: (0, 0))
    return pl.pallas_call(
        functools.partial(_out_ffn_kernel, hidden=hidden, th=th),
        grid=(n // tm,),
        in_specs=[pl.BlockSpec((tm, d), lambda i: (i, 0)),
                  pl.BlockSpec((tm, half), lambda i: (i, 0)), pl.BlockSpec((tm, half), lambda i: (i, 0)),
                  const(w_out.shape), const((1, d)), const(w_gu.shape), const(w_down.shape)],
        out_specs=pl.BlockSpec((tm, d), lambda i: (i, 0)),
        out_shape=jax.ShapeDtypeStruct((n, d), F32),
        compiler_params=_params("parallel"),
        name="out_ffn",
    )(x2, ya, yb, w_out, g.reshape(1, d), w_gu, w_down)


def _rope_tables(s, width):
    inv = ROPE_THETA ** (-jnp.arange(0, HEAD_DIM, 2, dtype=F32) / HEAD_DIM)
    ang = jnp.arange(s, dtype=F32)[:, None] * inv[None, :]
    cos, sin = jnp.cos(ang), jnp.sin(ang)
    reps = width // HEAD_DIM
    return (jnp.tile(jnp.concatenate([cos, cos], axis=-1), (1, reps)),
            jnp.tile(jnp.concatenate([-sin, sin], axis=-1), (1, reps)))


def _block_diag(blocks):
    n, d, _ = blocks.shape
    eye = jnp.eye(n, dtype=blocks.dtype)
    return jnp.einsum('nde,nm->ndme', blocks, eye).reshape(n * d, n * d)


def _pad_cols(w, total):
    return jnp.pad(w, ((0, 0), (0, total - w.shape[1])))


def kernel(x, norm_mix, norm_ffn,
           ev_w_in, ev_conv_w, ev_conv_b, ev_w_r, ev_b_r, ev_w_i, ev_b_i, ev_lam,
           ev_q_norm, ev_k_norm, ev_w_out,
           od_w_in, od_b_f, od_q_norm, od_k_norm, od_conv_w, od_conv_b, od_ln_g, od_ln_b,
           od_w_out, ffn_w_gu, ffn_w_down):
    b, s, d = x.shape
    depth = norm_mix.shape[0]
    w = N_HEADS * HEAD_DIM
    cos, sin = _rope_tables(s, w)
    head_ones = _block_diag(jnp.ones((N_HEADS, HEAD_DIM, HEAD_DIM), BF16))
    tile_gain = lambda g: jnp.tile(g, N_HEADS).reshape(1, w)
    x2 = x.reshape(b * s, d)
    seq = lambda t: t.reshape(b, s, t.shape[-1])
    for l in range(depth):
        j = l // 2
        if l % 2 == 0:
            w_in = _pad_cols(ev_w_in[j], 2 * w + 4 * w + LANES).astype(BF16)
            xg, q, k, v, qi, kw = _norm_matmul(x2, norm_mix[l], w_in, (2 * w, w, w, w, w, LANES))
            ya = _lru(seq(xg), ev_conv_w[j], ev_conv_b[j],
                      _block_diag(ev_w_r[j]).astype(BF16), ev_b_r[j],
                      _block_diag(ev_w_i[j]).astype(BF16), ev_b_i[j], ev_lam[j])
            qh, kt, vh, qih, kit = _prep_even(seq(q), seq(k), seq(v), seq(qi), seq(kw), cos, sin,
                                              tile_gain(ev_q_norm[j]), tile_gain(ev_k_norm[j]), head_ones)
            yb = _dsa(qih, seq(kw), kit, qh, kt, vh)
            w_out = ev_w_out[j]
        else:
            wi = od_w_in[j]
            w_in = jnp.concatenate([wi[:, 0:3 * w], wi[:, 3 * w + N_HEADS:],
                                    _pad_cols(wi[:, 3 * w:3 * w + N_HEADS], LANES)], axis=1).astype(BF16)
            q, k, v, u, fl = _norm_matmul(x2, norm_mix[l], w_in, (w, w, w, 2 * w, LANES))
            qh, kt, vh = _prep_odd(seq(q), seq(k), seq(v), seq(fl),
                                   _pad_cols(od_b_f[j].reshape(1, N_HEADS), LANES),
                                   tile_gain(od_q_norm[j]), tile_gain(od_k_norm[j]), head_ones)
            qk_bound = (HEAD_DIM * QK_SCALE_LOG2 * 1.01 * jnp.max(jnp.abs(od_q_norm[j]))
                        * jnp.max(jnp.abs(od_k_norm[j])) + 1.0).reshape(1).astype(F32)
            ya = _fox(qk_bound, qh, kt, vh, tq=min(512, s))
            yb = _convmod(seq(u), od_conv_w[j], od_conv_b[j], od_ln_g[j], od_ln_b[j])
            w_out = od_w_out[j]
        x2 = _out_ffn(x2, ya.reshape(b * s, w), yb.reshape(b * s, w), w_out.astype(BF16),
                      norm_ffn[l], ffn_w_gu[l].astype(BF16), ffn_w_down[l].astype(BF16))
    return x2.reshape(b, s, d)
```

```python
import functools

import jax
import jax.numpy as jnp
from jax import lax
from jax.experimental import pallas as pl
from jax.experimental.pallas import tpu as pltpu

F32 = jnp.float32
BF16 = jnp.bfloat16

HEAD_DIM = 64
N_HEADS = 8
N_PAIRS = N_HEADS // 2
LANES = 128
CHUNK = 64
CHUNK_SHIFT = 6
TOPK_MAX = 256
ROPE_THETA = 10000.0
NORM_EPS = 1e-6
LRU_C = 8.0
LRU_CONV = 4
CONV_KERNEL = 31
MASKED = -2.0 ** 100
UNDERFLOW_BITS = 160.0
LOG2E = 1.4426950408889634
QK_SCALE_LOG2 = HEAD_DIM ** -0.5 * LOG2E
VMEM_LIMIT = 56 * 1024 * 1024

assert CHUNK == 1 << CHUNK_SHIFT


def _params(*sem):
    return pltpu.CompilerParams(dimension_semantics=sem, vmem_limit_bytes=VMEM_LIMIT)


def _resident(shape, index_map):
    return pl.BlockSpec(shape, index_map, pipeline_mode=pl.Buffered(1))


def _norm_matmul_kernel(x_ref, g_ref, w_ref, *o_refs, widths):
    x = x_ref[...]
    ms = jnp.mean(x * x, axis=-1, keepdims=True)
    h = (x * lax.rsqrt(ms + NORM_EPS) * g_ref[...]).astype(BF16)
    off = 0
    for o_ref, wd in zip(o_refs, widths):
        o_ref[...] = jnp.dot(h, w_ref[:, off:off + wd], preferred_element_type=F32)
        off += wd


def _norm_matmul(x2, g, w, widths, tm=512):
    n, d = x2.shape
    assert n % tm == 0 and sum(widths) == w.shape[1]
    return pl.pallas_call(
        functools.partial(_norm_matmul_kernel, widths=widths),
        grid=(n // tm,),
        in_specs=[pl.BlockSpec((tm, d), lambda i: (i, 0)),
                  _resident((1, d), lambda i: (0, 0)),
                  _resident(w.shape, lambda i: (0, 0))],
        out_specs=[pl.BlockSpec((tm, wd), lambda i: (i, 0)) for wd in widths],
        out_shape=[jax.ShapeDtypeStruct((n, wd), F32) for wd in widths],
        compiler_params=_params("parallel"),
        name="norm_matmul",
    )(x2, g.reshape(1, d), w)


def _shift_rows(x, d, fill):
    row = lax.broadcasted_iota(jnp.int32, x.shape, 0)
    return jnp.where(row >= d, pltpu.roll(x, d, 0), fill)


def _linear_scan_rows(a, u):
    d = 1
    while d < a.shape[0]:
        u = a * _shift_rows(u, d, 0.0) + u
        a = a * _shift_rows(a, d, 1.0)
        d *= 2
    return a, u


def _cumsum_rows(x):
    d = 1
    while d < x.shape[0]:
        x = x + _shift_rows(x, d, 0.0)
        d *= 2
    return x


def _softplus(x):
    return jnp.maximum(x, 0.0) + jnp.log1p(jnp.exp(-jnp.abs(x)))


def _lru_kernel(xg_ref, cw_ref, cb_ref, wr_ref, br_ref, wi_ref, bi_ref, lam_ref, o_ref,
                xbuf_ref, h_ref, *, ts, width):
    @pl.when(pl.program_id(1) == 0)
    def _():
        xbuf_ref[0:8, :] = jnp.zeros((8, width), F32)
        h_ref[...] = jnp.zeros_like(h_ref)

    xa = xg_ref[0, :, 0:width]
    ga = xg_ref[0, :, width:2 * width]
    xbuf_ref[8:8 + ts, :] = xa
    xc = cb_ref[...] + cw_ref[0:1, :] * xbuf_ref[pl.ds(8 - (LRU_CONV - 1), ts), :]
    for j in range(1, LRU_CONV):
        xc = xc + cw_ref[j:j + 1, :] * xbuf_ref[pl.ds(8 - (LRU_CONV - 1) + j, ts), :]
    xbuf_ref[0:8, :] = xa[ts - 8:ts, :]

    xb = xc.astype(BF16)
    r = jax.nn.sigmoid(jnp.dot(xb, wr_ref[...], preferred_element_type=F32) + br_ref[...])
    gate = jax.nn.sigmoid(jnp.dot(xb, wi_ref[...], preferred_element_type=F32) + bi_ref[...])
    log_a = -LRU_C * r * _softplus(-lam_ref[...])
    a = jnp.exp(log_a)
    th = jnp.tanh(log_a)
    u = jnp.sqrt(-2.0 * th / (1.0 - th)) * (gate * xc)
    a_cum, h = _linear_scan_rows(a, u)
    h = h + a_cum * h_ref[...]
    h_ref[...] = h[ts - 1:ts, :]
    o_ref[0] = (h * jax.nn.gelu(ga, approximate=True)).astype(o_ref.dtype)


def _lru(xg, conv_w, conv_b, w_r, b_r, w_i, b_i, lam, ts=256):
    b, s, w2 = xg.shape
    width = w2 // 2
    assert s % ts == 0
    row = lambda v: v.reshape(1, width)
    const = lambda shape: _resident(shape, lambda bi, si: (0, 0))
    return pl.pallas_call(
        functools.partial(_lru_kernel, ts=ts, width=width),
        grid=(b, s // ts),
        in_specs=[pl.BlockSpec((1, ts, w2), lambda bi, si: (bi, si, 0)),
                  const((LRU_CONV, width)), const((1, width)),
                  const((width, width)), const((1, width)),
                  const((width, width)), const((1, width)), const((1, width))],
        out_specs=pl.BlockSpec((1, ts, width), lambda bi, si: (bi, si, 0)),
        out_shape=jax.ShapeDtypeStruct((b, s, width), BF16),
        scratch_shapes=[pltpu.VMEM((ts + 8, width), F32), pltpu.VMEM((1, width), F32)],
        compiler_params=_params("parallel", "arbitrary"),
        name="rg_lru",
    )(xg, conv_w, row(conv_b), w_r, row(b_r), w_i, row(b_i), row(lam))


def _head_mean_square(x, bd_ref):
    x2 = x * x
    hi = x2.astype(BF16)
    r1 = x2 - hi.astype(F32)
    mid = r1.astype(BF16)
    lo = (r1 - mid.astype(F32)).astype(BF16)
    bd = bd_ref[...]
    tot = (jnp.dot(hi, bd, preferred_element_type=F32) + jnp.dot(mid, bd, preferred_element_type=F32)
           + jnp.dot(lo, bd, preferred_element_type=F32))
    return tot * (1.0 / HEAD_DIM)


def _head_rms_norm(x, g, bd_ref):
    return x * lax.rsqrt(_head_mean_square(x, bd_ref) + NORM_EPS) * g


def _rope(x, cos, sin_signed):
    n = x.shape[1]
    lane = lax.broadcasted_iota(jnp.int32, x.shape, 1)
    first_half = (lane & (HEAD_DIM - 1)) < HEAD_DIM // 2
    partner = jnp.where(first_half, pltpu.roll(x, n - HEAD_DIM // 2, 1), pltpu.roll(x, HEAD_DIM // 2, 1))
    return x * cos + partner * sin_signed


def _prep_even_kernel(q_ref, k_ref, v_ref, qi_ref, kw_ref, cos_ref, sin_ref, gq_ref, gk_ref, bd_ref,
                      qo_ref, kto_ref, vo_ref, qio_ref, kito_ref):
    cos = cos_ref[...]
    sin = sin_ref[...]
    q = _rope(_head_rms_norm(q_ref[0], gq_ref[...], bd_ref), cos, sin)
    qo_ref[0] = (q * QK_SCALE_LOG2).astype(BF16)
    k = _rope(_head_rms_norm(k_ref[0], gk_ref[...], bd_ref), cos, sin)
    kto_ref[0] = k.T.astype(BF16)
    vo_ref[0] = v_ref[0].astype(BF16)
    qio_ref[0] = _rope(qi_ref[0], cos, sin).astype(BF16)
    ki = _rope(kw_ref[0], cos[:, 0:LANES], sin[:, 0:LANES])
    lane = lax.broadcasted_iota(jnp.int32, ki.shape, 1)
    ki2 = jnp.where(lane < HEAD_DIM, ki, pltpu.roll(ki, HEAD_DIM, 1))
    kito_ref[0] = ki2.T.astype(BF16)


def _prep_even(q, k, v, qi, kw, cos, sin, gq, gk, bd, ts=512):
    b, s, w = q.shape
    assert s % ts == 0
    tile = pl.BlockSpec((1, ts, w), lambda bi, si: (bi, si, 0))
    tab = pl.BlockSpec((ts, w), lambda bi, si: (si, 0))
    const = lambda shape: _resident(shape, lambda bi, si: (0, 0))
    return pl.pallas_call(
        _prep_even_kernel,
        grid=(b, s // ts),
        in_specs=[tile, tile, tile, tile, pl.BlockSpec((1, ts, LANES), lambda bi, si: (bi, si, 0)),
                  tab, tab, const((1, w)), const((1, w)), const((w, w))],
        out_specs=[tile, pl.BlockSpec((1, w, ts), lambda bi, si: (bi, 0, si)), tile, tile,
                   pl.BlockSpec((1, LANES, ts), lambda bi, si: (bi, 0, si))],
        out_shape=[jax.ShapeDtypeStruct((b, s, w), BF16), jax.ShapeDtypeStruct((b, w, s), BF16),
                   jax.ShapeDtypeStruct((b, s, w), BF16), jax.ShapeDtypeStruct((b, s, w), BF16),
                   jax.ShapeDtypeStruct((b, LANES, s), BF16)],
        compiler_params=_params("parallel", "parallel"),
        name="prep_even",
    )(q, k, v, qi, kw, cos, sin, gq, gk, bd)


def _prep_odd_kernel(q_ref, k_ref, v_ref, fl_ref, bf_ref, gq_ref, gk_ref, bd_ref,
                     qo_ref, kto_ref, vo_ref, carry_ref):
    @pl.when(pl.program_id(1) == 0)
    def _():
        carry_ref[...] = jnp.zeros_like(carry_ref)

    q = _head_rms_norm(q_ref[0], gq_ref[...], bd_ref) * QK_SCALE_LOG2
    k = _head_rms_norm(k_ref[0], gk_ref[...], bd_ref)
    v = v_ref[0]
    log_f = -_softplus(-(fl_ref[0] + bf_ref[...]))
    c = _cumsum_rows(log_f) + carry_ref[...]
    carry_ref[...] = c[c.shape[0] - 1:, :]
    c2 = c * LOG2E
    lane = lax.broadcasted_iota(jnp.int32, (c.shape[0], LANES), 1)
    for h in range(N_HEADS):
        p, e = divmod(h, 2)
        pick = lambda t: t[:, p * LANES:(p + 1) * LANES] if e == 0 else pltpu.roll(t[:, p * LANES:(p + 1) * LANES], HEAD_DIM, 1)
        cb = jnp.broadcast_to(c2[:, h:h + 1], (c.shape[0], LANES))
        hi = cb.astype(BF16).astype(F32)
        mid = (cb - hi).astype(BF16).astype(F32)
        lo = cb - hi - mid
        q_bias = jnp.where(lane == HEAD_DIM, hi, jnp.where(lane == HEAD_DIM + 1, mid, jnp.where(
            lane == HEAD_DIM + 2, lo, jnp.where(lane < HEAD_DIM + 6, 1.0, 0.0))))
        k_bias = jnp.where(lane < HEAD_DIM + 3, 1.0, jnp.where(lane == HEAD_DIM + 3, -hi, jnp.where(
            lane == HEAD_DIM + 4, -mid, jnp.where(lane == HEAD_DIM + 5, -lo, 0.0))))
        qo_ref[0, h] = jnp.where(lane < HEAD_DIM, pick(q), q_bias).astype(BF16)
        kto_ref[0, h] = jnp.where(lane < HEAD_DIM, pick(k), k_bias).T.astype(BF16)
        vo_ref[0, h] = jnp.where(lane < HEAD_DIM, pick(v), 1.0).astype(BF16)


def _prep_odd(q, k, v, fl, bf, gq, gk, bd, ts=512):
    b, s, w = q.shape
    assert s % ts == 0
    tile = pl.BlockSpec((1, ts, w), lambda bi, si: (bi, si, 0))
    narrow = pl.BlockSpec((1, ts, LANES), lambda bi, si: (bi, si, 0))
    heads = pl.BlockSpec((1, N_HEADS, ts, LANES), lambda bi, si: (bi, 0, si, 0))
    const = lambda shape: _resident(shape, lambda bi, si: (0, 0))
    return pl.pallas_call(
        _prep_odd_kernel,
        grid=(b, s // ts),
        in_specs=[tile, tile, tile, narrow, const((1, LANES)), const((1, w)), const((1, w)), const((w, w))],
        out_specs=[heads, pl.BlockSpec((1, N_HEADS, LANES, ts), lambda bi, si: (bi, 0, 0, si)), heads],
        out_shape=[jax.ShapeDtypeStruct((b, N_HEADS, s, LANES), BF16),
                   jax.ShapeDtypeStruct((b, N_HEADS, LANES, s), BF16),
                   jax.ShapeDtypeStruct((b, N_HEADS, s, LANES), BF16)],
        scratch_shapes=[pltpu.VMEM((1, LANES), F32)],
        compiler_params=_params("parallel", "arbitrary"),
        name="prep_odd",
    )(q, k, v, fl, bf, gq, gk, bd)


def _head_halves(x_pair):
    lane = lax.broadcasted_iota(jnp.int32, x_pair.shape, 1)
    zero = jnp.zeros_like(x_pair)
    return jnp.where(lane < HEAD_DIM, x_pair, zero), jnp.where(lane >= HEAD_DIM, x_pair, zero)


def _fold_lanes(x, op):
    out = x[:, 0:LANES]
    for c in range(1, x.shape[1] // LANES):
        out = op(out, x[:, c * LANES:(c + 1) * LANES])
    return out


def _online_softmax_block(scores, values, m_ref, acc_ref):
    probs, alphas = [], []
    for h, s in enumerate(scores):
        m_old = m_ref[h]
        m_new = jnp.maximum(m_old, jnp.max(_fold_lanes(s, jnp.maximum), axis=1, keepdims=True).astype(F32))
        p = jnp.exp2(s - jnp.tile(m_new.astype(s.dtype), (1, s.shape[1] // LANES)))
        m_ref[h] = m_new
        probs.append(p.astype(BF16))
        alphas.append(jnp.exp2(m_old - m_new))
    for h, (p, alpha) in enumerate(zip(probs, alphas)):
        acc_ref[h] = alpha * acc_ref[h] + jnp.dot(p, values[h], preferred_element_type=F32)


def _to_ordinal(x):
    bits = lax.bitcast_convert_type(x, jnp.int32)
    return jnp.where(bits < 0, bits ^ jnp.int32(0x7FFFFFFF), bits)


def _from_ordinal(o):
    return lax.bitcast_convert_type(jnp.where(o < 0, o ^ jnp.int32(0x7FFFFFFF), o), F32)


def _ordinal_midpoint(lo, hi):
    a, b = _to_ordinal(lo), _to_ordinal(hi)
    return _from_ordinal((a >> 1) + (b >> 1) + (a & b & 1))


def _next_float_up(x):
    return _from_ordinal(_to_ordinal(x) + 1)


def _normalise(acc):
    return acc / pltpu.roll(acc, HEAD_DIM, 1)


def _dsa_index_kernel(qi_ref, kw_ref, kit_ref, mask_ref, sc_ref, *, tq, tk, topk, max_iters):
    t0 = pl.program_id(1) * tq
    nb = (t0 + tq + tk - 1) // tk
    row_chunk = (t0 + lax.broadcasted_iota(jnp.int32, (tq, 1), 0)) >> CHUNK_SHIFT
    kf = float(topk)

    def cols(j):
        return pl.ds(pl.multiple_of(j * tk, tk), tk)

    qi_heads = []
    for p in range(N_PAIRS):
        qi_heads.extend(_head_halves(qi_ref[0, :, p * LANES:(p + 1) * LANES]))
    kw = kw_ref[0]

    def score_block(j, carry):
        kit = kit_ref[0, :, cols(j)]
        score = jnp.zeros((tq, tk), F32)
        for h in range(N_HEADS):
            rel = jnp.maximum(jnp.dot(qi_heads[h], kit, preferred_element_type=F32), 0.0)
            score = score + kw[:, HEAD_DIM + h:HEAD_DIM + h + 1] * rel
        key_chunk = (j * tk + lax.broadcasted_iota(jnp.int32, (1, tk), 1)) >> CHUNK_SHIFT
        sc_ref[:, cols(j)] = jnp.where(key_chunk <= row_chunk, score, -jnp.inf)
        return carry

    lax.fori_loop(0, nb, score_block, 0)

    reps = tk // LANES

    def to_col(row):
        return jnp.broadcast_to(row, (LANES, tq)).T

    def to_row(wide, reduce):
        return reduce(wide.T, axis=0, keepdims=True)

    def count(row, strict=False, wanted=None):
        col = to_col(row)
        parts = []
        for g, r in enumerate(range(0, tq, LANES)):
            bound = jnp.tile(col[r:r + LANES], (1, reps))

            def body(j, cnt):
                blk = sc_ref[r:r + LANES, cols(j)]
                hit = blk > bound if strict else blk >= bound
                return cnt + _fold_lanes(jnp.where(hit, 1.0, 0.0), jnp.add)

            blocks = nb if wanted is None else jnp.where(wanted[g] > 0.0, nb, 0)
            parts.append(lax.fori_loop(0, blocks, body, jnp.zeros((LANES, LANES), F32)))
        return to_row(jnp.concatenate(parts, axis=0), jnp.sum)

    def range_body(j, carry):
        lo, hi = carry
        blk = sc_ref[:, cols(j)]
        lo = jnp.minimum(lo, _fold_lanes(jnp.where(blk == -jnp.inf, jnp.inf, blk), jnp.minimum))
        return lo, jnp.maximum(hi, _fold_lanes(blk, jnp.maximum))

    lo_w, hi_w = lax.fori_loop(0, nb, range_body,
                               (jnp.full((tq, LANES), jnp.inf, F32), jnp.full((tq, LANES), -jnp.inf, F32)))
    lo0 = to_row(lo_w, jnp.min)
    hi0 = _next_float_up(to_row(hi_w, jnp.max))
    n_visible = (((t0 + lax.broadcasted_iota(jnp.int32, (1, tq), 1)) >> CHUNK_SHIFT) + 1) * CHUNK
    short_row = n_visible <= topk
    log_k = jnp.log(kf + 0.5)

    def open_groups(lo, hi, c_lo):
        finished = jnp.logical_or(jnp.logical_or(c_lo == kf, short_row),
                                  _to_ordinal(_ordinal_midpoint(lo, hi)) == _to_ordinal(lo))
        still = jnp.where(finished, 0.0, 1.0)
        return tuple(jnp.max(still[:, r:r + LANES]) for r in range(0, tq, LANES))

    def search_cond(state):
        return jnp.logical_and(state[0] < max_iters, functools.reduce(jnp.maximum, state[-1]) > 0.0)

    def search_body(state):
        it, lo, hi, c_lo, f_lo, f_hi, side, wanted = state
        guess = lo + (hi - lo) * jnp.clip(f_lo / (f_lo - f_hi), 1.0 / 64, 63.0 / 64)
        inside = jnp.logical_and(guess > lo, guess < hi)
        half = _ordinal_midpoint(lo, hi)
        mid = jnp.where(it % 3 == 2, half, jnp.where(inside, guess, half))
        c_mid = count(mid, wanted=wanted)
        counted = jnp.concatenate([jnp.zeros((1, LANES), F32) + w for w in wanted], axis=1) > 0.0
        enough = jnp.logical_and(counted, c_mid >= kf)
        short = jnp.logical_and(counted, c_mid < kf)
        f_mid = jnp.log(jnp.maximum(c_mid, 0.5)) - log_k
        f_hi = jnp.where(enough, jnp.where(side > 0.0, f_hi * 0.5, f_hi), jnp.where(short, f_mid, f_hi))
        f_lo = jnp.where(enough, f_mid, jnp.where(jnp.logical_and(short, side < 0.0), f_lo * 0.5, f_lo))
        lo, c_lo = jnp.where(enough, mid, lo), jnp.where(enough, c_mid, c_lo)
        hi = jnp.where(short, mid, hi)
        side = jnp.where(enough, 1.0, jnp.where(short, -1.0, side))
        return it + 1, lo, hi, c_lo, f_lo, f_hi, side, open_groups(lo, hi, c_lo)

    c_lo0 = n_visible.astype(F32)
    state0 = (jnp.int32(0), lo0, hi0, c_lo0, jnp.log(c_lo0) - log_k, jnp.log(0.5) - log_k + jnp.zeros_like(lo0),
              jnp.zeros_like(lo0), open_groups(lo0, hi0, c_lo0))
    _, lo, _, n_ge, _, _, _, _ = lax.while_loop(search_cond, search_body, state0)
    thr_row = jnp.where(short_row, jnp.finfo(F32).min, lo)
    thr = jnp.tile(to_col(thr_row), (1, reps))

    has_extra = jnp.max(jnp.where(jnp.logical_and(n_ge > kf, jnp.logical_not(short_row)), 1.0, 0.0))

    @pl.when(has_extra > 0.0)
    def _():
        keep = to_col(kf - count(thr_row, strict=True))[:, 0:1]
        upper = (lax.broadcasted_iota(jnp.int32, (tk, tk), 0)
                 <= lax.broadcasted_iota(jnp.int32, (tk, tk), 1))
        prefix = jnp.where(upper, 1.0, 0.0).astype(BF16)

        def body(j, seen):
            blk = sc_ref[:, cols(j)]
            tie = jnp.where(blk == thr, 1.0, 0.0)
            rank = seen + jnp.dot(tie.astype(BF16), prefix, preferred_element_type=F32)
            drop = jnp.where(rank > keep, tie, 0.0)
            sc_ref[:, cols(j)] = jnp.where(drop > 0.0, -jnp.inf, blk)
            return seen + jnp.sum(tie, axis=1, keepdims=True)

        lax.fori_loop(0, nb, body, jnp.zeros((tq, 1), F32))

    def mask_block(j, carry):
        mask_ref[0, :, cols(j)] = jnp.where(sc_ref[:, cols(j)] >= thr, 1, 0).astype(mask_ref.dtype)
        return carry

    def clear_block(j, carry):
        mask_ref[0, :, cols(j)] = jnp.zeros((tq, tk), mask_ref.dtype)
        return carry

    lax.fori_loop(0, nb, mask_block, 0)
    lax.fori_loop(nb, mask_ref.shape[2] // tk, clear_block, 0)


def _dsa_index(qi, kw, kit, tq, tk):
    b, s, w = qi.shape
    assert s % tk == 0 and s % tq == 0 and tq % CHUNK == 0
    topk = min(TOPK_MAX, s // 4)
    return pl.pallas_call(
        functools.partial(_dsa_index_kernel, tq=tq, tk=tk, topk=topk, max_iters=640),
        grid=(b, s // tq),
        in_specs=[pl.BlockSpec((1, tq, w), lambda bi, i: (bi, i, 0)),
                  pl.BlockSpec((1, tq, LANES), lambda bi, i: (bi, i, 0)),
                  _resident((1, LANES, s), lambda bi, i: (bi, 0, 0))],
        out_specs=pl.BlockSpec((1, tq, s), lambda bi, i: (bi, i, 0)),
        out_shape=jax.ShapeDtypeStruct((b, s, s), jnp.int8),
        scratch_shapes=[pltpu.VMEM((tq, s), F32)],
        compiler_params=_params("parallel", "arbitrary"),
        name="dsa_index",
    )(qi, kw, kit)


def _dsa_attend_kernel(q_ref, kt_ref, v_ref, mask_ref, o_ref, m_ref, acc_ref, *, tq, tk):
    nb = (pl.program_id(1) * tq + tq + tk - 1) // tk
    m_ref[...] = jnp.full_like(m_ref, MASKED)
    acc_ref[...] = jnp.zeros_like(acc_ref)
    q_heads = []
    for p in range(N_PAIRS):
        q_heads.extend(_head_halves(q_ref[0, :, p * LANES:(p + 1) * LANES]))
    first_half = lax.broadcasted_iota(jnp.int32, (tk, LANES), 1) < HEAD_DIM
    one = jnp.ones((), BF16)

    def attend_block(j, carry):
        cols = pl.ds(pl.multiple_of(j * tk, tk), tk)
        bias = jnp.where(mask_ref[0, :, cols].astype(jnp.int32) > 0, 0.0, MASKED).astype(BF16)
        scores, values = [], []
        for h in range(N_HEADS):
            p, e = divmod(h, 2)
            kt = kt_ref[0, p * LANES:(p + 1) * LANES, cols]
            scores.append(jnp.dot(q_heads[h], kt, preferred_element_type=F32).astype(BF16) + bias)
            v_pair = v_ref[0, cols, p * LANES:(p + 1) * LANES]
            values.append(jnp.where(first_half, v_pair, one) if e == 0 else jnp.where(first_half, one, v_pair))
        _online_softmax_block(scores, values, m_ref, acc_ref)
        return carry

    lax.fori_loop(0, nb, attend_block, 0)
    lane = lax.broadcasted_iota(jnp.int32, (tq, LANES), 1)
    for p in range(N_PAIRS):
        o_ref[0, :, p * LANES:(p + 1) * LANES] = jnp.where(
            lane < HEAD_DIM, _normalise(acc_ref[2 * p]), _normalise(acc_ref[2 * p + 1])).astype(o_ref.dtype)


def _dsa_attend(q, kt, v, mask, tq, tk):
    b, s, w = q.shape
    assert s % tk == 0 and s % tq == 0
    qtile = pl.BlockSpec((1, tq, w), lambda bi, i: (bi, i, 0))
    return pl.pallas_call(
        functools.partial(_dsa_attend_kernel, tq=tq, tk=tk),
        grid=(b, s // tq),
        in_specs=[qtile,
                  _resident((1, w, s), lambda bi, i: (bi, 0, 0)),
                  _resident((1, s, w), lambda bi, i: (bi, 0, 0)),
                  pl.BlockSpec((1, tq, s), lambda bi, i: (bi, i, 0))],
        out_specs=qtile,
        out_shape=jax.ShapeDtypeStruct((b, s, w), BF16),
        scratch_shapes=[pltpu.VMEM((N_HEADS, tq, LANES), F32), pltpu.VMEM((N_HEADS, tq, LANES), F32)],
        compiler_params=_params("parallel", "arbitrary"),
        name="dsa_attend",
    )(q, kt, v, mask)


def _dsa(qi, kw, kit, q, kt, v, tq=256, tk=512):
    s = q.shape[1]
    mask = _dsa_index(qi, kw, kit, min(tq, s), min(tk, s))
    return _dsa_attend(q, kt, v, mask, min(tq, s), min(tk, s))


def _fox_kernel(bound_ref, q_ref, kt_ref, v_ref, o_ref, m_ref, acc_ref, ckey_ref, *, tq, heads):
    i = pl.program_id(2)
    m_ref[...] = jnp.full_like(m_ref, MASKED)
    acc_ref[...] = jnp.zeros_like(acc_ref)
    causal = (lax.broadcasted_iota(jnp.int32, (tq, tq), 1) <= lax.broadcasted_iota(jnp.int32, (tq, tq), 0))

    @pl.when(i == 0)
    def _():
        for e in range(heads):
            tail = kt_ref[0, e, HEAD_DIM:HEAD_DIM + 16, :].astype(F32)
            row = lax.broadcasted_iota(jnp.int32, tail.shape, 0)
            ckey_ref[e] = -jnp.sum(jnp.where(jnp.logical_and(row >= 3, row < 6), tail, 0.0), axis=0, keepdims=True)

    pos = lax.broadcasted_iota(jnp.int32, (1, ckey_ref.shape[2]), 1)
    block_end = jnp.logical_and((pos & (tq - 1)) == tq - 1, pos < i * tq)
    lane = lax.broadcasted_iota(jnp.int32, (tq, LANES), 1)
    first = None
    for e in range(heads):
        c_first = ckey_ref[e, :, pl.ds(pl.multiple_of(i * tq, tq), LANES)][:, 0:1]
        dead = jnp.logical_and(block_end, ckey_ref[e] - c_first > 2.0 * bound_ref[0] + UNDERFLOW_BITS)
        n_dead = jnp.sum(jnp.where(dead, 1, 0))
        first = n_dead if first is None else jnp.minimum(first, n_dead)

    def block(j, diagonal):
        cols = pl.ds(pl.multiple_of(j * tq, tq), tq)
        scores = []
        for e in range(heads):
            s = jnp.dot(q_ref[0, e], kt_ref[0, e, :, cols], preferred_element_type=F32)
            scores.append(jnp.where(causal, s, MASKED) if diagonal else s)
        _online_softmax_block(scores, [v_ref[0, e, cols, :] for e in range(heads)], m_ref, acc_ref)

    def body(j, carry):
        block(j, False)
        return carry

    lax.fori_loop(first, i, body, 0)
    block(i, True)
    for p in range(heads // 2):
        a0, a1 = acc_ref[2 * p], acc_ref[2 * p + 1]
        o_ref[0, :, p * LANES:(p + 1) * LANES] = jnp.where(
            lane < HEAD_DIM, a0 / pltpu.roll(a0, HEAD_DIM, 1), pltpu.roll(a1, HEAD_DIM, 1) / a1).astype(o_ref.dtype)


def _fox(qk_bound, q, kt, v, tq=512, heads=4):
    b, nh, s, _ = q.shape
    assert s % tq == 0 and tq & (tq - 1) == 0 and nh == N_HEADS and nh % heads == 0 and heads % 2 == 0
    out_w = heads * HEAD_DIM
    return pl.pallas_call(
        functools.partial(_fox_kernel, tq=tq, heads=heads),
        grid=(b, nh // heads, s // tq),
        in_specs=[pl.BlockSpec(memory_space=pltpu.SMEM),
                  pl.BlockSpec((1, heads, tq, LANES), lambda bi, g, i: (bi, g, i, 0)),
                  _resident((1, heads, LANES, s), lambda bi, g, i: (bi, g, 0, 0)),
                  _resident((1, heads, s, LANES), lambda bi, g, i: (bi, g, 0, 0))],
        out_specs=pl.BlockSpec((1, tq, out_w), lambda bi, g, i: (bi, i, g)),
        out_shape=jax.ShapeDtypeStruct((b, s, N_HEADS * HEAD_DIM), BF16),
        scratch_shapes=[pltpu.VMEM((heads, tq, LANES), F32), pltpu.VMEM((heads, tq, LANES), F32),
                        pltpu.VMEM((heads, 1, s), F32)],
        compiler_params=_params("parallel", "parallel", "arbitrary"),
        name="fox",
    )(qk_bound, q, kt, v)


def _convmod_kernel(u_ref, cw_ref, cb_ref, g_ref, b_ref, o_ref, xbuf_ref, *, ts, width, halo):
    @pl.when(pl.program_id(1) == 0)
    def _():
        xbuf_ref[0:halo, :] = jnp.zeros((halo, width), F32)

    x = u_ref[0, :, 0:width] * jax.nn.sigmoid(u_ref[0, :, width:2 * width])
    xbuf_ref[halo:halo + ts, :] = x
    base = halo - (CONV_KERNEL - 1)
    y = cb_ref[...] + cw_ref[0:1, :] * xbuf_ref[pl.ds(base, ts), :]
    for j in range(1, CONV_KERNEL):
        y = y + cw_ref[j:j + 1, :] * xbuf_ref[pl.ds(base + j, ts), :]
    xbuf_ref[0:halo, :] = x[ts - halo:ts, :]
    mu = jnp.mean(y, axis=-1, keepdims=True)
    var = jnp.mean(jnp.square(y - mu), axis=-1, keepdims=True)
    z = (y - mu) * lax.rsqrt(var + NORM_EPS) * g_ref[...] + b_ref[...]
    o_ref[0] = (z * jax.nn.sigmoid(z)).astype(o_ref.dtype)


def _convmod(u, conv_w, conv_b, ln_g, ln_b, ts=512, halo=32):
    b, s, w2 = u.shape
    width = w2 // 2
    assert s % ts == 0 and halo >= CONV_KERNEL - 1
    row = lambda v: v.reshape(1, width)
    const = lambda shape: _resident(shape, lambda bi, si: (0, 0))
    return pl.pallas_call(
        functools.partial(_convmod_kernel, ts=ts, width=width, halo=halo),
        grid=(b, s // ts),
        in_specs=[pl.BlockSpec((1, ts, w2), lambda bi, si: (bi, si, 0)),
                  const((CONV_KERNEL, width)), const((1, width)), const((1, width)), const((1, width))],
        out_specs=pl.BlockSpec((1, ts, width), lambda bi, si: (bi, si, 0)),
        out_shape=jax.ShapeDtypeStruct((b, s, width), BF16),
        scratch_shapes=[pltpu.VMEM((ts + halo, width), F32)],
        compiler_params=_params("parallel", "arbitrary"),
        name="conv_module",
    )(u, conv_w, row(conv_b), row(ln_g), row(ln_b))


def _out_ffn_kernel(x_ref, ya_ref, yb_ref, wo_ref, g_ref, wgu_ref, wd_ref, o_ref, *, hidden, th):
    half = ya_ref.shape[1]
    x = x_ref[...] + (jnp.dot(ya_ref[...], wo_ref[0:half, :], preferred_element_type=F32)
                      + jnp.dot(yb_ref[...], wo_ref[half:2 * half, :], preferred_element_type=F32))
    ms = jnp.mean(x * x, axis=-1, keepdims=True)
    h = (x * lax.rsqrt(ms + NORM_EPS) * g_ref[...]).astype(BF16)
    y = x
    for c in range(0, hidden, th):
        gate = jnp.dot(h, wgu_ref[:, c:c + th], preferred_element_type=F32)
        up = jnp.dot(h, wgu_ref[:, hidden + c:hidden + c + th], preferred_element_type=F32)
        act = (gate * jax.nn.sigmoid(gate) * up).astype(BF16)
        y = y + jnp.dot(act, wd_ref[c:c + th, :], preferred_element_type=F32)
    o_ref[...] = y


def _out_ffn(x2, ya, yb, w_out, g, w_gu, w_down, tm=512, th=256):
    n, d = x2.shape
    half = ya.shape[1]
    hidden = w_down.shape[0]
    assert n % tm == 0 and hidden % th == 0
    const = lambda shape: _resident(shape, lambda i: (0, 0))
    return pl.pallas_call(
        functools.partial(_out_ffn_kernel, hidden=hidden, th=th),
        grid=(n // tm,),
        in_specs=[pl.BlockSpec((tm, d), lambda i: (i, 0)),
                  pl.BlockSpec((tm, half), lambda i: (i, 0)), pl.BlockSpec((tm, half), lambda i: (i, 0)),
                  const(w_out.shape), const((1, d)), const(w_gu.shape), const(w_down.shape)],
        out_specs=pl.BlockSpec((tm, d), lambda i: (i, 0)),
        out_shape=jax.ShapeDtypeStruct((n, d), F32),
        compiler_params=_params("parallel"),
        name="out_ffn",
    )(x2, ya, yb, w_out, g.reshape(1, d), w_gu, w_down)


def _rope_tables(s, width):
    inv = ROPE_THETA ** (-jnp.arange(0, HEAD_DIM, 2, dtype=F32) / HEAD_DIM)
    ang = jnp.arange(s, dtype=F32)[:, None] * inv[None, :]
    cos, sin = jnp.cos(ang), jnp.sin(ang)
    reps = width // HEAD_DIM
    return (jnp.tile(jnp.concatenate([cos, cos], axis=-1), (1, reps)),
            jnp.tile(jnp.concatenate([-sin, sin], axis=-1), (1, reps)))


def _block_diag(blocks):
    n, d, _ = blocks.shape
    eye = jnp.eye(n, dtype=blocks.dtype)
    return jnp.einsum('nde,nm->ndme', blocks, eye).reshape(n * d, n * d)


def _pad_cols(w, total):
    return jnp.pad(w, ((0, 0), (0, total - w.shape[1])))


def kernel(x, norm_mix, norm_ffn,
           ev_w_in, ev_conv_w, ev_conv_b, ev_w_r, ev_b_r, ev_w_i, ev_b_i, ev_lam,
           ev_q_norm, ev_k_norm, ev_w_out,
           od_w_in, od_b_f, od_q_norm, od_k_norm, od_conv_w, od_conv_b, od_ln_g, od_ln_b,
           od_w_out, ffn_w_gu, ffn_w_down):
    b, s, d = x.shape
    depth = norm_mix.shape[0]
    w = N_HEADS * HEAD_DIM
    cos, sin = _rope_tables(s, w)
    head_ones = _block_diag(jnp.ones((N_HEADS, HEAD_DIM, HEAD_DIM), BF16))
    tile_gain = lambda g: jnp.tile(g, N_HEADS).reshape(1, w)
    x2 = x.reshape(b * s, d)
    seq = lambda t: t.reshape(b, s, t.shape[-1])
    for l in range(depth):
        j = l // 2
        if l % 2 == 0:
            w_in = _pad_cols(ev_w_in[j], 2 * w + 4 * w + LANES).astype(BF16)
            xg, q, k, v, qi, kw = _norm_matmul(x2, norm_mix[l], w_in, (2 * w, w, w, w, w, LANES))
            ya = _lru(seq(xg), ev_conv_w[j], ev_conv_b[j],
                      _block_diag(ev_w_r[j]).astype(BF16), ev_b_r[j],
                      _block_diag(ev_w_i[j]).astype(BF16), ev_b_i[j], ev_lam[j])
            qh, kt, vh, qih, kit = _prep_even(seq(q), seq(k), seq(v), seq(qi), seq(kw), cos, sin,
                                              tile_gain(ev_q_norm[j]), tile_gain(ev_k_norm[j]), head_ones)
            yb = _dsa(qih, seq(kw), kit, qh, kt, vh)
            w_out = ev_w_out[j]
        else:
            wi = od_w_in[j]
            w_in = jnp.concatenate([wi[:, 0:3 * w], wi[:, 3 * w + N_HEADS:],
                                    _pad_cols(wi[:, 3 * w:3 * w + N_HEADS], LANES)], axis=1).astype(BF16)
            q, k, v, u, fl = _norm_matmul(x2, norm_mix[l], w_in, (w, w, w, 2 * w, LANES))
            qh, kt, vh = _prep_odd(seq(q), seq(k), seq(v), seq(fl),
                                   _pad_cols(od_b_f[j].reshape(1, N_HEADS), LANES),
                                   tile_gain(od_q_norm[j]), tile_gain(od_k_norm[j]), head_ones)
            qk_bound = (HEAD_DIM * QK_SCALE_LOG2 * 1.01 * jnp.max(jnp.abs(od_q_norm[j]))
                        * jnp.max(jnp.abs(od_k_norm[j])) + 1.0).reshape(1).astype(F32)
            ya = _fox(qk_bound, qh, kt, vh, tq=min(512, s))
            yb = _convmod(seq(u), od_conv_w[j], od_conv_b[j], od_ln_g[j], od_ln_b[j])
            w_out = od_w_out[j]
        x2 = _out_ffn(x2, ya.reshape(b * s, w), yb.reshape(b * s, w), w_out.astype(BF16),
                      norm_ffn[l], ffn_w_gu[l].astype(BF16), ffn_w_down[l].astype(BF16))
    return x2.reshape(b, s, d)
```

```python
import functools

import jax
import jax.numpy as jnp
from jax import lax
from jax.experimental import pallas as pl
from jax.experimental.pallas import tpu as pltpu

F32 = jnp.float32
BF16 = jnp.bfloat16

HEAD_DIM = 64
N_HEADS = 8
N_PAIRS = N_HEADS // 2
LANES = 128
SUBLANES = 8
CHUNK = 64
CHUNK_SHIFT = 6
TOPK_MAX = 256
ROPE_THETA = 10000.0
NORM_EPS = 1e-6
LRU_C = 8.0
LRU_CONV = 4
CONV_KERNEL = 31
MASKED = -2.0 ** 100
UNDERFLOW_BITS = 160.0
LOG2E = 1.4426950408889634
QK_SCALE_LOG2 = HEAD_DIM ** -0.5 * LOG2E
VMEM_LIMIT = 56 * 1024 * 1024

assert CHUNK == 1 << CHUNK_SHIFT


def _params(*sem):
    return pltpu.CompilerParams(dimension_semantics=sem, vmem_limit_bytes=VMEM_LIMIT)


def _resident(shape, index_map):
    return pl.BlockSpec(shape, index_map, pipeline_mode=pl.Buffered(1))


def _norm_matmul_kernel(x_ref, g_ref, w_ref, *o_refs, widths):
    x = x_ref[...]
    ms = jnp.mean(x * x, axis=-1, keepdims=True)
    h = (x * lax.rsqrt(ms + NORM_EPS) * g_ref[...]).astype(BF16)
    off = 0
    for o_ref, wd in zip(o_refs, widths):
        o_ref[...] = jnp.dot(h, w_ref[:, off:off + wd], preferred_element_type=F32)
        off += wd


def _norm_matmul(x2, g, w, widths, tm=512):
    n, d = x2.shape
    assert n % tm == 0 and sum(widths) == w.shape[1]
    return pl.pallas_call(
        functools.partial(_norm_matmul_kernel, widths=widths),
        grid=(n // tm,),
        in_specs=[pl.BlockSpec((tm, d), lambda i: (i, 0)),
                  _resident((1, d), lambda i: (0, 0)),
                  _resident(w.shape, lambda i: (0, 0))],
        out_specs=[pl.BlockSpec((tm, wd), lambda i: (i, 0)) for wd in widths],
        out_shape=[jax.ShapeDtypeStruct((n, wd), F32) for wd in widths],
        compiler_params=_params("parallel"),
        name="norm_matmul",
    )(x2, g.reshape(1, d), w)


def _shift_rows(x, d, fill):
    row = lax.broadcasted_iota(jnp.int32, x.shape, 0)
    return jnp.where(row >= d, pltpu.roll(x, d, 0), fill)


def _linear_scan_rows(a, u):
    d = 1
    while d < a.shape[0]:
        u = a * _shift_rows(u, d, 0.0) + u
        a = a * _shift_rows(a, d, 1.0)
        d *= 2
    return a, u


def _cumsum_rows(x):
    d = 1
    while d < x.shape[0]:
        x = x + _shift_rows(x, d, 0.0)
        d *= 2
    return x


def _softplus(x):
    return jnp.maximum(x, 0.0) + jnp.log1p(jnp.exp(-jnp.abs(x)))


def _lru_kernel(xg_ref, cw_ref, cb_ref, wr_ref, br_ref, wi_ref, bi_ref, lam_ref, o_ref,
                xbuf_ref, h_ref, *, ts, width):
    @pl.when(pl.program_id(1) == 0)
    def _():
        xbuf_ref[0:SUBLANES, :] = jnp.zeros((SUBLANES, width), F32)
        h_ref[...] = jnp.zeros_like(h_ref)

    xa = xg_ref[0, :, 0:width]
    ga = xg_ref[0, :, width:2 * width]
    xbuf_ref[SUBLANES:SUBLANES + ts, :] = xa
    base = SUBLANES - (LRU_CONV - 1)
    xc = cb_ref[...] + cw_ref[0:1, :] * xbuf_ref[pl.ds(base, ts), :]
    for j in range(1, LRU_CONV):
        xc = xc + cw_ref[j:j + 1, :] * xbuf_ref[pl.ds(base + j, ts), :]
    xbuf_ref[0:SUBLANES, :] = xa[ts - SUBLANES:ts, :]

    xb = xc.astype(BF16)
    r = jax.nn.sigmoid(jnp.dot(xb, wr_ref[...], preferred_element_type=F32) + br_ref[...])
    gate = jax.nn.sigmoid(jnp.dot(xb, wi_ref[...], preferred_element_type=F32) + bi_ref[...])
    log_a = -LRU_C * r * _softplus(-lam_ref[...])
    a = jnp.exp(log_a)
    th = jnp.tanh(log_a)
    u = jnp.sqrt(-2.0 * th / (1.0 - th)) * (gate * xc)
    a_cum, h = _linear_scan_rows(a, u)
    h = h + a_cum * h_ref[...]
    h_ref[...] = h[ts - 1:ts, :]
    o_ref[0] = (h * jax.nn.gelu(ga, approximate=True)).astype(o_ref.dtype)


def _lru(xg, conv_w, conv_b, w_r, b_r, w_i, b_i, lam, ts=256):
    b, s, w2 = xg.shape
    width = w2 // 2
    assert s % ts == 0
    row = lambda v: v.reshape(1, width)
    const = lambda shape: _resident(shape, lambda bi, si: (0, 0))
    return pl.pallas_call(
        functools.partial(_lru_kernel, ts=ts, width=width),
        grid=(b, s // ts),
        in_specs=[pl.BlockSpec((1, ts, w2), lambda bi, si: (bi, si, 0)),
                  const((LRU_CONV, width)), const((1, width)),
                  const((width, width)), const((1, width)),
                  const((width, width)), const((1, width)), const((1, width))],
        out_specs=pl.BlockSpec((1, ts, width), lambda bi, si: (bi, si, 0)),
        out_shape=jax.ShapeDtypeStruct((b, s, width), BF16),
        scratch_shapes=[pltpu.VMEM((ts + SUBLANES, width), F32), pltpu.VMEM((1, width), F32)],
        compiler_params=_params("parallel", "arbitrary"),
        name="rg_lru",
    )(xg, conv_w, row(conv_b), w_r, row(b_r), w_i, row(b_i), row(lam))


def _head_mean_square(x, bd_ref):
    x2 = x * x
    hi = x2.astype(BF16)
    r1 = x2 - hi.astype(F32)
    mid = r1.astype(BF16)
    lo = (r1 - mid.astype(F32)).astype(BF16)
    bd = bd_ref[...]
    tot = (jnp.dot(hi, bd, preferred_element_type=F32) + jnp.dot(mid, bd, preferred_element_type=F32)
           + jnp.dot(lo, bd, preferred_element_type=F32))
    return tot * (1.0 / HEAD_DIM)


def _head_rms_norm(x, g, bd_ref):
    return x * lax.rsqrt(_head_mean_square(x, bd_ref) + NORM_EPS) * g


def _rope(x, cos, sin_signed):
    n = x.shape[1]
    lane = lax.broadcasted_iota(jnp.int32, x.shape, 1)
    first_half = (lane & (HEAD_DIM - 1)) < HEAD_DIM // 2
    partner = jnp.where(first_half, pltpu.roll(x, n - HEAD_DIM // 2, 1), pltpu.roll(x, HEAD_DIM // 2, 1))
    return x * cos + partner * sin_signed


def _prep_even_kernel(q_ref, k_ref, v_ref, qi_ref, kw_ref, cos_ref, sin_ref, gq_ref, gk_ref, bd_ref,
                      qo_ref, kto_ref, vo_ref, qio_ref, kito_ref):
    cos = cos_ref[...]
    sin = sin_ref[...]
    q = _rope(_head_rms_norm(q_ref[0], gq_ref[...], bd_ref), cos, sin)
    qo_ref[0] = (q * QK_SCALE_LOG2).astype(BF16)
    k = _rope(_head_rms_norm(k_ref[0], gk_ref[...], bd_ref), cos, sin)
    kto_ref[0] = k.T.astype(BF16)
    vo_ref[0] = v_ref[0].astype(BF16)
    qio_ref[0] = _rope(qi_ref[0], cos, sin).astype(BF16)
    ki = _rope(kw_ref[0], cos[:, 0:LANES], sin[:, 0:LANES])
    lane = lax.broadcasted_iota(jnp.int32, ki.shape, 1)
    ki2 = jnp.where(lane < HEAD_DIM, ki, pltpu.roll(ki, HEAD_DIM, 1))
    kito_ref[0] = ki2.T.astype(BF16)


def _prep_even(q, k, v, qi, kw, cos, sin, gq, gk, bd, ts=512):
    b, s, w = q.shape
    assert s % ts == 0
    tile = pl.BlockSpec((1, ts, w), lambda bi, si: (bi, si, 0))
    tab = pl.BlockSpec((ts, w), lambda bi, si: (si, 0))
    const = lambda shape: _resident(shape, lambda bi, si: (0, 0))
    return pl.pallas_call(
        _prep_even_kernel,
        grid=(b, s // ts),
        in_specs=[tile, tile, tile, tile, pl.BlockSpec((1, ts, LANES), lambda bi, si: (bi, si, 0)),
                  tab, tab, const((1, w)), const((1, w)), const((w, w))],
        out_specs=[tile, pl.BlockSpec((1, w, ts), lambda bi, si: (bi, 0, si)), tile, tile,
                   pl.BlockSpec((1, LANES, ts), lambda bi, si: (bi, 0, si))],
        out_shape=[jax.ShapeDtypeStruct((b, s, w), BF16), jax.ShapeDtypeStruct((b, w, s), BF16),
                   jax.ShapeDtypeStruct((b, s, w), BF16), jax.ShapeDtypeStruct((b, s, w), BF16),
                   jax.ShapeDtypeStruct((b, LANES, s), BF16)],
        compiler_params=_params("parallel", "parallel"),
        name="prep_even",
    )(q, k, v, qi, kw, cos, sin, gq, gk, bd)


def _prep_odd_kernel(q_ref, k_ref, v_ref, fl_ref, bf_ref, gq_ref, gk_ref, bd_ref,
                     qo_ref, kto_ref, vo_ref, carry_ref):
    @pl.when(pl.program_id(1) == 0)
    def _():
        carry_ref[...] = jnp.zeros_like(carry_ref)

    q = _head_rms_norm(q_ref[0], gq_ref[...], bd_ref) * QK_SCALE_LOG2
    k = _head_rms_norm(k_ref[0], gk_ref[...], bd_ref)
    v = v_ref[0]
    log_f = -_softplus(-(fl_ref[0] + bf_ref[...]))
    c = _cumsum_rows(log_f) + carry_ref[...]
    carry_ref[...] = c[c.shape[0] - 1:, :]
    c2 = c * LOG2E
    lane = lax.broadcasted_iota(jnp.int32, (c.shape[0], LANES), 1)
    for h in range(N_HEADS):
        p, e = divmod(h, 2)
        pick = lambda t: t[:, p * LANES:(p + 1) * LANES] if e == 0 else pltpu.roll(t[:, p * LANES:(p + 1) * LANES], HEAD_DIM, 1)
        cb = jnp.broadcast_to(c2[:, h:h + 1], (c.shape[0], LANES))
        hi = cb.astype(BF16).astype(F32)
        mid = (cb - hi).astype(BF16).astype(F32)
        lo = cb - hi - mid
        q_bias = jnp.where(lane == HEAD_DIM, hi, jnp.where(lane == HEAD_DIM + 1, mid, jnp.where(
            lane == HEAD_DIM + 2, lo, jnp.where(lane < HEAD_DIM + 6, 1.0, 0.0))))
        k_bias = jnp.where(lane < HEAD_DIM + 3, 1.0, jnp.where(lane == HEAD_DIM + 3, -hi, jnp.where(
            lane == HEAD_DIM + 4, -mid, jnp.where(lane == HEAD_DIM + 5, -lo, 0.0))))
        qo_ref[0, h] = jnp.where(lane < HEAD_DIM, pick(q), q_bias).astype(BF16)
        kto_ref[0, h] = jnp.where(lane < HEAD_DIM, pick(k), k_bias).T.astype(BF16)
        vo_ref[0, h] = jnp.where(lane < HEAD_DIM, pick(v), 1.0).astype(BF16)


def _prep_odd(q, k, v, fl, bf, gq, gk, bd, ts=512):
    b, s, w = q.shape
    assert s % ts == 0
    tile = pl.BlockSpec((1, ts, w), lambda bi, si: (bi, si, 0))
    narrow = pl.BlockSpec((1, ts, LANES), lambda bi, si: (bi, si, 0))
    heads = pl.BlockSpec((1, N_HEADS, ts, LANES), lambda bi, si: (bi, 0, si, 0))
    const = lambda shape: _resident(shape, lambda bi, si: (0, 0))
    return pl.pallas_call(
        _prep_odd_kernel,
        grid=(b, s // ts),
        in_specs=[tile, tile, tile, narrow, const((1, LANES)), const((1, w)), const((1, w)), const((w, w))],
        out_specs=[heads, pl.BlockSpec((1, N_HEADS, LANES, ts), lambda bi, si: (bi, 0, 0, si)), heads],
        out_shape=[jax.ShapeDtypeStruct((b, N_HEADS, s, LANES), BF16),
                   jax.ShapeDtypeStruct((b, N_HEADS, LANES, s), BF16),
                   jax.ShapeDtypeStruct((b, N_HEADS, s, LANES), BF16)],
        scratch_shapes=[pltpu.VMEM((1, LANES), F32)],
        compiler_params=_params("parallel", "arbitrary"),
        name="prep_odd",
    )(q, k, v, fl, bf, gq, gk, bd)


def _head_halves(x_pair):
    lane = lax.broadcasted_iota(jnp.int32, x_pair.shape, 1)
    zero = jnp.zeros_like(x_pair)
    return jnp.where(lane < HEAD_DIM, x_pair, zero), jnp.where(lane >= HEAD_DIM, x_pair, zero)


def _fold_lanes(x, op):
    out = x[:, 0:LANES]
    for c in range(1, x.shape[1] // LANES):
        out = op(out, x[:, c * LANES:(c + 1) * LANES])
    return out


def _online_softmax_block(scores, values, m_ref, acc_ref):
    probs, alphas = [], []
    for h, s in enumerate(scores):
        m_old = m_ref[h]
        m_new = jnp.maximum(m_old, jnp.max(_fold_lanes(s, jnp.maximum), axis=1, keepdims=True).astype(F32))
        p = jnp.exp2(s - jnp.tile(m_new.astype(s.dtype), (1, s.shape[1] // LANES)))
        m_ref[h] = m_new
        probs.append(p.astype(BF16))
        alphas.append(jnp.exp2(m_old - m_new))
    for h, (p, alpha) in enumerate(zip(probs, alphas)):
        acc_ref[h] = alpha * acc_ref[h] + jnp.dot(p, values[h], preferred_element_type=F32)


def _to_ordinal(x):
    bits = lax.bitcast_convert_type(x, jnp.int32)
    return jnp.where(bits < 0, bits ^ jnp.int32(0x7FFFFFFF), bits)


def _from_ordinal(o):
    return lax.bitcast_convert_type(jnp.where(o < 0, o ^ jnp.int32(0x7FFFFFFF), o), F32)


def _ordinal_midpoint(lo, hi):
    a, b = _to_ordinal(lo), _to_ordinal(hi)
    return _from_ordinal((a >> 1) + (b >> 1) + (a & b & 1))


def _next_float_up(x):
    return _from_ordinal(_to_ordinal(x) + 1)


def _normalise(acc):
    return acc / pltpu.roll(acc, HEAD_DIM, 1)


def _dsa_index_kernel(qi_ref, kw_ref, kit_ref, mask_ref, sc_ref, lo_ref, hi_ref, *, tq, tk, topk, max_iters):
    t0 = pl.program_id(1) * tq
    nb = (t0 + tq + tk - 1) // tk
    row_chunk = (t0 + lax.broadcasted_iota(jnp.int32, (tq, 1), 0)) >> CHUNK_SHIFT
    kf = float(topk)

    def cols(j):
        return pl.ds(pl.multiple_of(j * tk, tk), tk)

    qi_heads = []
    for p in range(N_PAIRS):
        qi_heads.extend(_head_halves(qi_ref[0, :, p * LANES:(p + 1) * LANES]))
    kw = kw_ref[0]

    def score_block(j, carry):
        kit = kit_ref[0, :, cols(j)]
        score = jnp.zeros((tq, tk), F32)
        for h in range(N_HEADS):
            rel = jnp.maximum(jnp.dot(qi_heads[h], kit, preferred_element_type=F32), 0.0)
            score = score + kw[:, HEAD_DIM + h:HEAD_DIM + h + 1] * rel
        key_chunk = (j * tk + lax.broadcasted_iota(jnp.int32, (1, tk), 1)) >> CHUNK_SHIFT
        visible = key_chunk <= row_chunk
        sc_ref[:, cols(j)] = jnp.where(visible, score, -jnp.inf)
        lo_ref[...] = jnp.minimum(lo_ref[...], _fold_lanes(jnp.where(visible, score, jnp.inf), jnp.minimum))
        hi_ref[...] = jnp.maximum(hi_ref[...], _fold_lanes(jnp.where(visible, score, -jnp.inf), jnp.maximum))
        return carry

    lo_ref[...] = jnp.full_like(lo_ref, jnp.inf)
    hi_ref[...] = jnp.full_like(hi_ref, -jnp.inf)
    lax.fori_loop(0, nb, score_block, 0)

    reps = tk // LANES

    def to_col(row):
        return jnp.broadcast_to(row, (LANES, tq)).T

    def to_row(wide, reduce):
        return reduce(wide.T, axis=0, keepdims=True)

    def count(row, strict=False, wanted=None):
        col = to_col(row)
        parts = []
        for g, r in enumerate(range(0, tq, LANES)):
            bound = jnp.tile(col[r:r + LANES], (1, reps))

            def body(j, cnt):
                blk = sc_ref[r:r + LANES, cols(j)]
                hit = blk > bound if strict else blk >= bound
                return cnt + _fold_lanes(jnp.where(hit, 1.0, 0.0), jnp.add)

            blocks = nb if wanted is None else jnp.where(wanted[g] > 0.0, nb, 0)
            parts.append(lax.fori_loop(0, blocks, body, jnp.zeros((LANES, LANES), F32)))
        return to_row(jnp.concatenate(parts, axis=0), jnp.sum)

    lo0 = to_row(lo_ref[...], jnp.min)
    hi0 = _next_float_up(to_row(hi_ref[...], jnp.max))
    n_visible = (((t0 + lax.broadcasted_iota(jnp.int32, (1, tq), 1)) >> CHUNK_SHIFT) + 1) * CHUNK
    short_row = n_visible <= topk
    log_k = jnp.log(kf + 0.5)

    def open_groups(lo, hi, c_lo):
        finished = jnp.logical_or(jnp.logical_or(c_lo == kf, short_row),
                                  _to_ordinal(_ordinal_midpoint(lo, hi)) == _to_ordinal(lo))
        still = jnp.where(finished, 0.0, 1.0)
        return tuple(jnp.max(still[:, r:r + LANES]) for r in range(0, tq, LANES))

    def search_cond(state):
        return jnp.logical_and(state[0] < max_iters, functools.reduce(jnp.maximum, state[-1]) > 0.0)

    def search_body(state):
        it, lo, hi, c_lo, f_lo, f_hi, side, wanted = state
        guess = lo + (hi - lo) * jnp.clip(f_lo / (f_lo - f_hi), 1.0 / 64, 63.0 / 64)
        inside = jnp.logical_and(guess > lo, guess < hi)
        half = _ordinal_midpoint(lo, hi)
        mid = jnp.where(it % 3 == 2, half, jnp.where(inside, guess, half))
        c_mid = count(mid, wanted=wanted)
        counted = jnp.concatenate([jnp.zeros((1, LANES), F32) + w for w in wanted], axis=1) > 0.0
        enough = jnp.logical_and(counted, c_mid >= kf)
        short = jnp.logical_and(counted, c_mid < kf)
        f_mid = jnp.log(jnp.maximum(c_mid, 0.5)) - log_k
        f_hi = jnp.where(enough, jnp.where(side > 0.0, f_hi * 0.5, f_hi), jnp.where(short, f_mid, f_hi))
        f_lo = jnp.where(enough, f_mid, jnp.where(jnp.logical_and(short, side < 0.0), f_lo * 0.5, f_lo))
        lo, c_lo = jnp.where(enough, mid, lo), jnp.where(enough, c_mid, c_lo)
        hi = jnp.where(short, mid, hi)
        side = jnp.where(enough, 1.0, jnp.where(short, -1.0, side))
        return it + 1, lo, hi, c_lo, f_lo, f_hi, side, open_groups(lo, hi, c_lo)

    c_lo0 = n_visible.astype(F32)
    state0 = (jnp.int32(0), lo0, hi0, c_lo0, jnp.log(c_lo0) - log_k, jnp.log(0.5) - log_k + jnp.zeros_like(lo0),
              jnp.zeros_like(lo0), open_groups(lo0, hi0, c_lo0))
    _, lo, _, n_ge, _, _, _, _ = lax.while_loop(search_cond, search_body, state0)
    thr_row = jnp.where(short_row, jnp.finfo(F32).min, lo)
    thr = jnp.tile(to_col(thr_row), (1, reps))

    has_extra = jnp.max(jnp.where(jnp.logical_and(n_ge > kf, jnp.logical_not(short_row)), 1.0, 0.0))

    @pl.when(has_extra > 0.0)
    def _():
        keep = to_col(kf - count(thr_row, strict=True))[:, 0:1]
        upper = (lax.broadcasted_iota(jnp.int32, (tk, tk), 0)
                 <= lax.broadcasted_iota(jnp.int32, (tk, tk), 1))
        prefix = jnp.where(upper, 1.0, 0.0).astype(BF16)

        def body(j, seen):
            blk = sc_ref[:, cols(j)]
            tie = jnp.where(blk == thr, 1.0, 0.0)
            rank = seen + jnp.dot(tie.astype(BF16), prefix, preferred_element_type=F32)
            drop = jnp.where(rank > keep, tie, 0.0)
            sc_ref[:, cols(j)] = jnp.where(drop > 0.0, -jnp.inf, blk)
            return seen + jnp.sum(tie, axis=1, keepdims=True)

        lax.fori_loop(0, nb, body, jnp.zeros((tq, 1), F32))

    def mask_block(j, carry):
        mask_ref[0, :, cols(j)] = jnp.where(sc_ref[:, cols(j)] >= thr, 1, 0).astype(mask_ref.dtype)
        return carry

    def clear_block(j, carry):
        mask_ref[0, :, cols(j)] = jnp.zeros((tq, tk), mask_ref.dtype)
        return carry

    lax.fori_loop(0, nb, mask_block, 0)
    lax.fori_loop(nb, mask_ref.shape[2] // tk, clear_block, 0)


def _dsa_index(qi, kw, kit, tq, tk):
    b, s, w = qi.shape
    assert s % tk == 0 and s % tq == 0 and tq % CHUNK == 0
    topk = min(TOPK_MAX, s // 4)
    return pl.pallas_call(
        functools.partial(_dsa_index_kernel, tq=tq, tk=tk, topk=topk, max_iters=640),
        grid=(b, s // tq),
        in_specs=[pl.BlockSpec((1, tq, w), lambda bi, i: (bi, i, 0)),
                  pl.BlockSpec((1, tq, LANES), lambda bi, i: (bi, i, 0)),
                  _resident((1, LANES, s), lambda bi, i: (bi, 0, 0))],
        out_specs=pl.BlockSpec((1, tq, s), lambda bi, i: (bi, i, 0)),
        out_shape=jax.ShapeDtypeStruct((b, s, s), jnp.int8),
        scratch_shapes=[pltpu.VMEM((tq, s), F32), pltpu.VMEM((tq, LANES), F32), pltpu.VMEM((tq, LANES), F32)],
        compiler_params=_params("parallel", "arbitrary"),
        name="dsa_index",
    )(qi, kw, kit)


def _dsa_attend_kernel(q_ref, kt_ref, v_ref, mask_ref, o_ref, m_ref, acc_ref, *, tq, tk):
    nb = (pl.program_id(1) * tq + tq + tk - 1) // tk
    m_ref[...] = jnp.full_like(m_ref, MASKED)
    acc_ref[...] = jnp.zeros_like(acc_ref)
    q_heads = []
    for p in range(N_PAIRS):
        q_heads.extend(_head_halves(q_ref[0, :, p * LANES:(p + 1) * LANES]))
    first_half = lax.broadcasted_iota(jnp.int32, (tk, LANES), 1) < HEAD_DIM
    one = jnp.ones((), BF16)

    def attend_block(j, carry):
        cols = pl.ds(pl.multiple_of(j * tk, tk), tk)
        bias = jnp.where(mask_ref[0, :, cols].astype(jnp.int32) > 0, 0.0, MASKED).astype(BF16)
        scores, values = [], []
        for h in range(N_HEADS):
            p, e = divmod(h, 2)
            kt = kt_ref[0, p * LANES:(p + 1) * LANES, cols]
            scores.append(jnp.dot(q_heads[h], kt, preferred_element_type=F32).astype(BF16) + bias)
            v_pair = v_ref[0, cols, p * LANES:(p + 1) * LANES]
            values.append(jnp.where(first_half, v_pair, one) if e == 0 else jnp.where(first_half, one, v_pair))
        _online_softmax_block(scores, values, m_ref, acc_ref)
        return carry

    lax.fori_loop(0, nb, attend_block, 0)
    lane = lax.broadcasted_iota(jnp.int32, (tq, LANES), 1)
    for p in range(N_PAIRS):
        o_ref[0, :, p * LANES:(p + 1) * LANES] = jnp.where(
            lane < HEAD_DIM, _normalise(acc_ref[2 * p]), _normalise(acc_ref[2 * p + 1])).astype(o_ref.dtype)


def _dsa_attend(q, kt, v, mask, tq, tk):
    b, s, w = q.shape
    assert s % tk == 0 and s % tq == 0
    qtile = pl.BlockSpec((1, tq, w), lambda bi, i: (bi, i, 0))
    return pl.pallas_call(
        functools.partial(_dsa_attend_kernel, tq=tq, tk=tk),
        grid=(b, s // tq),
        in_specs=[qtile,
                  _resident((1, w, s), lambda bi, i: (bi, 0, 0)),
                  _resident((1, s, w), lambda bi, i: (bi, 0, 0)),
                  pl.BlockSpec((1, tq, s), lambda bi, i: (bi, i, 0))],
        out_specs=qtile,
        out_shape=jax.ShapeDtypeStruct((b, s, w), BF16),
        scratch_shapes=[pltpu.VMEM((N_HEADS, tq, LANES), F32), pltpu.VMEM((N_HEADS, tq, LANES), F32)],
        compiler_params=_params("parallel", "arbitrary"),
        name="dsa_attend",
    )(q, kt, v, mask)


def _dsa(qi, kw, kit, q, kt, v, tq=256, tk=512):
    s = q.shape[1]
    mask = _dsa_index(qi, kw, kit, min(tq, s), min(tk, s))
    return _dsa_attend(q, kt, v, mask, min(tq, s), min(tk, s))


def _fox_kernel(bound_ref, q_ref, kt_ref, v_ref, o_ref, m_ref, acc_ref, ckey_ref, *, tq, heads):
    i = pl.program_id(2)
    m_ref[...] = jnp.full_like(m_ref, MASKED)
    acc_ref[...] = jnp.zeros_like(acc_ref)
    causal = (lax.broadcasted_iota(jnp.int32, (tq, tq), 1) <= lax.broadcasted_iota(jnp.int32, (tq, tq), 0))

    @pl.when(i == 0)
    def _():
        for e in range(heads):
            tail = kt_ref[0, e, HEAD_DIM:HEAD_DIM + 16, :].astype(F32)
            row = lax.broadcasted_iota(jnp.int32, tail.shape, 0)
            ckey_ref[e] = -jnp.sum(jnp.where(jnp.logical_and(row >= 3, row < 6), tail, 0.0), axis=0, keepdims=True)

    pos = lax.broadcasted_iota(jnp.int32, (1, ckey_ref.shape[2]), 1)
    block_end = jnp.logical_and((pos & (tq - 1)) == tq - 1, pos < i * tq)
    lane = lax.broadcasted_iota(jnp.int32, (tq, LANES), 1)
    first = None
    for e in range(heads):
        c_first = ckey_ref[e, :, pl.ds(pl.multiple_of(i * tq, tq), LANES)][:, 0:1]
        dead = jnp.logical_and(block_end, ckey_ref[e] - c_first > 2.0 * bound_ref[0] + UNDERFLOW_BITS)
        n_dead = jnp.sum(jnp.where(dead, 1, 0))
        first = n_dead if first is None else jnp.minimum(first, n_dead)

    def block(j, diagonal):
        cols = pl.ds(pl.multiple_of(j * tq, tq), tq)
        scores = []
        for e in range(heads):
            s = jnp.dot(q_ref[0, e], kt_ref[0, e, :, cols], preferred_element_type=F32)
            scores.append(jnp.where(causal, s, MASKED) if diagonal else s)
        _online_softmax_block(scores, [v_ref[0, e, cols, :] for e in range(heads)], m_ref, acc_ref)

    def body(j, carry):
        block(j, False)
        return carry

    lax.fori_loop(first, i, body, 0)
    block(i, True)
    for p in range(heads // 2):
        a0, a1 = acc_ref[2 * p], acc_ref[2 * p + 1]
        o_ref[0, :, p * LANES:(p + 1) * LANES] = jnp.where(
            lane < HEAD_DIM, a0 / pltpu.roll(a0, HEAD_DIM, 1), pltpu.roll(a1, HEAD_DIM, 1) / a1).astype(o_ref.dtype)


def _fox(qk_bound, q, kt, v, tq=512, heads=4):
    b, nh, s, _ = q.shape
    assert s % tq == 0 and tq & (tq - 1) == 0 and nh == N_HEADS and nh % heads == 0 and heads % 2 == 0
    out_w = heads * HEAD_DIM
    return pl.pallas_call(
        functools.partial(_fox_kernel, tq=tq, heads=heads),
        grid=(b, nh // heads, s // tq),
        in_specs=[pl.BlockSpec(memory_space=pltpu.SMEM),
                  pl.BlockSpec((1, heads, tq, LANES), lambda bi, g, i: (bi, g, i, 0)),
                  _resident((1, heads, LANES, s), lambda bi, g, i: (bi, g, 0, 0)),
                  _resident((1, heads, s, LANES), lambda bi, g, i: (bi, g, 0, 0))],
        out_specs=pl.BlockSpec((1, tq, out_w), lambda bi, g, i: (bi, i, g)),
        out_shape=jax.ShapeDtypeStruct((b, s, N_HEADS * HEAD_DIM), BF16),
        scratch_shapes=[pltpu.VMEM((heads, tq, LANES), F32), pltpu.VMEM((heads, tq, LANES), F32),
                        pltpu.VMEM((heads, 1, s), F32)],
        compiler_params=_params("parallel", "parallel", "arbitrary"),
        name="fox",
    )(qk_bound, q, kt, v)


def _convmod_kernel(u_ref, cw_ref, cb_ref, g_ref, b_ref, o_ref, xbuf_ref, *, ts, width, halo):
    @pl.when(pl.program_id(1) == 0)
    def _():
        xbuf_ref[0:halo, :] = jnp.zeros((halo, width), F32)

    x = u_ref[0, :, 0:width] * jax.nn.sigmoid(u_ref[0, :, width:2 * width])
    xbuf_ref[halo:halo + ts, :] = x
    base = halo - (CONV_KERNEL - 1)
    y = cb_ref[...] + cw_ref[0:1, :] * xbuf_ref[pl.ds(base, ts), :]
    for j in range(1, CONV_KERNEL):
        y = y + cw_ref[j:j + 1, :] * xbuf_ref[pl.ds(base + j, ts), :]
    xbuf_ref[0:halo, :] = x[ts - halo:ts, :]
    mu = jnp.mean(y, axis=-1, keepdims=True)
    var = jnp.mean(jnp.square(y - mu), axis=-1, keepdims=True)
    z = (y - mu) * lax.rsqrt(var + NORM_EPS) * g_ref[...] + b_ref[...]
    o_ref[0] = (z * jax.nn.sigmoid(z)).astype(o_ref.dtype)


def _convmod(u, conv_w, conv_b, ln_g, ln_b, ts=512, halo=32):
    b, s, w2 = u.shape
    width = w2 // 2
    assert s % ts == 0 and halo >= CONV_KERNEL - 1
    row = lambda v: v.reshape(1, width)
    const = lambda shape: _resident(shape, lambda bi, si: (0, 0))
    return pl.pallas_call(
        functools.partial(_convmod_kernel, ts=ts, width=width, halo=halo),
        grid=(b, s // ts),
        in_specs=[pl.BlockSpec((1, ts, w2), lambda bi, si: (bi, si, 0)),
                  const((CONV_KERNEL, width)), const((1, width)), const((1, width)), const((1, width))],
        out_specs=pl.BlockSpec((1, ts, width), lambda bi, si: (bi, si, 0)),
        out_shape=jax.ShapeDtypeStruct((b, s, width), BF16),
        scratch_shapes=[pltpu.VMEM((ts + halo, width), F32)],
        compiler_params=_params("parallel", "arbitrary"),
        name="conv_module",
    )(u, conv_w, row(conv_b), row(ln_g), row(ln_b))


def _out_ffn_kernel(x_ref, ya_ref, yb_ref, wo_ref, g_ref, wgu_ref, wd_ref, o_ref, *, hidden, th):
    half = ya_ref.shape[1]
    x = x_ref[...] + (jnp.dot(ya_ref[...], wo_ref[0:half, :], preferred_element_type=F32)
                      + jnp.dot(yb_ref[...], wo_ref[half:2 * half, :], preferred_element_type=F32))
    ms = jnp.mean(x * x, axis=-1, keepdims=True)
    h = (x * lax.rsqrt(ms + NORM_EPS) * g_ref[...]).astype(BF16)
    y = x
    for c in range(0, hidden, th):
        gate = jnp.dot(h, wgu_ref[:, c:c + th], preferred_element_type=F32)
        up = jnp.dot(h, wgu_ref[:, hidden + c:hidden + c + th], preferred_element_type=F32)
        act = (gate * jax.nn.sigmoid(gate) * up).astype(BF16)
        y = y + jnp.dot(act, wd_ref[c:c + th, :], preferred_element_type=F32)
    o_ref[...] = y


def _out_ffn(x2, ya, yb, w_out, g, w_gu, w_down, tm=512, th=256):
    n, d = x2.shape
    half = ya.shape[1]
    hidden = w_down.shape[0]
    assert n % tm == 0 and hidden % th == 0
    const = lambda shape: _resident(shape, lambda i: (0, 0))
    return pl.pallas_call(
        functools.partial(_out_ffn_kernel, hidden=hidden, th=th),
        grid=(n // tm,),
        in_specs=[pl.BlockSpec((tm, d), lambda i: (i, 0)),
                  pl.BlockSpec((tm, half), lambda i: (i, 0)), pl.BlockSpec((tm, half), lambda i: (i, 0)),
                  const(w_out.shape), const((1, d)), const(w_gu.shape), const(w_down.shape)],
        out_specs=pl.BlockSpec((tm, d), lambda i: (i, 0)),
        out_shape=jax.ShapeDtypeStruct((n, d), F32),
        compiler_params=_params("parallel"),
        name="out_ffn",
    )(x2, ya, yb, w_out, g.reshape(1, d), w_gu, w_down)


def _rope_tables(s, width):
    inv = ROPE_THETA ** (-jnp.arange(0, HEAD_DIM, 2, dtype=F32) / HEAD_DIM)
    ang = jnp.arange(s, dtype=F32)[:, None] * inv[None, :]
    cos, sin = jnp.cos(ang), jnp.sin(ang)
    reps = width // HEAD_DIM
    return (jnp.tile(jnp.concatenate([cos, cos], axis=-1), (1, reps)),
            jnp.tile(jnp.concatenate([-sin, sin], axis=-1), (1, reps)))


def _block_diag(blocks):
    n, d, _ = blocks.shape
    eye = jnp.eye(n, dtype=blocks.dtype)
    return jnp.einsum('nde,nm->ndme', blocks, eye).reshape(n * d, n * d)


def _pad_cols(w, total):
    return jnp.pad(w, ((0, 0), (0, total - w.shape[1])))


def kernel(x, norm_mix, norm_ffn,
           ev_w_in, ev_conv_w, ev_conv_b, ev_w_r, ev_b_r, ev_w_i, ev_b_i, ev_lam,
           ev_q_norm, ev_k_norm, ev_w_out,
           od_w_in, od_b_f, od_q_norm, od_k_norm, od_conv_w, od_conv_b, od_ln_g, od_ln_b,
           od_w_out, ffn_w_gu, ffn_w_down):
    b, s, d = x.shape
    depth = norm_mix.shape[0]
    w = N_HEADS * HEAD_DIM
    cos, sin = _rope_tables(s, w)
    head_ones = _block_diag(jnp.ones((N_HEADS, HEAD_DIM, HEAD_DIM), BF16))
    tile_gain = lambda g: jnp.tile(g, N_HEADS).reshape(1, w)
    x2 = x.reshape(b * s, d)
    seq = lambda t: t.reshape(b, s, t.shape[-1])
    for l in range(depth):
        j = l // 2
        if l % 2 == 0:
            w_in = _pad_cols(ev_w_in[j], 2 * w + 4 * w + LANES).astype(BF16)
            xg, q, k, v, qi, kw = _norm_matmul(x2, norm_mix[l], w_in, (2 * w, w, w, w, w, LANES))
            ya = _lru(seq(xg), ev_conv_w[j], ev_conv_b[j],
                      _block_diag(ev_w_r[j]).astype(BF16), ev_b_r[j],
                      _block_diag(ev_w_i[j]).astype(BF16), ev_b_i[j], ev_lam[j])
            qh, kt, vh, qih, kit = _prep_even(seq(q), seq(k), seq(v), seq(qi), seq(kw), cos, sin,
                                              tile_gain(ev_q_norm[j]), tile_gain(ev_k_norm[j]), head_ones)
            yb = _dsa(qih, seq(kw), kit, qh, kt, vh)
            w_out = ev_w_out[j]
        else:
            wi = od_w_in[j]
            w_in = jnp.concatenate([wi[:, 0:3 * w], wi[:, 3 * w + N_HEADS:],
                                    _pad_cols(wi[:, 3 * w:3 * w + N_HEADS], LANES)], axis=1).astype(BF16)
            q, k, v, u, fl = _norm_matmul(x2, norm_mix[l], w_in, (w, w, w, 2 * w, LANES))
            qh, kt, vh = _prep_odd(seq(q), seq(k), seq(v), seq(fl),
                                   _pad_cols(od_b_f[j].reshape(1, N_HEADS), LANES),
                                   tile_gain(od_q_norm[j]), tile_gain(od_k_norm[j]), head_ones)
            qk_bound = (HEAD_DIM * QK_SCALE_LOG2 * 1.01 * jnp.max(jnp.abs(od_q_norm[j]))
                        * jnp.max(jnp.abs(od_k_norm[j])) + 1.0).reshape(1).astype(F32)
            ya = _fox(qk_bound, qh, kt, vh, tq=min(512, s))
            yb = _convmod(seq(u), od_conv_w[j], od_conv_b[j], od_ln_g[j], od_ln_b[j])
            w_out = od_w_out[j]
        x2 = _out_ffn(x2, ya.reshape(b * s, w), yb.reshape(b * s, w), w_out.astype(BF16),
                      norm_ffn[l], ffn_w_gu[l].astype(BF16), ffn_w_down[l].astype(BF16))
    return x2.reshape(b, s, d)
```

```python
import functools

import jax
import jax.numpy as jnp
from jax import lax
from jax.experimental import pallas as pl
from jax.experimental.pallas import tpu as pltpu

F32 = jnp.float32
BF16 = jnp.bfloat16

HEAD_DIM = 64
N_HEADS = 8
N_PAIRS = N_HEADS // 2
LANES = 128
SUBLANES = 8
CHUNK = 64
CHUNK_SHIFT = 6
TOPK_MAX = 256
ROPE_THETA = 10000.0
NORM_EPS = 1e-6
LRU_C = 8.0
LRU_CONV = 4
CONV_KERNEL = 31
MASKED = -2.0 ** 100
UNDERFLOW_BITS = 160.0
LOG2E = 1.4426950408889634
QK_SCALE_LOG2 = HEAD_DIM ** -0.5 * LOG2E
VMEM_LIMIT = 56 * 1024 * 1024

assert CHUNK == 1 << CHUNK_SHIFT


def _params(*sem):
    return pltpu.CompilerParams(dimension_semantics=sem, vmem_limit_bytes=VMEM_LIMIT)


def _resident(shape, index_map):
    return pl.BlockSpec(shape, index_map, pipeline_mode=pl.Buffered(1))


def _norm_matmul_kernel(x_ref, g_ref, w_ref, *o_refs, widths):
    x = x_ref[...]
    ms = jnp.mean(x * x, axis=-1, keepdims=True)
    h = (x * lax.rsqrt(ms + NORM_EPS) * g_ref[...]).astype(BF16)
    off = 0
    for o_ref, wd in zip(o_refs, widths):
        o_ref[...] = jnp.dot(h, w_ref[:, off:off + wd], preferred_element_type=F32)
        off += wd


def _norm_matmul(x2, g, w, widths, tm=512):
    n, d = x2.shape
    assert n % tm == 0 and sum(widths) == w.shape[1]
    return pl.pallas_call(
        functools.partial(_norm_matmul_kernel, widths=widths),
        grid=(n // tm,),
        in_specs=[pl.BlockSpec((tm, d), lambda i: (i, 0)),
                  _resident((1, d), lambda i: (0, 0)),
                  _resident(w.shape, lambda i: (0, 0))],
        out_specs=[pl.BlockSpec((tm, wd), lambda i: (i, 0)) for wd in widths],
        out_shape=[jax.ShapeDtypeStruct((n, wd), F32) for wd in widths],
        compiler_params=_params("parallel"),
        name="norm_matmul",
    )(x2, g.reshape(1, d), w)


def _shift_rows(x, d, fill):
    row = lax.broadcasted_iota(jnp.int32, x.shape, 0)
    return jnp.where(row >= d, pltpu.roll(x, d, 0), fill)


def _linear_scan_rows(a, u):
    d = 1
    while d < a.shape[0]:
        u = a * _shift_rows(u, d, 0.0) + u
        a = a * _shift_rows(a, d, 1.0)
        d *= 2
    return a, u


def _cumsum_rows(x):
    d = 1
    while d < x.shape[0]:
        x = x + _shift_rows(x, d, 0.0)
        d *= 2
    return x


def _softplus(x):
    return jnp.maximum(x, 0.0) + jnp.log1p(jnp.exp(-jnp.abs(x)))


def _lru_kernel(xg_ref, cw_ref, cb_ref, wr_ref, br_ref, wi_ref, bi_ref, lam_ref, o_ref,
                xbuf_ref, h_ref, *, ts, width):
    @pl.when(pl.program_id(1) == 0)
    def _():
        xbuf_ref[0:SUBLANES, :] = jnp.zeros((SUBLANES, width), F32)
        h_ref[...] = jnp.zeros_like(h_ref)

    xa = xg_ref[0, :, 0:width]
    ga = xg_ref[0, :, width:2 * width]
    xbuf_ref[SUBLANES:SUBLANES + ts, :] = xa
    base = SUBLANES - (LRU_CONV - 1)
    xc = cb_ref[...] + cw_ref[0:1, :] * xbuf_ref[pl.ds(base, ts), :]
    for j in range(1, LRU_CONV):
        xc = xc + cw_ref[j:j + 1, :] * xbuf_ref[pl.ds(base + j, ts), :]
    xbuf_ref[0:SUBLANES, :] = xa[ts - SUBLANES:ts, :]

    xb = xc.astype(BF16)
    r = jax.nn.sigmoid(jnp.dot(xb, wr_ref[...], preferred_element_type=F32) + br_ref[...])
    gate = jax.nn.sigmoid(jnp.dot(xb, wi_ref[...], preferred_element_type=F32) + bi_ref[...])
    log_a = -LRU_C * r * _softplus(-lam_ref[...])
    a = jnp.exp(log_a)
    th = jnp.tanh(log_a)
    u = jnp.sqrt(-2.0 * th / (1.0 - th)) * (gate * xc)
    a_cum, h = _linear_scan_rows(a, u)
    h = h + a_cum * h_ref[...]
    h_ref[...] = h[ts - 1:ts, :]
    o_ref[0] = (h * jax.nn.gelu(ga, approximate=True)).astype(o_ref.dtype)


def _lru(xg, conv_w, conv_b, w_r, b_r, w_i, b_i, lam, ts=256):
    b, s, w2 = xg.shape
    width = w2 // 2
    assert s % ts == 0
    row = lambda v: v.reshape(1, width)
    const = lambda shape: _resident(shape, lambda bi, si: (0, 0))
    return pl.pallas_call(
        functools.partial(_lru_kernel, ts=ts, width=width),
        grid=(b, s // ts),
        in_specs=[pl.BlockSpec((1, ts, w2), lambda bi, si: (bi, si, 0)),
                  const((LRU_CONV, width)), const((1, width)),
                  const((width, width)), const((1, width)),
                  const((width, width)), const((1, width)), const((1, width))],
        out_specs=pl.BlockSpec((1, ts, width), lambda bi, si: (bi, si, 0)),
        out_shape=jax.ShapeDtypeStruct((b, s, width), BF16),
        scratch_shapes=[pltpu.VMEM((ts + SUBLANES, width), F32), pltpu.VMEM((1, width), F32)],
        compiler_params=_params("parallel", "arbitrary"),
        name="rg_lru",
    )(xg, conv_w, row(conv_b), w_r, row(b_r), w_i, row(b_i), row(lam))


def _head_mean_square(x, bd_ref):
    x2 = x * x
    hi = x2.astype(BF16)
    r1 = x2 - hi.astype(F32)
    mid = r1.astype(BF16)
    lo = (r1 - mid.astype(F32)).astype(BF16)
    bd = bd_ref[...]
    tot = (jnp.dot(hi, bd, preferred_element_type=F32) + jnp.dot(mid, bd, preferred_element_type=F32)
           + jnp.dot(lo, bd, preferred_element_type=F32))
    return tot * (1.0 / HEAD_DIM)


def _head_rms_norm(x, g, bd_ref):
    return x * lax.rsqrt(_head_mean_square(x, bd_ref) + NORM_EPS) * g


def _rope(x, cos, sin_signed):
    n = x.shape[1]
    lane = lax.broadcasted_iota(jnp.int32, x.shape, 1)
    first_half = (lane & (HEAD_DIM - 1)) < HEAD_DIM // 2
    partner = jnp.where(first_half, pltpu.roll(x, n - HEAD_DIM // 2, 1), pltpu.roll(x, HEAD_DIM // 2, 1))
    return x * cos + partner * sin_signed


def _prep_even_kernel(q_ref, k_ref, v_ref, qi_ref, kw_ref, cos_ref, sin_ref, gq_ref, gk_ref, bd_ref,
                      qo_ref, kto_ref, vo_ref, qio_ref, kito_ref):
    cos = cos_ref[...]
    sin = sin_ref[...]
    q = _rope(_head_rms_norm(q_ref[0], gq_ref[...], bd_ref), cos, sin)
    qo_ref[0] = (q * QK_SCALE_LOG2).astype(BF16)
    k = _rope(_head_rms_norm(k_ref[0], gk_ref[...], bd_ref), cos, sin)
    kto_ref[0] = k.T.astype(BF16)
    vo_ref[0] = v_ref[0].astype(BF16)
    qio_ref[0] = _rope(qi_ref[0], cos, sin).astype(BF16)
    ki = _rope(kw_ref[0], cos[:, 0:LANES], sin[:, 0:LANES])
    lane = lax.broadcasted_iota(jnp.int32, ki.shape, 1)
    ki2 = jnp.where(lane < HEAD_DIM, ki, pltpu.roll(ki, HEAD_DIM, 1))
    kito_ref[0] = ki2.T.astype(BF16)


def _prep_even(q, k, v, qi, kw, cos, sin, gq, gk, bd, ts=512):
    b, s, w = q.shape
    assert s % ts == 0
    tile = pl.BlockSpec((1, ts, w), lambda bi, si: (bi, si, 0))
    tab = pl.BlockSpec((ts, w), lambda bi, si: (si, 0))
    const = lambda shape: _resident(shape, lambda bi, si: (0, 0))
    return pl.pallas_call(
        _prep_even_kernel,
        grid=(b, s // ts),
        in_specs=[tile, tile, tile, tile, pl.BlockSpec((1, ts, LANES), lambda bi, si: (bi, si, 0)),
                  tab, tab, const((1, w)), const((1, w)), const((w, w))],
        out_specs=[tile, pl.BlockSpec((1, w, ts), lambda bi, si: (bi, 0, si)), tile, tile,
                   pl.BlockSpec((1, LANES, ts), lambda bi, si: (bi, 0, si))],
        out_shape=[jax.ShapeDtypeStruct((b, s, w), BF16), jax.ShapeDtypeStruct((b, w, s), BF16),
                   jax.ShapeDtypeStruct((b, s, w), BF16), jax.ShapeDtypeStruct((b, s, w), BF16),
                   jax.ShapeDtypeStruct((b, LANES, s), BF16)],
        compiler_params=_params("parallel", "parallel"),
        name="prep_even",
    )(q, k, v, qi, kw, cos, sin, gq, gk, bd)


def _prep_odd_kernel(q_ref, k_ref, v_ref, fl_ref, bf_ref, gq_ref, gk_ref, bd_ref,
                     qo_ref, kto_ref, vo_ref, carry_ref):
    @pl.when(pl.program_id(1) == 0)
    def _():
        carry_ref[...] = jnp.zeros_like(carry_ref)

    q = _head_rms_norm(q_ref[0], gq_ref[...], bd_ref) * QK_SCALE_LOG2
    k = _head_rms_norm(k_ref[0], gk_ref[...], bd_ref)
    v = v_ref[0]
    log_f = -_softplus(-(fl_ref[0] + bf_ref[...]))
    c = _cumsum_rows(log_f) + carry_ref[...]
    carry_ref[...] = c[c.shape[0] - 1:, :]
    c2 = c * LOG2E
    lane = lax.broadcasted_iota(jnp.int32, (c.shape[0], LANES), 1)
    for h in range(N_HEADS):
        p, e = divmod(h, 2)
        pick = lambda t: t[:, p * LANES:(p + 1) * LANES] if e == 0 else pltpu.roll(t[:, p * LANES:(p + 1) * LANES], HEAD_DIM, 1)
        cb = jnp.broadcast_to(c2[:, h:h + 1], (c.shape[0], LANES))
        hi = cb.astype(BF16).astype(F32)
        mid = (cb - hi).astype(BF16).astype(F32)
        lo = cb - hi - mid
        q_bias = jnp.where(lane == HEAD_DIM, hi, jnp.where(lane == HEAD_DIM + 1, mid, jnp.where(
            lane == HEAD_DIM + 2, lo, jnp.where(lane < HEAD_DIM + 6, 1.0, 0.0))))
        k_bias = jnp.where(lane < HEAD_DIM + 3, 1.0, jnp.where(lane == HEAD_DIM + 3, -hi, jnp.where(
            lane == HEAD_DIM + 4, -mid, jnp.where(lane == HEAD_DIM + 5, -lo, 0.0))))
        qo_ref[0, h] = jnp.where(lane < HEAD_DIM, pick(q), q_bias).astype(BF16)
        kto_ref[0, h] = jnp.where(lane < HEAD_DIM, pick(k), k_bias).T.astype(BF16)
        vo_ref[0, h] = jnp.where(lane < HEAD_DIM, pick(v), 1.0).astype(BF16)


def _prep_odd(q, k, v, fl, bf, gq, gk, bd, ts=512):
    b, s, w = q.shape
    assert s % ts == 0
    tile = pl.BlockSpec((1, ts, w), lambda bi, si: (bi, si, 0))
    narrow = pl.BlockSpec((1, ts, LANES), lambda bi, si: (bi, si, 0))
    heads = pl.BlockSpec((1, N_HEADS, ts, LANES), lambda bi, si: (bi, 0, si, 0))
    const = lambda shape: _resident(shape, lambda bi, si: (0, 0))
    return pl.pallas_call(
        _prep_odd_kernel,
        grid=(b, s // ts),
        in_specs=[tile, tile, tile, narrow, const((1, LANES)), const((1, w)), const((1, w)), const((w, w))],
        out_specs=[heads, pl.BlockSpec((1, N_HEADS, LANES, ts), lambda bi, si: (bi, 0, 0, si)), heads],
        out_shape=[jax.ShapeDtypeStruct((b, N_HEADS, s, LANES), BF16),
                   jax.ShapeDtypeStruct((b, N_HEADS, LANES, s), BF16),
                   jax.ShapeDtypeStruct((b, N_HEADS, s, LANES), BF16)],
        scratch_shapes=[pltpu.VMEM((1, LANES), F32)],
        compiler_params=_params("parallel", "arbitrary"),
        name="prep_odd",
    )(q, k, v, fl, bf, gq, gk, bd)


def _head_halves(x_pair):
    lane = lax.broadcasted_iota(jnp.int32, x_pair.shape, 1)
    zero = jnp.zeros_like(x_pair)
    return jnp.where(lane < HEAD_DIM, x_pair, zero), jnp.where(lane >= HEAD_DIM, x_pair, zero)


def _fold_lanes(x, op):
    out = x[:, 0:LANES]
    for c in range(1, x.shape[1] // LANES):
        out = op(out, x[:, c * LANES:(c + 1) * LANES])
    return out


def _online_softmax_block(scores, values, m_ref, acc_ref):
    probs, alphas = [], []
    for h, s in enumerate(scores):
        m_old = m_ref[h]
        m_new = jnp.maximum(m_old, jnp.max(_fold_lanes(s, jnp.maximum), axis=1, keepdims=True).astype(F32))
        p = jnp.exp2(s - jnp.tile(m_new.astype(s.dtype), (1, s.shape[1] // LANES)))
        m_ref[h] = m_new
        probs.append(p.astype(BF16))
        alphas.append(jnp.exp2(m_old - m_new))
    for h, (p, alpha) in enumerate(zip(probs, alphas)):
        acc_ref[h] = alpha * acc_ref[h] + jnp.dot(p, values[h], preferred_element_type=F32)


def _to_ordinal(x):
    bits = lax.bitcast_convert_type(x, jnp.int32)
    return jnp.where(bits < 0, bits ^ jnp.int32(0x7FFFFFFF), bits)


def _from_ordinal(o):
    return lax.bitcast_convert_type(jnp.where(o < 0, o ^ jnp.int32(0x7FFFFFFF), o), F32)


def _ordinal_midpoint(lo, hi):
    a, b = _to_ordinal(lo), _to_ordinal(hi)
    return _from_ordinal((a >> 1) + (b >> 1) + (a & b & 1))


def _next_float_up(x):
    return _from_ordinal(_to_ordinal(x) + 1)


def _normalise(acc):
    return acc / pltpu.roll(acc, HEAD_DIM, 1)


def _dsa_index_kernel(qi_ref, kw_ref, kit_ref, mask_ref, sc_ref, lo_ref, hi_ref, *, tq, tk, topk, max_iters):
    t0 = pl.program_id(1) * tq
    nb = (t0 + tq + tk - 1) // tk
    row_chunk = (t0 + lax.broadcasted_iota(jnp.int32, (tq, 1), 0)) >> CHUNK_SHIFT
    kf = float(topk)

    def cols(j):
        return pl.ds(pl.multiple_of(j * tk, tk), tk)

    qi_heads = []
    for p in range(N_PAIRS):
        qi_heads.extend(_head_halves(qi_ref[0, :, p * LANES:(p + 1) * LANES]))
    kw = kw_ref[0]

    def score_block(j, carry):
        kit = kit_ref[0, :, cols(j)]
        score = jnp.zeros((tq, tk), F32)
        for h in range(N_HEADS):
            rel = jnp.maximum(jnp.dot(qi_heads[h], kit, preferred_element_type=F32), 0.0)
            score = score + kw[:, HEAD_DIM + h:HEAD_DIM + h + 1] * rel
        key_chunk = (j * tk + lax.broadcasted_iota(jnp.int32, (1, tk), 1)) >> CHUNK_SHIFT
        visible = key_chunk <= row_chunk
        sc_ref[:, cols(j)] = jnp.where(visible, score, -jnp.inf)
        lo_ref[...] = jnp.minimum(lo_ref[...], _fold_lanes(jnp.where(visible, score, jnp.inf), jnp.minimum))
        hi_ref[...] = jnp.maximum(hi_ref[...], _fold_lanes(jnp.where(visible, score, -jnp.inf), jnp.maximum))
        return carry

    lo_ref[...] = jnp.full_like(lo_ref, jnp.inf)
    hi_ref[...] = jnp.full_like(hi_ref, -jnp.inf)
    lax.fori_loop(0, nb, score_block, 0)

    reps = tk // LANES

    def to_col(row):
        return jnp.broadcast_to(row, (LANES, tq)).T

    def to_row(wide, reduce):
        return reduce(wide.T, axis=0, keepdims=True)

    def count(row, strict=False, wanted=None):
        col = to_col(row)
        parts = []
        for g, r in enumerate(range(0, tq, LANES)):
            bound = jnp.tile(col[r:r + LANES], (1, reps))

            def body(j, cnt):
                blk = sc_ref[r:r + LANES, cols(j)]
                hit = blk > bound if strict else blk >= bound
                return cnt + _fold_lanes(jnp.where(hit, 1.0, 0.0), jnp.add)

            blocks = nb if wanted is None else jnp.where(wanted[g] > 0.0, nb, 0)
            parts.append(lax.fori_loop(0, blocks, body, jnp.zeros((LANES, LANES), F32)))
        return to_row(jnp.concatenate(parts, axis=0), jnp.sum)

    lo0 = to_row(lo_ref[...], jnp.min)
    hi0 = _next_float_up(to_row(hi_ref[...], jnp.max))
    n_visible = (((t0 + lax.broadcasted_iota(jnp.int32, (1, tq), 1)) >> CHUNK_SHIFT) + 1) * CHUNK
    short_row = n_visible <= topk
    log_k = jnp.log(kf + 0.5)

    def open_groups(lo, hi, c_lo):
        finished = jnp.logical_or(jnp.logical_or(c_lo == kf, short_row),
                                  _to_ordinal(_ordinal_midpoint(lo, hi)) == _to_ordinal(lo))
        still = jnp.where(finished, 0.0, 1.0)
        return tuple(jnp.max(still[:, r:r + LANES]) for r in range(0, tq, LANES))

    def search_cond(state):
        return jnp.logical_and(state[0] < max_iters, functools.reduce(jnp.maximum, state[-1]) > 0.0)

    def search_body(state):
        it, lo, hi, c_lo, f_lo, f_hi, side, wanted = state
        guess = lo + (hi - lo) * jnp.clip(f_lo / (f_lo - f_hi), 1.0 / 64, 63.0 / 64)
        inside = jnp.logical_and(guess > lo, guess < hi)
        half = _ordinal_midpoint(lo, hi)
        mid = jnp.where(it % 3 == 2, half, jnp.where(inside, guess, half))
        c_mid = count(mid, wanted=wanted)
        counted = jnp.concatenate([jnp.zeros((1, LANES), F32) + w for w in wanted], axis=1) > 0.0
        enough = jnp.logical_and(counted, c_mid >= kf)
        short = jnp.logical_and(counted, c_mid < kf)
        f_mid = jnp.log(jnp.maximum(c_mid, 0.5)) - log_k
        f_hi = jnp.where(enough, jnp.where(side > 0.0, f_hi * 0.5, f_hi), jnp.where(short, f_mid, f_hi))
        f_lo = jnp.where(enough, f_mid, jnp.where(jnp.logical_and(short, side < 0.0), f_lo * 0.5, f_lo))
        lo, c_lo = jnp.where(enough, mid, lo), jnp.where(enough, c_mid, c_lo)
        hi = jnp.where(short, mid, hi)
        side = jnp.where(enough, 1.0, jnp.where(short, -1.0, side))
        return it + 1, lo, hi, c_lo, f_lo, f_hi, side, open_groups(lo, hi, c_lo)

    c_lo0 = n_visible.astype(F32)
    state0 = (jnp.int32(0), lo0, hi0, c_lo0, jnp.log(c_lo0) - log_k, jnp.log(0.5) - log_k + jnp.zeros_like(lo0),
              jnp.zeros_like(lo0), open_groups(lo0, hi0, c_lo0))
    _, lo, _, n_ge, _, _, _, _ = lax.while_loop(search_cond, search_body, state0)
    thr_row = jnp.where(short_row, jnp.finfo(F32).min, lo)
    thr = jnp.tile(to_col(thr_row), (1, reps))

    has_extra = jnp.max(jnp.where(jnp.logical_and(n_ge > kf, jnp.logical_not(short_row)), 1.0, 0.0))

    @pl.when(has_extra > 0.0)
    def _():
        keep = to_col(kf - count(thr_row, strict=True))[:, 0:1]
        upper = (lax.broadcasted_iota(jnp.int32, (tk, tk), 0)
                 <= lax.broadcasted_iota(jnp.int32, (tk, tk), 1))
        prefix = jnp.where(upper, 1.0, 0.0).astype(BF16)

        def body(j, seen):
            blk = sc_ref[:, cols(j)]
            tie = jnp.where(blk == thr, 1.0, 0.0)
            rank = seen + jnp.dot(tie.astype(BF16), prefix, preferred_element_type=F32)
            drop = jnp.where(rank > keep, tie, 0.0)
            sc_ref[:, cols(j)] = jnp.where(drop > 0.0, -jnp.inf, blk)
            return seen + jnp.sum(tie, axis=1, keepdims=True)

        lax.fori_loop(0, nb, body, jnp.zeros((tq, 1), F32))

    def mask_block(j, carry):
        mask_ref[0, :, cols(j)] = jnp.where(sc_ref[:, cols(j)] >= thr, 1, 0).astype(mask_ref.dtype)
        return carry

    def clear_block(j, carry):
        mask_ref[0, :, cols(j)] = jnp.zeros((tq, tk), mask_ref.dtype)
        return carry

    lax.fori_loop(0, nb, mask_block, 0)
    lax.fori_loop(nb, mask_ref.shape[2] // tk, clear_block, 0)


def _dsa_index(qi, kw, kit, tq, tk):
    b, s, w = qi.shape
    assert s % tk == 0 and s % tq == 0 and tq % CHUNK == 0
    topk = min(TOPK_MAX, s // 4)
    return pl.pallas_call(
        functools.partial(_dsa_index_kernel, tq=tq, tk=tk, topk=topk, max_iters=640),
        grid=(b, s // tq),
        in_specs=[pl.BlockSpec((1, tq, w), lambda bi, i: (bi, i, 0)),
                  pl.BlockSpec((1, tq, LANES), lambda bi, i: (bi, i, 0)),
                  _resident((1, LANES, s), lambda bi, i: (bi, 0, 0))],
        out_specs=_resident((1, tq, s), lambda bi, i: (bi, i, 0)),
        out_shape=jax.ShapeDtypeStruct((b, s, s), jnp.int8),
        scratch_shapes=[pltpu.VMEM((tq, s), F32), pltpu.VMEM((tq, LANES), F32), pltpu.VMEM((tq, LANES), F32)],
        compiler_params=_params("parallel", "arbitrary"),
        name="dsa_index",
    )(qi, kw, kit)


def _dsa_attend_kernel(q_ref, kt_ref, v_ref, mask_ref, o_ref, m_ref, acc_ref, *, tq, tk):
    nb = (pl.program_id(1) * tq + tq + tk - 1) // tk
    m_ref[...] = jnp.full_like(m_ref, MASKED)
    acc_ref[...] = jnp.zeros_like(acc_ref)
    q_heads = []
    for p in range(N_PAIRS):
        q_heads.extend(_head_halves(q_ref[0, :, p * LANES:(p + 1) * LANES]))
    first_half = lax.broadcasted_iota(jnp.int32, (tk, LANES), 1) < HEAD_DIM
    one = jnp.ones((), BF16)

    def attend_block(j, carry):
        cols = pl.ds(pl.multiple_of(j * tk, tk), tk)
        bias = jnp.where(mask_ref[0, :, cols].astype(jnp.int32) > 0, 0.0, MASKED).astype(BF16)
        scores, values = [], []
        for h in range(N_HEADS):
            p, e = divmod(h, 2)
            kt = kt_ref[0, p * LANES:(p + 1) * LANES, cols]
            scores.append(jnp.dot(q_heads[h], kt, preferred_element_type=F32).astype(BF16) + bias)
            v_pair = v_ref[0, cols, p * LANES:(p + 1) * LANES]
            values.append(jnp.where(first_half, v_pair, one) if e == 0 else jnp.where(first_half, one, v_pair))
        _online_softmax_block(scores, values, m_ref, acc_ref)
        return carry

    lax.fori_loop(0, nb, attend_block, 0)
    lane = lax.broadcasted_iota(jnp.int32, (tq, LANES), 1)
    for p in range(N_PAIRS):
        o_ref[0, :, p * LANES:(p + 1) * LANES] = jnp.where(
            lane < HEAD_DIM, _normalise(acc_ref[2 * p]), _normalise(acc_ref[2 * p + 1])).astype(o_ref.dtype)


def _dsa_attend(q, kt, v, mask, tq, tk):
    b, s, w = q.shape
    assert s % tk == 0 and s % tq == 0
    qtile = pl.BlockSpec((1, tq, w), lambda bi, i: (bi, i, 0))
    return pl.pallas_call(
        functools.partial(_dsa_attend_kernel, tq=tq, tk=tk),
        grid=(b, s // tq),
        in_specs=[qtile,
                  _resident((1, w, s), lambda bi, i: (bi, 0, 0)),
                  _resident((1, s, w), lambda bi, i: (bi, 0, 0)),
                  pl.BlockSpec((1, tq, s), lambda bi, i: (bi, i, 0))],
        out_specs=qtile,
        out_shape=jax.ShapeDtypeStruct((b, s, w), BF16),
        scratch_shapes=[pltpu.VMEM((N_HEADS, tq, LANES), F32), pltpu.VMEM((N_HEADS, tq, LANES), F32)],
        compiler_params=_params("parallel", "arbitrary"),
        name="dsa_attend",
    )(q, kt, v, mask)


def _dsa(qi, kw, kit, q, kt, v, tq_index=512, tq_attend=256, tk=512):
    s = q.shape[1]
    mask = _dsa_index(qi, kw, kit, min(tq_index, s), min(tk, s))
    return _dsa_attend(q, kt, v, mask, min(tq_attend, s), min(tk, s))


def _fox_kernel(bound_ref, q_ref, kt_ref, v_ref, o_ref, m_ref, acc_ref, ckey_ref, *, tq, heads):
    i = pl.program_id(2)
    m_ref[...] = jnp.full_like(m_ref, MASKED)
    acc_ref[...] = jnp.zeros_like(acc_ref)
    causal = (lax.broadcasted_iota(jnp.int32, (tq, tq), 1) <= lax.broadcasted_iota(jnp.int32, (tq, tq), 0))

    @pl.when(i == 0)
    def _():
        for e in range(heads):
            tail = kt_ref[0, e, HEAD_DIM:HEAD_DIM + 16, :].astype(F32)
            row = lax.broadcasted_iota(jnp.int32, tail.shape, 0)
            ckey_ref[e] = -jnp.sum(jnp.where(jnp.logical_and(row >= 3, row < 6), tail, 0.0), axis=0, keepdims=True)

    pos = lax.broadcasted_iota(jnp.int32, (1, ckey_ref.shape[2]), 1)
    block_end = jnp.logical_and((pos & (tq - 1)) == tq - 1, pos < i * tq)
    lane = lax.broadcasted_iota(jnp.int32, (tq, LANES), 1)
    first = None
    for e in range(heads):
        c_first = ckey_ref[e, :, pl.ds(pl.multiple_of(i * tq, tq), LANES)][:, 0:1]
        dead = jnp.logical_and(block_end, ckey_ref[e] - c_first > 2.0 * bound_ref[0] + UNDERFLOW_BITS)
        n_dead = jnp.sum(jnp.where(dead, 1, 0))
        first = n_dead if first is None else jnp.minimum(first, n_dead)

    def block(j, diagonal):
        cols = pl.ds(pl.multiple_of(j * tq, tq), tq)
        scores = []
        for e in range(heads):
            s = jnp.dot(q_ref[0, e], kt_ref[0, e, :, cols], preferred_element_type=F32)
            scores.append(jnp.where(causal, s, MASKED) if diagonal else s)
        _online_softmax_block(scores, [v_ref[0, e, cols, :] for e in range(heads)], m_ref, acc_ref)

    def body(j, carry):
        block(j, False)
        return carry

    lax.fori_loop(first, i, body, 0)
    block(i, True)
    for p in range(heads // 2):
        a0, a1 = acc_ref[2 * p], acc_ref[2 * p + 1]
        o_ref[0, :, p * LANES:(p + 1) * LANES] = jnp.where(
            lane < HEAD_DIM, a0 / pltpu.roll(a0, HEAD_DIM, 1), pltpu.roll(a1, HEAD_DIM, 1) / a1).astype(o_ref.dtype)


def _fox(qk_bound, q, kt, v, tq=512, heads=4):
    b, nh, s, _ = q.shape
    assert s % tq == 0 and tq & (tq - 1) == 0 and nh == N_HEADS and nh % heads == 0 and heads % 2 == 0
    out_w = heads * HEAD_DIM
    return pl.pallas_call(
        functools.partial(_fox_kernel, tq=tq, heads=heads),
        grid=(b, nh // heads, s // tq),
        in_specs=[pl.BlockSpec(memory_space=pltpu.SMEM),
                  pl.BlockSpec((1, heads, tq, LANES), lambda bi, g, i: (bi, g, i, 0)),
                  _resident((1, heads, LANES, s), lambda bi, g, i: (bi, g, 0, 0)),
                  _resident((1, heads, s, LANES), lambda bi, g, i: (bi, g, 0, 0))],
        out_specs=pl.BlockSpec((1, tq, out_w), lambda bi, g, i: (bi, i, g)),
        out_shape=jax.ShapeDtypeStruct((b, s, N_HEADS * HEAD_DIM), BF16),
        scratch_shapes=[pltpu.VMEM((heads, tq, LANES), F32), pltpu.VMEM((heads, tq, LANES), F32),
                        pltpu.VMEM((heads, 1, s), F32)],
        compiler_params=_params("parallel", "parallel", "arbitrary"),
        name="fox",
    )(qk_bound, q, kt, v)


def _convmod_kernel(u_ref, cw_ref, cb_ref, g_ref, b_ref, o_ref, xbuf_ref, *, ts, width, halo):
    @pl.when(pl.program_id(1) == 0)
    def _():
        xbuf_ref[0:halo, :] = jnp.zeros((halo, width), F32)

    x = u_ref[0, :, 0:width] * jax.nn.sigmoid(u_ref[0, :, width:2 * width])
    xbuf_ref[halo:halo + ts, :] = x
    base = halo - (CONV_KERNEL - 1)
    y = cb_ref[...] + cw_ref[0:1, :] * xbuf_ref[pl.ds(base, ts), :]
    for j in range(1, CONV_KERNEL):
        y = y + cw_ref[j:j + 1, :] * xbuf_ref[pl.ds(base + j, ts), :]
    xbuf_ref[0:halo, :] = x[ts - halo:ts, :]
    mu = jnp.mean(y, axis=-1, keepdims=True)
    var = jnp.mean(jnp.square(y - mu), axis=-1, keepdims=True)
    z = (y - mu) * lax.rsqrt(var + NORM_EPS) * g_ref[...] + b_ref[...]
    o_ref[0] = (z * jax.nn.sigmoid(z)).astype(o_ref.dtype)


def _convmod(u, conv_w, conv_b, ln_g, ln_b, ts=512, halo=32):
    b, s, w2 = u.shape
    width = w2 // 2
    assert s % ts == 0 and halo >= CONV_KERNEL - 1
    row = lambda v: v.reshape(1, width)
    const = lambda shape: _resident(shape, lambda bi, si: (0, 0))
    return pl.pallas_call(
        functools.partial(_convmod_kernel, ts=ts, width=width, halo=halo),
        grid=(b, s // ts),
        in_specs=[pl.BlockSpec((1, ts, w2), lambda bi, si: (bi, si, 0)),
                  const((CONV_KERNEL, width)), const((1, width)), const((1, width)), const((1, width))],
        out_specs=pl.BlockSpec((1, ts, width), lambda bi, si: (bi, si, 0)),
        out_shape=jax.ShapeDtypeStruct((b, s, width), BF16),
        scratch_shapes=[pltpu.VMEM((ts + halo, width), F32)],
        compiler_params=_params("parallel", "arbitrary"),
        name="conv_module",
    )(u, conv_w, row(conv_b), row(ln_g), row(ln_b))


def _out_ffn_kernel(x_ref, ya_ref, yb_ref, wo_ref, g_ref, wgu_ref, wd_ref, o_ref, *, hidden, th):
    half = ya_ref.shape[1]
    x = x_ref[...] + (jnp.dot(ya_ref[...], wo_ref[0:half, :], preferred_element_type=F32)
                      + jnp.dot(yb_ref[...], wo_ref[half:2 * half, :], preferred_element_type=F32))
    ms = jnp.mean(x * x, axis=-1, keepdims=True)
    h = (x * lax.rsqrt(ms + NORM_EPS) * g_ref[...]).astype(BF16)
    y = x
    for c in range(0, hidden, th):
        gate = jnp.dot(h, wgu_ref[:, c:c + th], preferred_element_type=F32)
        up = jnp.dot(h, wgu_ref[:, hidden + c:hidden + c + th], preferred_element_type=F32)
        act = (gate * jax.nn.sigmoid(gate) * up).astype(BF16)
        y = y + jnp.dot(act, wd_ref[c:c + th, :], preferred_element_type=F32)
    o_ref[...] = y


def _out_ffn(x2, ya, yb, w_out, g, w_gu, w_down, tm=512, th=256):
    n, d = x2.shape
    half = ya.shape[1]
    hidden = w_down.shape[0]
    assert n % tm == 0 and hidden % th == 0
    const = lambda shape: _resident(shape, lambda i: (0, 0))
    return pl.pallas_call(
        functools.partial(_out_ffn_kernel, hidden=hidden, th=th),
        grid=(n // tm,),
        in_specs=[pl.BlockSpec((tm, d), lambda i: (i, 0)),
                  pl.BlockSpec((tm, half), lambda i: (i, 0)), pl.BlockSpec((tm, half), lambda i: (i, 0)),
                  const(w_out.shape), const((1, d)), const(w_gu.shape), const(w_down.shape)],
        out_specs=pl.BlockSpec((tm, d), lambda i: (i, 0)),
        out_shape=jax.ShapeDtypeStruct((n, d), F32),
        compiler_params=_params("parallel"),
        name="out_ffn",
    )(x2, ya, yb, w_out, g.reshape(1, d), w_gu, w_down)


def _rope_tables(s, width):
    inv = ROPE_THETA ** (-jnp.arange(0, HEAD_DIM, 2, dtype=F32) / HEAD_DIM)
    ang = jnp.arange(s, dtype=F32)[:, None] * inv[None, :]
    cos, sin = jnp.cos(ang), jnp.sin(ang)
    reps = width // HEAD_DIM
    return (jnp.tile(jnp.concatenate([cos, cos], axis=-1), (1, reps)),
            jnp.tile(jnp.concatenate([-sin, sin], axis=-1), (1, reps)))


def _block_diag(blocks):
    n, d, _ = blocks.shape
    eye = jnp.eye(n, dtype=blocks.dtype)
    return jnp.einsum('nde,nm->ndme', blocks, eye).reshape(n * d, n * d)


def _pad_cols(w, total):
    return jnp.pad(w, ((0, 0), (0, total - w.shape[1])))


def kernel(x, norm_mix, norm_ffn,
           ev_w_in, ev_conv_w, ev_conv_b, ev_w_r, ev_b_r, ev_w_i, ev_b_i, ev_lam,
           ev_q_norm, ev_k_norm, ev_w_out,
           od_w_in, od_b_f, od_q_norm, od_k_norm, od_conv_w, od_conv_b, od_ln_g, od_ln_b,
           od_w_out, ffn_w_gu, ffn_w_down):
    b, s, d = x.shape
    depth = norm_mix.shape[0]
    w = N_HEADS * HEAD_DIM
    cos, sin = _rope_tables(s, w)
    head_ones = _block_diag(jnp.ones((N_HEADS, HEAD_DIM, HEAD_DIM), BF16))
    tile_gain = lambda g: jnp.tile(g, N_HEADS).reshape(1, w)
    x2 = x.reshape(b * s, d)
    seq = lambda t: t.reshape(b, s, t.shape[-1])
    for l in range(depth):
        j = l // 2
        if l % 2 == 0:
            w_in = _pad_cols(ev_w_in[j], 2 * w + 4 * w + LANES).astype(BF16)
            xg, q, k, v, qi, kw = _norm_matmul(x2, norm_mix[l], w_in, (2 * w, w, w, w, w, LANES))
            ya = _lru(seq(xg), ev_conv_w[j], ev_conv_b[j],
                      _block_diag(ev_w_r[j]).astype(BF16), ev_b_r[j],
                      _block_diag(ev_w_i[j]).astype(BF16), ev_b_i[j], ev_lam[j])
            qh, kt, vh, qih, kit = _prep_even(seq(q), seq(k), seq(v), seq(qi), seq(kw), cos, sin,
                                              tile_gain(ev_q_norm[j]), tile_gain(ev_k_norm[j]), head_ones)
            yb = _dsa(qih, seq(kw), kit, qh, kt, vh)
            w_out = ev_w_out[j]
        else:
            wi = od_w_in[j]
            w_in = jnp.concatenate([wi[:, 0:3 * w], wi[:, 3 * w + N_HEADS:],
                                    _pad_cols(wi[:, 3 * w:3 * w + N_HEADS], LANES)], axis=1).astype(BF16)
            q, k, v, u, fl = _norm_matmul(x2, norm_mix[l], w_in, (w, w, w, 2 * w, LANES))
            qh, kt, vh = _prep_odd(seq(q), seq(k), seq(v), seq(fl),
                                   _pad_cols(od_b_f[j].reshape(1, N_HEADS), LANES),
                                   tile_gain(od_q_norm[j]), tile_gain(od_k_norm[j]), head_ones)
            qk_bound = (HEAD_DIM * QK_SCALE_LOG2 * 1.01 * jnp.max(jnp.abs(od_q_norm[j]))
                        * jnp.max(jnp.abs(od_k_norm[j])) + 1.0).reshape(1).astype(F32)
            ya = _fox(qk_bound, qh, kt, vh, tq=min(512, s))
            yb = _convmod(seq(u), od_conv_w[j], od_conv_b[j], od_ln_g[j], od_ln_b[j])
            w_out = od_w_out[j]
        x2 = _out_ffn(x2, ya.reshape(b * s, w), yb.reshape(b * s, w), w_out.astype(BF16),
                      norm_ffn[l], ffn_w_gu[l].astype(BF16), ffn_w_down[l].astype(BF16))
    return x2.reshape(b, s, d)
```

```python
import functools

import jax
import jax.numpy as jnp
from jax import lax
from jax.experimental import pallas as pl
from jax.experimental.pallas import tpu as pltpu

F32 = jnp.float32
BF16 = jnp.bfloat16

HEAD_DIM = 64
N_HEADS = 8
N_PAIRS = N_HEADS // 2
LANES = 128
SUBLANES = 8
CHUNK = 64
CHUNK_SHIFT = 6
TOPK_MAX = 256
ROPE_THETA = 10000.0
NORM_EPS = 1e-6
LRU_C = 8.0
LRU_CONV = 4
CONV_KERNEL = 31
MASKED = -2.0 ** 100
UNDERFLOW_BITS = 160.0
LOG2E = 1.4426950408889634
QK_SCALE_LOG2 = HEAD_DIM ** -0.5 * LOG2E
VMEM_LIMIT = 56 * 1024 * 1024

ROW_GROUP = 64
ROW_GROUP_SHIFT = 6

assert CHUNK == 1 << CHUNK_SHIFT and ROW_GROUP == 1 << ROW_GROUP_SHIFT


def _params(*sem):
    return pltpu.CompilerParams(dimension_semantics=sem, vmem_limit_bytes=VMEM_LIMIT)


def _resident(shape, index_map):
    return pl.BlockSpec(shape, index_map, pipeline_mode=pl.Buffered(1))


def _norm_matmul_kernel(x_ref, g_ref, w_ref, *o_refs, widths):
    x = x_ref[...]
    ms = jnp.mean(x * x, axis=-1, keepdims=True)
    h = (x * lax.rsqrt(ms + NORM_EPS) * g_ref[...]).astype(BF16)
    off = 0
    for o_ref, wd in zip(o_refs, widths):
        o_ref[...] = jnp.dot(h, w_ref[:, off:off + wd], preferred_element_type=F32)
        off += wd


def _norm_matmul(x2, g, w, widths, tm=512):
    n, d = x2.shape
    assert n % tm == 0 and sum(widths) == w.shape[1]
    return pl.pallas_call(
        functools.partial(_norm_matmul_kernel, widths=widths),
        grid=(n // tm,),
        in_specs=[pl.BlockSpec((tm, d), lambda i: (i, 0)),
                  _resident((1, d), lambda i: (0, 0)),
                  _resident(w.shape, lambda i: (0, 0))],
        out_specs=[pl.BlockSpec((tm, wd), lambda i: (i, 0)) for wd in widths],
        out_shape=[jax.ShapeDtypeStruct((n, wd), F32) for wd in widths],
        compiler_params=_params("parallel"),
        name="norm_matmul",
    )(x2, g.reshape(1, d), w)


def _shift_rows(x, d, fill):
    row = lax.broadcasted_iota(jnp.int32, x.shape, 0)
    return jnp.where(row >= d, pltpu.roll(x, d, 0), fill)


def _linear_scan_rows(a, u):
    d = 1
    while d < a.shape[0]:
        u = a * _shift_rows(u, d, 0.0) + u
        a = a * _shift_rows(a, d, 1.0)
        d *= 2
    return a, u


def _cumsum_rows(x):
    d = 1
    while d < x.shape[0]:
        x = x + _shift_rows(x, d, 0.0)
        d *= 2
    return x


def _softplus(x):
    return jnp.maximum(x, 0.0) + jnp.log1p(jnp.exp(-jnp.abs(x)))


def _lru_kernel(xg_ref, cw_ref, cb_ref, wr_ref, br_ref, wi_ref, bi_ref, lam_ref, o_ref,
                xbuf_ref, h_ref, *, ts, width):
    @pl.when(pl.program_id(1) == 0)
    def _():
        xbuf_ref[0:SUBLANES, :] = jnp.zeros((SUBLANES, width), F32)
        h_ref[...] = jnp.zeros_like(h_ref)

    xa = xg_ref[0, :, 0:width]
    ga = xg_ref[0, :, width:2 * width]
    xbuf_ref[SUBLANES:SUBLANES + ts, :] = xa
    base = SUBLANES - (LRU_CONV - 1)
    xc = cb_ref[...] + cw_ref[0:1, :] * xbuf_ref[pl.ds(base, ts), :]
    for j in range(1, LRU_CONV):
        xc = xc + cw_ref[j:j + 1, :] * xbuf_ref[pl.ds(base + j, ts), :]
    xbuf_ref[0:SUBLANES, :] = xa[ts - SUBLANES:ts, :]

    xb = xc.astype(BF16)
    r = jax.nn.sigmoid(jnp.dot(xb, wr_ref[...], preferred_element_type=F32) + br_ref[...])
    gate = jax.nn.sigmoid(jnp.dot(xb, wi_ref[...], preferred_element_type=F32) + bi_ref[...])
    log_a = -LRU_C * r * _softplus(-lam_ref[...])
    a = jnp.exp(log_a)
    th = jnp.tanh(log_a)
    u = jnp.sqrt(-2.0 * th / (1.0 - th)) * (gate * xc)
    a_cum, h = _linear_scan_rows(a, u)
    h = h + a_cum * h_ref[...]
    h_ref[...] = h[ts - 1:ts, :]
    o_ref[0] = (h * jax.nn.gelu(ga, approximate=True)).astype(o_ref.dtype)


def _lru(xg, conv_w, conv_b, w_r, b_r, w_i, b_i, lam, ts=256):
    b, s, w2 = xg.shape
    width = w2 // 2
    assert s % ts == 0
    row = lambda v: v.reshape(1, width)
    const = lambda shape: _resident(shape, lambda bi, si: (0, 0))
    return pl.pallas_call(
        functools.partial(_lru_kernel, ts=ts, width=width),
        grid=(b, s // ts),
        in_specs=[pl.BlockSpec((1, ts, w2), lambda bi, si: (bi, si, 0)),
                  const((LRU_CONV, width)), const((1, width)),
                  const((width, width)), const((1, width)),
                  const((width, width)), const((1, width)), const((1, width))],
        out_specs=pl.BlockSpec((1, ts, width), lambda bi, si: (bi, si, 0)),
        out_shape=jax.ShapeDtypeStruct((b, s, width), BF16),
        scratch_shapes=[pltpu.VMEM((ts + SUBLANES, width), F32), pltpu.VMEM((1, width), F32)],
        compiler_params=_params("parallel", "arbitrary"),
        name="rg_lru",
    )(xg, conv_w, row(conv_b), w_r, row(b_r), w_i, row(b_i), row(lam))


def _head_mean_square(x, bd_ref):
    x2 = x * x
    hi = x2.astype(BF16)
    r1 = x2 - hi.astype(F32)
    mid = r1.astype(BF16)
    lo = (r1 - mid.astype(F32)).astype(BF16)
    bd = bd_ref[...]
    tot = (jnp.dot(hi, bd, preferred_element_type=F32) + jnp.dot(mid, bd, preferred_element_type=F32)
           + jnp.dot(lo, bd, preferred_element_type=F32))
    return tot * (1.0 / HEAD_DIM)


def _head_rms_norm(x, g, bd_ref):
    return x * lax.rsqrt(_head_mean_square(x, bd_ref) + NORM_EPS) * g


def _rope(x, cos, sin_signed):
    n = x.shape[1]
    lane = lax.broadcasted_iota(jnp.int32, x.shape, 1)
    first_half = (lane & (HEAD_DIM - 1)) < HEAD_DIM // 2
    partner = jnp.where(first_half, pltpu.roll(x, n - HEAD_DIM // 2, 1), pltpu.roll(x, HEAD_DIM // 2, 1))
    return x * cos + partner * sin_signed


def _prep_even_kernel(q_ref, k_ref, v_ref, qi_ref, kw_ref, cos_ref, sin_ref, gq_ref, gk_ref, bd_ref,
                      qo_ref, kto_ref, vo_ref, qio_ref, kito_ref):
    cos = cos_ref[...]
    sin = sin_ref[...]
    q = _rope(_head_rms_norm(q_ref[0], gq_ref[...], bd_ref), cos, sin)
    qo_ref[0] = (q * QK_SCALE_LOG2).astype(BF16)
    k = _rope(_head_rms_norm(k_ref[0], gk_ref[...], bd_ref), cos, sin)
    kto_ref[0] = k.T.astype(BF16)
    vo_ref[0] = v_ref[0].astype(BF16)
    qio_ref[0] = _rope(qi_ref[0], cos, sin).astype(BF16)
    ki = _rope(kw_ref[0], cos[:, 0:LANES], sin[:, 0:LANES])
    lane = lax.broadcasted_iota(jnp.int32, ki.shape, 1)
    ki2 = jnp.where(lane < HEAD_DIM, ki, pltpu.roll(ki, HEAD_DIM, 1))
    kito_ref[0] = ki2.T.astype(BF16)


def _prep_even(q, k, v, qi, kw, cos, sin, gq, gk, bd, ts=512):
    b, s, w = q.shape
    assert s % ts == 0
    tile = pl.BlockSpec((1, ts, w), lambda bi, si: (bi, si, 0))
    tab = pl.BlockSpec((ts, w), lambda bi, si: (si, 0))
    const = lambda shape: _resident(shape, lambda bi, si: (0, 0))
    return pl.pallas_call(
        _prep_even_kernel,
        grid=(b, s // ts),
        in_specs=[tile, tile, tile, tile, pl.BlockSpec((1, ts, LANES), lambda bi, si: (bi, si, 0)),
                  tab, tab, const((1, w)), const((1, w)), const((w, w))],
        out_specs=[tile, pl.BlockSpec((1, w, ts), lambda bi, si: (bi, 0, si)), tile, tile,
                   pl.BlockSpec((1, LANES, ts), lambda bi, si: (bi, 0, si))],
        out_shape=[jax.ShapeDtypeStruct((b, s, w), BF16), jax.ShapeDtypeStruct((b, w, s), BF16),
                   jax.ShapeDtypeStruct((b, s, w), BF16), jax.ShapeDtypeStruct((b, s, w), BF16),
                   jax.ShapeDtypeStruct((b, LANES, s), BF16)],
        compiler_params=_params("parallel", "parallel"),
        name="prep_even",
    )(q, k, v, qi, kw, cos, sin, gq, gk, bd)


def _prep_odd_kernel(q_ref, k_ref, v_ref, fl_ref, bf_ref, gq_ref, gk_ref, bd_ref,
                     qo_ref, kto_ref, vo_ref, carry_ref):
    @pl.when(pl.program_id(1) == 0)
    def _():
        carry_ref[...] = jnp.zeros_like(carry_ref)

    q = _head_rms_norm(q_ref[0], gq_ref[...], bd_ref) * QK_SCALE_LOG2
    k = _head_rms_norm(k_ref[0], gk_ref[...], bd_ref)
    v = v_ref[0]
    log_f = -_softplus(-(fl_ref[0] + bf_ref[...]))
    c = _cumsum_rows(log_f) + carry_ref[...]
    carry_ref[...] = c[c.shape[0] - 1:, :]
    c2 = c * LOG2E
    lane = lax.broadcasted_iota(jnp.int32, (c.shape[0], LANES), 1)
    for h in range(N_HEADS):
        p, e = divmod(h, 2)
        pick = lambda t: t[:, p * LANES:(p + 1) * LANES] if e == 0 else pltpu.roll(t[:, p * LANES:(p + 1) * LANES], HEAD_DIM, 1)
        cb = jnp.broadcast_to(c2[:, h:h + 1], (c.shape[0], LANES))
        hi = cb.astype(BF16).astype(F32)
        mid = (cb - hi).astype(BF16).astype(F32)
        lo = cb - hi - mid
        q_bias = jnp.where(lane == HEAD_DIM, hi, jnp.where(lane == HEAD_DIM + 1, mid, jnp.where(
            lane == HEAD_DIM + 2, lo, jnp.where(lane < HEAD_DIM + 6, 1.0, 0.0))))
        k_bias = jnp.where(lane < HEAD_DIM + 3, 1.0, jnp.where(lane == HEAD_DIM + 3, -hi, jnp.where(
            lane == HEAD_DIM + 4, -mid, jnp.where(lane == HEAD_DIM + 5, -lo, 0.0))))
        qo_ref[0, h] = jnp.where(lane < HEAD_DIM, pick(q), q_bias).astype(BF16)
        kto_ref[0, h] = jnp.where(lane < HEAD_DIM, pick(k), k_bias).T.astype(BF16)
        vo_ref[0, h] = jnp.where(lane < HEAD_DIM, pick(v), 1.0).astype(BF16)


def _prep_odd(q, k, v, fl, bf, gq, gk, bd, ts=512):
    b, s, w = q.shape
    assert s % ts == 0
    tile = pl.BlockSpec((1, ts, w), lambda bi, si: (bi, si, 0))
    narrow = pl.BlockSpec((1, ts, LANES), lambda bi, si: (bi, si, 0))
    heads = pl.BlockSpec((1, N_HEADS, ts, LANES), lambda bi, si: (bi, 0, si, 0))
    const = lambda shape: _resident(shape, lambda bi, si: (0, 0))
    return pl.pallas_call(
        _prep_odd_kernel,
        grid=(b, s // ts),
        in_specs=[tile, tile, tile, narrow, const((1, LANES)), const((1, w)), const((1, w)), const((w, w))],
        out_specs=[heads, pl.BlockSpec((1, N_HEADS, LANES, ts), lambda bi, si: (bi, 0, 0, si)), heads],
        out_shape=[jax.ShapeDtypeStruct((b, N_HEADS, s, LANES), BF16),
                   jax.ShapeDtypeStruct((b, N_HEADS, LANES, s), BF16),
                   jax.ShapeDtypeStruct((b, N_HEADS, s, LANES), BF16)],
        scratch_shapes=[pltpu.VMEM((1, LANES), F32)],
        compiler_params=_params("parallel", "arbitrary"),
        name="prep_odd",
    )(q, k, v, fl, bf, gq, gk, bd)


def _head_halves(x_pair):
    lane = lax.broadcasted_iota(jnp.int32, x_pair.shape, 1)
    zero = jnp.zeros_like(x_pair)
    return jnp.where(lane < HEAD_DIM, x_pair, zero), jnp.where(lane >= HEAD_DIM, x_pair, zero)


def _fold_lanes(x, op):
    out = x[:, 0:LANES]
    for c in range(1, x.shape[1] // LANES):
        out = op(out, x[:, c * LANES:(c + 1) * LANES])
    return out


def _online_softmax_block(scores, values, m_ref, acc_ref):
    probs, alphas = [], []
    for h, s in enumerate(scores):
        m_old = m_ref[h]
        m_new = jnp.maximum(m_old, jnp.max(_fold_lanes(s, jnp.maximum), axis=1, keepdims=True).astype(F32))
        p = jnp.exp2(s - jnp.tile(m_new.astype(s.dtype), (1, s.shape[1] // LANES)))
        m_ref[h] = m_new
        probs.append(p.astype(BF16))
        alphas.append(jnp.exp2(m_old - m_new))
    for h, (p, alpha) in enumerate(zip(probs, alphas)):
        acc_ref[h] = alpha * acc_ref[h] + jnp.dot(p, values[h], preferred_element_type=F32)


def _to_ordinal(x):
    bits = lax.bitcast_convert_type(x, jnp.int32)
    return jnp.where(bits < 0, bits ^ jnp.int32(0x7FFFFFFF), bits)


def _from_ordinal(o):
    return lax.bitcast_convert_type(jnp.where(o < 0, o ^ jnp.int32(0x7FFFFFFF), o), F32)


def _ordinal_midpoint(lo, hi):
    a, b = _to_ordinal(lo), _to_ordinal(hi)
    return _from_ordinal((a >> 1) + (b >> 1) + (a & b & 1))


def _next_float_up(x):
    return _from_ordinal(_to_ordinal(x) + 1)


def _normalise(acc):
    return acc / pltpu.roll(acc, HEAD_DIM, 1)


def _dsa_index_kernel(qi_ref, kw_ref, kit_ref, mask_ref, sc_ref, lo_ref, hi_ref, *, tq, tk, topk, max_iters):
    t0 = pl.program_id(1) * tq
    nb = (t0 + tq + tk - 1) // tk
    row_chunk = (t0 + lax.broadcasted_iota(jnp.int32, (tq, 1), 0)) >> CHUNK_SHIFT
    kf = float(topk)

    def cols(j):
        return pl.ds(pl.multiple_of(j * tk, tk), tk)

    qi_heads = []
    for p in range(N_PAIRS):
        qi_heads.extend(_head_halves(qi_ref[0, :, p * LANES:(p + 1) * LANES]))
    kw = kw_ref[0]

    def score_block(j, carry):
        kit = kit_ref[0, :, cols(j)]
        score = jnp.zeros((tq, tk), F32)
        for h in range(N_HEADS):
            rel = jnp.maximum(jnp.dot(qi_heads[h], kit, preferred_element_type=F32), 0.0)
            score = score + kw[:, HEAD_DIM + h:HEAD_DIM + h + 1] * rel
        key_chunk = (j * tk + lax.broadcasted_iota(jnp.int32, (1, tk), 1)) >> CHUNK_SHIFT
        visible = key_chunk <= row_chunk
        sc_ref[:, cols(j)] = jnp.where(visible, score, -jnp.inf)
        lo_ref[...] = jnp.minimum(lo_ref[...], _fold_lanes(jnp.where(visible, score, jnp.inf), jnp.minimum))
        hi_ref[...] = jnp.maximum(hi_ref[...], _fold_lanes(jnp.where(visible, score, -jnp.inf), jnp.maximum))
        return carry

    lo_ref[...] = jnp.full_like(lo_ref, jnp.inf)
    hi_ref[...] = jnp.full_like(hi_ref, -jnp.inf)
    lax.fori_loop(0, nb, score_block, 0)

    reps = tk // LANES

    def to_col(row):
        return jnp.broadcast_to(row, (LANES, tq)).T

    def to_row(wide, reduce):
        return reduce(wide.T, axis=0, keepdims=True)

    def count(row, strict=False, wanted=None):
        col = to_col(row)
        parts = []
        for g, r in enumerate(range(0, tq, ROW_GROUP)):
            bound = jnp.tile(col[r:r + ROW_GROUP], (1, reps))

            def body(j, cnt):
                blk = sc_ref[r:r + ROW_GROUP, cols(j)]
                hit = blk > bound if strict else blk >= bound
                return cnt + _fold_lanes(jnp.where(hit, 1.0, 0.0), jnp.add)

            blocks = nb if wanted is None else jnp.where(wanted[g] > 0.0, nb, 0)
            parts.append(lax.fori_loop(0, blocks, body, jnp.zeros((ROW_GROUP, LANES), F32)))
        return to_row(jnp.concatenate(parts, axis=0), jnp.sum)

    lo0 = to_row(lo_ref[...], jnp.min)
    hi0 = _next_float_up(to_row(hi_ref[...], jnp.max))
    row_index = lax.broadcasted_iota(jnp.int32, (1, tq), 1)
    group_of_row = row_index >> ROW_GROUP_SHIFT
    n_visible = (((t0 + row_index) >> CHUNK_SHIFT) + 1) * CHUNK
    short_row = n_visible <= topk
    log_k = jnp.log(kf + 0.5)

    def open_groups(lo, hi, c_lo):
        finished = jnp.logical_or(jnp.logical_or(c_lo == kf, short_row),
                                  _to_ordinal(_ordinal_midpoint(lo, hi)) == _to_ordinal(lo))
        still = jnp.where(finished, 0.0, 1.0)
        return tuple(jnp.max(jnp.where(group_of_row == g, still, 0.0)) for g in range(tq // ROW_GROUP))

    def search_cond(state):
        return jnp.logical_and(state[0] < max_iters, functools.reduce(jnp.maximum, state[-1]) > 0.0)

    def search_body(state):
        it, lo, hi, c_lo, f_lo, f_hi, side, wanted = state
        guess = lo + (hi - lo) * jnp.clip(f_lo / (f_lo - f_hi), 1.0 / 64, 63.0 / 64)
        inside = jnp.logical_and(guess > lo, guess < hi)
        half = _ordinal_midpoint(lo, hi)
        mid = jnp.where(it % 3 == 2, half, jnp.where(inside, guess, half))
        c_mid = count(mid, wanted=wanted)
        counted = sum(jnp.where(group_of_row == g, w, 0.0) for g, w in enumerate(wanted)) > 0.0
        enough = jnp.logical_and(counted, c_mid >= kf)
        short = jnp.logical_and(counted, c_mid < kf)
        f_mid = jnp.log(jnp.maximum(c_mid, 0.5)) - log_k
        f_hi = jnp.where(enough, jnp.where(side > 0.0, f_hi * 0.5, f_hi), jnp.where(short, f_mid, f_hi))
        f_lo = jnp.where(enough, f_mid, jnp.where(jnp.logical_and(short, side < 0.0), f_lo * 0.5, f_lo))
        lo, c_lo = jnp.where(enough, mid, lo), jnp.where(enough, c_mid, c_lo)
        hi = jnp.where(short, mid, hi)
        side = jnp.where(enough, 1.0, jnp.where(short, -1.0, side))
        return it + 1, lo, hi, c_lo, f_lo, f_hi, side, open_groups(lo, hi, c_lo)

    c_lo0 = n_visible.astype(F32)
    state0 = (jnp.int32(0), lo0, hi0, c_lo0, jnp.log(c_lo0) - log_k, jnp.log(0.5) - log_k + jnp.zeros_like(lo0),
              jnp.zeros_like(lo0), open_groups(lo0, hi0, c_lo0))
    _, lo, _, n_ge, _, _, _, _ = lax.while_loop(search_cond, search_body, state0)
    thr_row = jnp.where(short_row, jnp.finfo(F32).min, lo)
    thr = jnp.tile(to_col(thr_row), (1, reps))

    has_extra = jnp.max(jnp.where(jnp.logical_and(n_ge > kf, jnp.logical_not(short_row)), 1.0, 0.0))

    @pl.when(has_extra > 0.0)
    def _():
        keep = to_col(kf - count(thr_row, strict=True))[:, 0:1]
        upper = (lax.broadcasted_iota(jnp.int32, (tk, tk), 0)
                 <= lax.broadcasted_iota(jnp.int32, (tk, tk), 1))
        prefix = jnp.where(upper, 1.0, 0.0).astype(BF16)

        def body(j, seen):
            blk = sc_ref[:, cols(j)]
            tie = jnp.where(blk == thr, 1.0, 0.0)
            rank = seen + jnp.dot(tie.astype(BF16), prefix, preferred_element_type=F32)
            drop = jnp.where(rank > keep, tie, 0.0)
            sc_ref[:, cols(j)] = jnp.where(drop > 0.0, -jnp.inf, blk)
            return seen + jnp.sum(tie, axis=1, keepdims=True)

        lax.fori_loop(0, nb, body, jnp.zeros((tq, 1), F32))

    def mask_block(j, carry):
        mask_ref[0, :, cols(j)] = jnp.where(sc_ref[:, cols(j)] >= thr, 1, 0).astype(mask_ref.dtype)
        return carry

    def clear_block(j, carry):
        mask_ref[0, :, cols(j)] = jnp.zeros((tq, tk), mask_ref.dtype)
        return carry

    lax.fori_loop(0, nb, mask_block, 0)
    lax.fori_loop(nb, mask_ref.shape[2] // tk, clear_block, 0)


def _dsa_index(qi, kw, kit, tq, tk):
    b, s, w = qi.shape
    assert s % tk == 0 and s % tq == 0 and tq % CHUNK == 0
    topk = min(TOPK_MAX, s // 4)
    return pl.pallas_call(
        functools.partial(_dsa_index_kernel, tq=tq, tk=tk, topk=topk, max_iters=640),
        grid=(b, s // tq),
        in_specs=[pl.BlockSpec((1, tq, w), lambda bi, i: (bi, i, 0)),
                  pl.BlockSpec((1, tq, LANES), lambda bi, i: (bi, i, 0)),
                  _resident((1, LANES, s), lambda bi, i: (bi, 0, 0))],
        out_specs=_resident((1, tq, s), lambda bi, i: (bi, i, 0)),
        out_shape=jax.ShapeDtypeStruct((b, s, s), jnp.int8),
        scratch_shapes=[pltpu.VMEM((tq, s), F32), pltpu.VMEM((tq, LANES), F32), pltpu.VMEM((tq, LANES), F32)],
        compiler_params=_params("parallel", "arbitrary"),
        name="dsa_index",
    )(qi, kw, kit)


def _dsa_attend_kernel(q_ref, kt_ref, v_ref, mask_ref, o_ref, m_ref, acc_ref, *, tq, tk):
    nb = (pl.program_id(1) * tq + tq + tk - 1) // tk
    m_ref[...] = jnp.full_like(m_ref, MASKED)
    acc_ref[...] = jnp.zeros_like(acc_ref)
    q_heads = []
    for p in range(N_PAIRS):
        q_heads.extend(_head_halves(q_ref[0, :, p * LANES:(p + 1) * LANES]))
    first_half = lax.broadcasted_iota(jnp.int32, (tk, LANES), 1) < HEAD_DIM
    one = jnp.ones((), BF16)

    def attend_block(j, carry):
        cols = pl.ds(pl.multiple_of(j * tk, tk), tk)
        bias = jnp.where(mask_ref[0, :, cols].astype(jnp.int32) > 0, 0.0, MASKED).astype(BF16)
        scores, values = [], []
        for h in range(N_HEADS):
            p, e = divmod(h, 2)
            kt = kt_ref[0, p * LANES:(p + 1) * LANES, cols]
            scores.append(jnp.dot(q_heads[h], kt, preferred_element_type=F32).astype(BF16) + bias)
            v_pair = v_ref[0, cols, p * LANES:(p + 1) * LANES]
            values.append(jnp.where(first_half, v_pair, one) if e == 0 else jnp.where(first_half, one, v_pair))
        _online_softmax_block(scores, values, m_ref, acc_ref)
        return carry

    lax.fori_loop(0, nb, attend_block, 0)
    lane = lax.broadcasted_iota(jnp.int32, (tq, LANES), 1)
    for p in range(N_PAIRS):
        o_ref[0, :, p * LANES:(p + 1) * LANES] = jnp.where(
            lane < HEAD_DIM, _normalise(acc_ref[2 * p]), _normalise(acc_ref[2 * p + 1])).astype(o_ref.dtype)


def _dsa_attend(q, kt, v, mask, tq, tk):
    b, s, w = q.shape
    assert s % tk == 0 and s % tq == 0
    qtile = pl.BlockSpec((1, tq, w), lambda bi, i: (bi, i, 0))
    return pl.pallas_call(
        functools.partial(_dsa_attend_kernel, tq=tq, tk=tk),
        grid=(b, s // tq),
        in_specs=[qtile,
                  _resident((1, w, s), lambda bi, i: (bi, 0, 0)),
                  _resident((1, s, w), lambda bi, i: (bi, 0, 0)),
                  pl.BlockSpec((1, tq, s), lambda bi, i: (bi, i, 0))],
        out_specs=qtile,
        out_shape=jax.ShapeDtypeStruct((b, s, w), BF16),
        scratch_shapes=[pltpu.VMEM((N_HEADS, tq, LANES), F32), pltpu.VMEM((N_HEADS, tq, LANES), F32)],
        compiler_params=_params("parallel", "arbitrary"),
        name="dsa_attend",
    )(q, kt, v, mask)


def _dsa(qi, kw, kit, q, kt, v, tq_index=512, tq_attend=256, tk=512):
    s = q.shape[1]
    mask = _dsa_index(qi, kw, kit, min(tq_index, s), min(tk, s))
    return _dsa_attend(q, kt, v, mask, min(tq_attend, s), min(tk, s))


def _fox_kernel(bound_ref, q_ref, kt_ref, v_ref, o_ref, m_ref, acc_ref, ckey_ref, *, tq, heads):
    i = pl.program_id(2)
    m_ref[...] = jnp.full_like(m_ref, MASKED)
    acc_ref[...] = jnp.zeros_like(acc_ref)
    causal = (lax.broadcasted_iota(jnp.int32, (tq, tq), 1) <= lax.broadcasted_iota(jnp.int32, (tq, tq), 0))

    @pl.when(i == 0)
    def _():
        for e in range(heads):
            tail = kt_ref[0, e, HEAD_DIM:HEAD_DIM + 16, :].astype(F32)
            row = lax.broadcasted_iota(jnp.int32, tail.shape, 0)
            ckey_ref[e] = -jnp.sum(jnp.where(jnp.logical_and(row >= 3, row < 6), tail, 0.0), axis=0, keepdims=True)

    pos = lax.broadcasted_iota(jnp.int32, (1, ckey_ref.shape[2]), 1)
    block_end = jnp.logical_and((pos & (tq - 1)) == tq - 1, pos < i * tq)
    lane = lax.broadcasted_iota(jnp.int32, (tq, LANES), 1)
    first = None
    for e in range(heads):
        c_first = ckey_ref[e, :, pl.ds(pl.multiple_of(i * tq, tq), LANES)][:, 0:1]
        dead = jnp.logical_and(block_end, ckey_ref[e] - c_first > 2.0 * bound_ref[0] + UNDERFLOW_BITS)
        n_dead = jnp.sum(jnp.where(dead, 1, 0))
        first = n_dead if first is None else jnp.minimum(first, n_dead)

    def block(j, diagonal):
        cols = pl.ds(pl.multiple_of(j * tq, tq), tq)
        scores = []
        for e in range(heads):
            s = jnp.dot(q_ref[0, e], kt_ref[0, e, :, cols], preferred_element_type=F32)
            scores.append(jnp.where(causal, s, MASKED) if diagonal else s)
        _online_softmax_block(scores, [v_ref[0, e, cols, :] for e in range(heads)], m_ref, acc_ref)

    def body(j, carry):
        block(j, False)
        return carry

    lax.fori_loop(first, i, body, 0)
    block(i, True)
    for p in range(heads // 2):
        a0, a1 = acc_ref[2 * p], acc_ref[2 * p + 1]
        o_ref[0, :, p * LANES:(p + 1) * LANES] = jnp.where(
            lane < HEAD_DIM, a0 / pltpu.roll(a0, HEAD_DIM, 1), pltpu.roll(a1, HEAD_DIM, 1) / a1).astype(o_ref.dtype)


def _fox(qk_bound, q, kt, v, tq=512, heads=4):
    b, nh, s, _ = q.shape
    assert s % tq == 0 and tq & (tq - 1) == 0 and nh == N_HEADS and nh % heads == 0 and heads % 2 == 0
    out_w = heads * HEAD_DIM
    return pl.pallas_call(
        functools.partial(_fox_kernel, tq=tq, heads=heads),
        grid=(b, nh // heads, s // tq),
        in_specs=[pl.BlockSpec(memory_space=pltpu.SMEM),
                  pl.BlockSpec((1, heads, tq, LANES), lambda bi, g, i: (bi, g, i, 0)),
                  _resident((1, heads, LANES, s), lambda bi, g, i: (bi, g, 0, 0)),
                  _resident((1, heads, s, LANES), lambda bi, g, i: (bi, g, 0, 0))],
        out_specs=pl.BlockSpec((1, tq, out_w), lambda bi, g, i: (bi, i, g)),
        out_shape=jax.ShapeDtypeStruct((b, s, N_HEADS * HEAD_DIM), BF16),
        scratch_shapes=[pltpu.VMEM((heads, tq, LANES), F32), pltpu.VMEM((heads, tq, LANES), F32),
                        pltpu.VMEM((heads, 1, s), F32)],
        compiler_params=_params("parallel", "parallel", "arbitrary"),
        name="fox",
    )(qk_bound, q, kt, v)


def _convmod_kernel(u_ref, cw_ref, cb_ref, g_ref, b_ref, o_ref, xbuf_ref, *, ts, width, halo):
    @pl.when(pl.program_id(1) == 0)
    def _():
        xbuf_ref[0:halo, :] = jnp.zeros((halo, width), F32)

    x = u_ref[0, :, 0:width] * jax.nn.sigmoid(u_ref[0, :, width:2 * width])
    xbuf_ref[halo:halo + ts, :] = x
    base = halo - (CONV_KERNEL - 1)
    y = cb_ref[...] + cw_ref[0:1, :] * xbuf_ref[pl.ds(base, ts), :]
    for j in range(1, CONV_KERNEL):
        y = y + cw_ref[j:j + 1, :] * xbuf_ref[pl.ds(base + j, ts), :]
    xbuf_ref[0:halo, :] = x[ts - halo:ts, :]
    mu = jnp.mean(y, axis=-1, keepdims=True)
    var = jnp.mean(jnp.square(y - mu), axis=-1, keepdims=True)
    z = (y - mu) * lax.rsqrt(var + NORM_EPS) * g_ref[...] + b_ref[...]
    o_ref[0] = (z * jax.nn.sigmoid(z)).astype(o_ref.dtype)


def _convmod(u, conv_w, conv_b, ln_g, ln_b, ts=512, halo=32):
    b, s, w2 = u.shape
    width = w2 // 2
    assert s % ts == 0 and halo >= CONV_KERNEL - 1
    row = lambda v: v.reshape(1, width)
    const = lambda shape: _resident(shape, lambda bi, si: (0, 0))
    return pl.pallas_call(
        functools.partial(_convmod_kernel, ts=ts, width=width, halo=halo),
        grid=(b, s // ts),
        in_specs=[pl.BlockSpec((1, ts, w2), lambda bi, si: (bi, si, 0)),
                  const((CONV_KERNEL, width)), const((1, width)), const((1, width)), const((1, width))],
        out_specs=pl.BlockSpec((1, ts, width), lambda bi, si: (bi, si, 0)),
        out_shape=jax.ShapeDtypeStruct((b, s, width), BF16),
        scratch_shapes=[pltpu.VMEM((ts + halo, width), F32)],
        compiler_params=_params("parallel", "arbitrary"),
        name="conv_module",
    )(u, conv_w, row(conv_b), row(ln_g), row(ln_b))


def _out_ffn_kernel(x_ref, ya_ref, yb_ref, wo_ref, g_ref, wgu_ref, wd_ref, o_ref, *, hidden, th):
    half = ya_ref.shape[1]
    x = x_ref[...] + (jnp.dot(ya_ref[...], wo_ref[0:half, :], preferred_element_type=F32)
                      + jnp.dot(yb_ref[...], wo_ref[half:2 * half, :], preferred_element_type=F32))
    ms = jnp.mean(x * x, axis=-1, keepdims=True)
    h = (x * lax.rsqrt(ms + NORM_EPS) * g_ref[...]).astype(BF16)
    y = x
    for c in range(0, hidden, th):
        gate = jnp.dot(h, wgu_ref[:, c:c + th], preferred_element_type=F32)
        up = jnp.dot(h, wgu_ref[:, hidden + c:hidden + c + th], preferred_element_type=F32)
        act = (gate * jax.nn.sigmoid(gate) * up).astype(BF16)
        y = y + jnp.dot(act, wd_ref[c:c + th, :], preferred_element_type=F32)
    o_ref[...] = y


def _out_ffn(x2, ya, yb, w_out, g, w_gu, w_down, tm=512, th=256):
    n, d = x2.shape
    half = ya.shape[1]
    hidden = w_down.shape[0]
    assert n % tm == 0 and hidden % th == 0
    const = lambda shape: _resident(shape, lambda i: (0, 0))
    return pl.pallas_call(
        functools.partial(_out_ffn_kernel, hidden=hidden, th=th),
        grid=(n // tm,),
        in_specs=[pl.BlockSpec((tm, d), lambda i: (i, 0)),
                  pl.BlockSpec((tm, half), lambda i: (i, 0)), pl.BlockSpec((tm, half), lambda i: (i, 0)),
                  const(w_out.shape), const((1, d)), const(w_gu.shape), const(w_down.shape)],
        out_specs=pl.BlockSpec((tm, d), lambda i: (i, 0)),
        out_shape=jax.ShapeDtypeStruct((n, d), F32),
        compiler_params=_params("parallel"),
        name="out_ffn",
    )(x2, ya, yb, w_out, g.reshape(1, d), w_gu, w_down)


def _rope_tables(s, width):
    inv = ROPE_THETA ** (-jnp.arange(0, HEAD_DIM, 2, dtype=F32) / HEAD_DIM)
    ang = jnp.arange(s, dtype=F32)[:, None] * inv[None, :]
    cos, sin = jnp.cos(ang), jnp.sin(ang)
    reps = width // HEAD_DIM
    return (jnp.tile(jnp.concatenate([cos, cos], axis=-1), (1, reps)),
            jnp.tile(jnp.concatenate([-sin, sin], axis=-1), (1, reps)))


def _block_diag(blocks):
    n, d, _ = blocks.shape
    eye = jnp.eye(n, dtype=blocks.dtype)
    return jnp.einsum('nde,nm->ndme', blocks, eye).reshape(n * d, n * d)


def _pad_cols(w, total):
    return jnp.pad(w, ((0, 0), (0, total - w.shape[1])))


def kernel(x, norm_mix, norm_ffn,
           ev_w_in, ev_conv_w, ev_conv_b, ev_w_r, ev_b_r, ev_w_i, ev_b_i, ev_lam,
           ev_q_norm, ev_k_norm, ev_w_out,
           od_w_in, od_b_f, od_q_norm, od_k_norm, od_conv_w, od_conv_b, od_ln_g, od_ln_b,
           od_w_out, ffn_w_gu, ffn_w_down):
    b, s, d = x.shape
    depth = norm_mix.shape[0]
    w = N_HEADS * HEAD_DIM
    cos, sin = _rope_tables(s, w)
    head_ones = _block_diag(jnp.ones((N_HEADS, HEAD_DIM, HEAD_DIM), BF16))
    tile_gain = lambda g: jnp.tile(g, N_HEADS).reshape(1, w)
    x2 = x.reshape(b * s, d)
    seq = lambda t: t.reshape(b, s, t.shape[-1])
    for l in range(depth):
        j = l // 2
        if l % 2 == 0:
            w_in = _pad_cols(ev_w_in[j], 2 * w + 4 * w + LANES).astype(BF16)
            xg, q, k, v, qi, kw = _norm_matmul(x2, norm_mix[l], w_in, (2 * w, w, w, w, w, LANES))
            ya = _lru(seq(xg), ev_conv_w[j], ev_conv_b[j],
                      _block_diag(ev_w_r[j]).astype(BF16), ev_b_r[j],
                      _block_diag(ev_w_i[j]).astype(BF16), ev_b_i[j], ev_lam[j])
            qh, kt, vh, qih, kit = _prep_even(seq(q), seq(k), seq(v), seq(qi), seq(kw), cos, sin,
                                              tile_gain(ev_q_norm[j]), tile_gain(ev_k_norm[j]), head_ones)
            yb = _dsa(qih, seq(kw), kit, qh, kt, vh)
            w_out = ev_w_out[j]
        else:
            wi = od_w_in[j]
            w_in = jnp.concatenate([wi[:, 0:3 * w], wi[:, 3 * w + N_HEADS:],
                                    _pad_cols(wi[:, 3 * w:3 * w + N_HEADS], LANES)], axis=1).astype(BF16)
            q, k, v, u, fl = _norm_matmul(x2, norm_mix[l], w_in, (w, w, w, 2 * w, LANES))
            qh, kt, vh = _prep_odd(seq(q), seq(k), seq(v), seq(fl),
                                   _pad_cols(od_b_f[j].reshape(1, N_HEADS), LANES),
                                   tile_gain(od_q_norm[j]), tile_gain(od_k_norm[j]), head_ones)
            qk_bound = (HEAD_DIM * QK_SCALE_LOG2 * 1.01 * jnp.max(jnp.abs(od_q_norm[j]))
                        * jnp.max(jnp.abs(od_k_norm[j])) + 1.0).reshape(1).astype(F32)
            ya = _fox(qk_bound, qh, kt, vh, tq=min(512, s))
            yb = _convmod(seq(u), od_conv_w[j], od_conv_b[j], od_ln_g[j], od_ln_b[j])
            w_out = od_w_out[j]
        x2 = _out_ffn(x2, ya.reshape(b * s, w), yb.reshape(b * s, w), w_out.astype(BF16),
                      norm_ffn[l], ffn_w_gu[l].astype(BF16), ffn_w_down[l].astype(BF16))
    return x2.reshape(b, s, d)
```

```python
import functools

import jax
import jax.numpy as jnp
from jax import lax
from jax.experimental import pallas as pl
from jax.experimental.pallas import tpu as pltpu

F32 = jnp.float32
BF16 = jnp.bfloat16

HEAD_DIM = 64
N_HEADS = 8
N_PAIRS = N_HEADS // 2
LANES = 128
SUBLANES = 8
CHUNK = 64
CHUNK_SHIFT = 6
TOPK_MAX = 256
ROPE_THETA = 10000.0
NORM_EPS = 1e-6
LRU_C = 8.0
LRU_CONV = 4
CONV_KERNEL = 31
MASKED = -2.0 ** 100
UNDERFLOW_BITS = 160.0
LOG2E = 1.4426950408889634
QK_SCALE_LOG2 = HEAD_DIM ** -0.5 * LOG2E
VMEM_LIMIT = 56 * 1024 * 1024

ROW_GROUP = 64
ROW_GROUP_SHIFT = 6

assert CHUNK == 1 << CHUNK_SHIFT and ROW_GROUP == 1 << ROW_GROUP_SHIFT


def _params(*sem):
    return pltpu.CompilerParams(dimension_semantics=sem, vmem_limit_bytes=VMEM_LIMIT)


def _resident(shape, index_map):
    return pl.BlockSpec(shape, index_map, pipeline_mode=pl.Buffered(1))


def _norm_matmul_kernel(x_ref, g_ref, w_ref, *o_refs, widths):
    x = x_ref[...]
    ms = jnp.mean(x * x, axis=-1, keepdims=True)
    h = (x * lax.rsqrt(ms + NORM_EPS) * g_ref[...]).astype(BF16)
    off = 0
    for o_ref, wd in zip(o_refs, widths):
        o_ref[...] = jnp.dot(h, w_ref[:, off:off + wd], preferred_element_type=F32)
        off += wd


def _norm_matmul(x2, g, w, widths, tm=512):
    n, d = x2.shape
    assert n % tm == 0 and sum(widths) == w.shape[1]
    return pl.pallas_call(
        functools.partial(_norm_matmul_kernel, widths=widths),
        grid=(n // tm,),
        in_specs=[pl.BlockSpec((tm, d), lambda i: (i, 0)),
                  _resident((1, d), lambda i: (0, 0)),
                  _resident(w.shape, lambda i: (0, 0))],
        out_specs=[pl.BlockSpec((tm, wd), lambda i: (i, 0)) for wd in widths],
        out_shape=[jax.ShapeDtypeStruct((n, wd), F32) for wd in widths],
        compiler_params=_params("parallel"),
        name="norm_matmul",
    )(x2, g.reshape(1, d), w)


def _shift_rows(x, d, fill):
    row = lax.broadcasted_iota(jnp.int32, x.shape, 0)
    return jnp.where(row >= d, pltpu.roll(x, d, 0), fill)


def _linear_scan_rows(a, u):
    d = 1
    while d < a.shape[0]:
        u = a * _shift_rows(u, d, 0.0) + u
        a = a * _shift_rows(a, d, 1.0)
        d *= 2
    return a, u


def _cumsum_rows(x):
    d = 1
    while d < x.shape[0]:
        x = x + _shift_rows(x, d, 0.0)
        d *= 2
    return x


def _softplus(x):
    return jnp.maximum(x, 0.0) + jnp.log1p(jnp.exp(-jnp.abs(x)))


def _lru_kernel(xg_ref, cw_ref, cb_ref, wr_ref, br_ref, wi_ref, bi_ref, lam_ref, o_ref,
                xbuf_ref, h_ref, *, ts, width):
    @pl.when(pl.program_id(1) == 0)
    def _():
        xbuf_ref[0:SUBLANES, :] = jnp.zeros((SUBLANES, width), F32)
        h_ref[...] = jnp.zeros_like(h_ref)

    xa = xg_ref[0, :, 0:width]
    ga = xg_ref[0, :, width:2 * width]
    xbuf_ref[SUBLANES:SUBLANES + ts, :] = xa
    base = SUBLANES - (LRU_CONV - 1)
    xc = cb_ref[...] + cw_ref[0:1, :] * xbuf_ref[pl.ds(base, ts), :]
    for j in range(1, LRU_CONV):
        xc = xc + cw_ref[j:j + 1, :] * xbuf_ref[pl.ds(base + j, ts), :]
    xbuf_ref[0:SUBLANES, :] = xa[ts - SUBLANES:ts, :]

    xb = xc.astype(BF16)
    r = jax.nn.sigmoid(jnp.dot(xb, wr_ref[...], preferred_element_type=F32) + br_ref[...])
    gate = jax.nn.sigmoid(jnp.dot(xb, wi_ref[...], preferred_element_type=F32) + bi_ref[...])
    log_a = -LRU_C * r * _softplus(-lam_ref[...])
    a = jnp.exp(log_a)
    th = jnp.tanh(log_a)
    u = jnp.sqrt(-2.0 * th / (1.0 - th)) * (gate * xc)
    a_cum, h = _linear_scan_rows(a, u)
    h = h + a_cum * h_ref[...]
    h_ref[...] = h[ts - 1:ts, :]
    o_ref[0] = (h * jax.nn.gelu(ga, approximate=True)).astype(o_ref.dtype)


def _lru(xg, conv_w, conv_b, w_r, b_r, w_i, b_i, lam, ts=256):
    b, s, w2 = xg.shape
    width = w2 // 2
    assert s % ts == 0
    row = lambda v: v.reshape(1, width)
    const = lambda shape: _resident(shape, lambda bi, si: (0, 0))
    return pl.pallas_call(
        functools.partial(_lru_kernel, ts=ts, width=width),
        grid=(b, s // ts),
        in_specs=[pl.BlockSpec((1, ts, w2), lambda bi, si: (bi, si, 0)),
                  const((LRU_CONV, width)), const((1, width)),
                  const((width, width)), const((1, width)),
                  const((width, width)), const((1, width)), const((1, width))],
        out_specs=pl.BlockSpec((1, ts, width), lambda bi, si: (bi, si, 0)),
        out_shape=jax.ShapeDtypeStruct((b, s, width), BF16),
        scratch_shapes=[pltpu.VMEM((ts + SUBLANES, width), F32), pltpu.VMEM((1, width), F32)],
        compiler_params=_params("parallel", "arbitrary"),
        name="rg_lru",
    )(xg, conv_w, row(conv_b), w_r, row(b_r), w_i, row(b_i), row(lam))


def _head_mean_square(x, bd_ref):
    x2 = x * x
    hi = x2.astype(BF16)
    r1 = x2 - hi.astype(F32)
    mid = r1.astype(BF16)
    lo = (r1 - mid.astype(F32)).astype(BF16)
    bd = bd_ref[...]
    tot = (jnp.dot(hi, bd, preferred_element_type=F32) + jnp.dot(mid, bd, preferred_element_type=F32)
           + jnp.dot(lo, bd, preferred_element_type=F32))
    return tot * (1.0 / HEAD_DIM)


def _head_rms_norm(x, g, bd_ref):
    return x * lax.rsqrt(_head_mean_square(x, bd_ref) + NORM_EPS) * g


def _rope(x, cos, sin_signed):
    n = x.shape[1]
    lane = lax.broadcasted_iota(jnp.int32, x.shape, 1)
    first_half = (lane & (HEAD_DIM - 1)) < HEAD_DIM // 2
    partner = jnp.where(first_half, pltpu.roll(x, n - HEAD_DIM // 2, 1), pltpu.roll(x, HEAD_DIM // 2, 1))
    return x * cos + partner * sin_signed


def _prep_even_kernel(q_ref, k_ref, v_ref, qi_ref, kw_ref, cos_ref, sin_ref, gq_ref, gk_ref, bd_ref,
                      qo_ref, kto_ref, vo_ref, qio_ref, kito_ref):
    cos = cos_ref[...]
    sin = sin_ref[...]
    q = _rope(_head_rms_norm(q_ref[0], gq_ref[...], bd_ref), cos, sin)
    qo_ref[0] = (q * QK_SCALE_LOG2).astype(BF16)
    k = _rope(_head_rms_norm(k_ref[0], gk_ref[...], bd_ref), cos, sin)
    kto_ref[0] = k.T.astype(BF16)
    vo_ref[0] = v_ref[0].astype(BF16)
    qio_ref[0] = _rope(qi_ref[0], cos, sin).astype(BF16)
    ki = _rope(kw_ref[0], cos[:, 0:LANES], sin[:, 0:LANES])
    lane = lax.broadcasted_iota(jnp.int32, ki.shape, 1)
    ki2 = jnp.where(lane < HEAD_DIM, ki, pltpu.roll(ki, HEAD_DIM, 1))
    kito_ref[0] = ki2.T.astype(BF16)


def _prep_even(q, k, v, qi, kw, cos, sin, gq, gk, bd, ts=512):
    b, s, w = q.shape
    assert s % ts == 0
    tile = pl.BlockSpec((1, ts, w), lambda bi, si: (bi, si, 0))
    tab = pl.BlockSpec((ts, w), lambda bi, si: (si, 0))
    const = lambda shape: _resident(shape, lambda bi, si: (0, 0))
    return pl.pallas_call(
        _prep_even_kernel,
        grid=(b, s // ts),
        in_specs=[tile, tile, tile, tile, pl.BlockSpec((1, ts, LANES), lambda bi, si: (bi, si, 0)),
                  tab, tab, const((1, w)), const((1, w)), const((w, w))],
        out_specs=[tile, pl.BlockSpec((1, w, ts), lambda bi, si: (bi, 0, si)), tile, tile,
                   pl.BlockSpec((1, LANES, ts), lambda bi, si: (bi, 0, si))],
        out_shape=[jax.ShapeDtypeStruct((b, s, w), BF16), jax.ShapeDtypeStruct((b, w, s), BF16),
                   jax.ShapeDtypeStruct((b, s, w), BF16), jax.ShapeDtypeStruct((b, s, w), BF16),
                   jax.ShapeDtypeStruct((b, LANES, s), BF16)],
        compiler_params=_params("parallel", "parallel"),
        name="prep_even",
    )(q, k, v, qi, kw, cos, sin, gq, gk, bd)


def _prep_odd_kernel(q_ref, k_ref, v_ref, fl_ref, bf_ref, gq_ref, gk_ref, bd_ref,
                     qo_ref, kto_ref, vo_ref, carry_ref):
    @pl.when(pl.program_id(1) == 0)
    def _():
        carry_ref[...] = jnp.zeros_like(carry_ref)

    q = _head_rms_norm(q_ref[0], gq_ref[...], bd_ref) * QK_SCALE_LOG2
    k = _head_rms_norm(k_ref[0], gk_ref[...], bd_ref)
    v = v_ref[0]
    log_f = -_softplus(-(fl_ref[0] + bf_ref[...]))
    c = _cumsum_rows(log_f) + carry_ref[...]
    carry_ref[...] = c[c.shape[0] - 1:, :]
    c2 = c * LOG2E
    lane = lax.broadcasted_iota(jnp.int32, (c.shape[0], LANES), 1)
    for h in range(N_HEADS):
        p, e = divmod(h, 2)
        pick = lambda t: t[:, p * LANES:(p + 1) * LANES] if e == 0 else pltpu.roll(t[:, p * LANES:(p + 1) * LANES], HEAD_DIM, 1)
        cb = jnp.broadcast_to(c2[:, h:h + 1], (c.shape[0], LANES))
        hi = cb.astype(BF16).astype(F32)
        mid = (cb - hi).astype(BF16).astype(F32)
        lo = cb - hi - mid
        q_bias = jnp.where(lane == HEAD_DIM, hi, jnp.where(lane == HEAD_DIM + 1, mid, jnp.where(
            lane == HEAD_DIM + 2, lo, jnp.where(lane < HEAD_DIM + 6, 1.0, 0.0))))
        k_bias = jnp.where(lane < HEAD_DIM + 3, 1.0, jnp.where(lane == HEAD_DIM + 3, -hi, jnp.where(
            lane == HEAD_DIM + 4, -mid, jnp.where(lane == HEAD_DIM + 5, -lo, 0.0))))
        qo_ref[0, h] = jnp.where(lane < HEAD_DIM, pick(q), q_bias).astype(BF16)
        kto_ref[0, h] = jnp.where(lane < HEAD_DIM, pick(k), k_bias).T.astype(BF16)
        vo_ref[0, h] = jnp.where(lane < HEAD_DIM, pick(v), 1.0).astype(BF16)


def _prep_odd(q, k, v, fl, bf, gq, gk, bd, ts=512):
    b, s, w = q.shape
    assert s % ts == 0
    tile = pl.BlockSpec((1, ts, w), lambda bi, si: (bi, si, 0))
    narrow = pl.BlockSpec((1, ts, LANES), lambda bi, si: (bi, si, 0))
    heads = pl.BlockSpec((1, N_HEADS, ts, LANES), lambda bi, si: (bi, 0, si, 0))
    const = lambda shape: _resident(shape, lambda bi, si: (0, 0))
    return pl.pallas_call(
        _prep_odd_kernel,
        grid=(b, s // ts),
        in_specs=[tile, tile, tile, narrow, const((1, LANES)), const((1, w)), const((1, w)), const((w, w))],
        out_specs=[heads, pl.BlockSpec((1, N_HEADS, LANES, ts), lambda bi, si: (bi, 0, 0, si)), heads],
        out_shape=[jax.ShapeDtypeStruct((b, N_HEADS, s, LANES), BF16),
                   jax.ShapeDtypeStruct((b, N_HEADS, LANES, s), BF16),
                   jax.ShapeDtypeStruct((b, N_HEADS, s, LANES), BF16)],
        scratch_shapes=[pltpu.VMEM((1, LANES), F32)],
        compiler_params=_params("parallel", "arbitrary"),
        name="prep_odd",
    )(q, k, v, fl, bf, gq, gk, bd)


def _head_halves(x_pair):
    lane = lax.broadcasted_iota(jnp.int32, x_pair.shape, 1)
    zero = jnp.zeros_like(x_pair)
    return jnp.where(lane < HEAD_DIM, x_pair, zero), jnp.where(lane >= HEAD_DIM, x_pair, zero)


def _fold_lanes(x, op):
    out = x[:, 0:LANES]
    for c in range(1, x.shape[1] // LANES):
        out = op(out, x[:, c * LANES:(c + 1) * LANES])
    return out


def _online_softmax_block(scores, values, m_ref, acc_ref):
    probs, alphas = [], []
    for h, s in enumerate(scores):
        m_old = m_ref[h]
        m_new = jnp.maximum(m_old, jnp.max(_fold_lanes(s, jnp.maximum), axis=1, keepdims=True).astype(F32))
        p = jnp.exp2(s - jnp.tile(m_new.astype(s.dtype), (1, s.shape[1] // LANES)))
        m_ref[h] = m_new
        probs.append(p.astype(BF16))
        alphas.append(jnp.exp2(m_old - m_new))
    for h, (p, alpha) in enumerate(zip(probs, alphas)):
        acc_ref[h] = alpha * acc_ref[h] + jnp.dot(p, values[h], preferred_element_type=F32)


def _to_ordinal(x):
    bits = lax.bitcast_convert_type(x, jnp.int32)
    return jnp.where(bits < 0, bits ^ jnp.int32(0x7FFFFFFF), bits)


def _from_ordinal(o):
    return lax.bitcast_convert_type(jnp.where(o < 0, o ^ jnp.int32(0x7FFFFFFF), o), F32)


def _ordinal_midpoint(lo, hi):
    a, b = _to_ordinal(lo), _to_ordinal(hi)
    return _from_ordinal((a >> 1) + (b >> 1) + (a & b & 1))


def _next_float_up(x):
    return _from_ordinal(_to_ordinal(x) + 1)


def _normalise(acc):
    return acc / pltpu.roll(acc, HEAD_DIM, 1)


def _dsa_index_kernel(qi_ref, kw_ref, kit_ref, mask_ref, sc_ref, lo_ref, hi_ref, *, tq, tk, topk, max_iters):
    t0 = pl.program_id(1) * tq
    nb = (t0 + tq + tk - 1) // tk
    row_chunk = (t0 + lax.broadcasted_iota(jnp.int32, (tq, 1), 0)) >> CHUNK_SHIFT
    kf = float(topk)

    def cols(j):
        return pl.ds(pl.multiple_of(j * tk, tk), tk)

    qi_heads = []
    for p in range(N_PAIRS):
        qi_heads.extend(_head_halves(qi_ref[0, :, p * LANES:(p + 1) * LANES]))
    kw = kw_ref[0]

    def score_block(j, carry):
        kit = kit_ref[0, :, cols(j)]
        score = jnp.zeros((tq, tk), F32)
        for h in range(N_HEADS):
            rel = jnp.maximum(jnp.dot(qi_heads[h], kit, preferred_element_type=F32), 0.0)
            score = score + kw[:, HEAD_DIM + h:HEAD_DIM + h + 1] * rel
        key_chunk = (j * tk + lax.broadcasted_iota(jnp.int32, (1, tk), 1)) >> CHUNK_SHIFT
        visible = key_chunk <= row_chunk
        sc_ref[:, cols(j)] = jnp.where(visible, score, -jnp.inf)
        lo_ref[...] = jnp.minimum(lo_ref[...], _fold_lanes(jnp.where(visible, score, jnp.inf), jnp.minimum))
        hi_ref[...] = jnp.maximum(hi_ref[...], _fold_lanes(jnp.where(visible, score, -jnp.inf), jnp.maximum))
        return carry

    lo_ref[...] = jnp.full_like(lo_ref, jnp.inf)
    hi_ref[...] = jnp.full_like(hi_ref, -jnp.inf)
    lax.fori_loop(0, nb, score_block, 0)

    reps = tk // LANES

    def to_col(row):
        return jnp.broadcast_to(row, (LANES, tq)).T

    def to_row(wide, reduce):
        return reduce(wide.T, axis=0, keepdims=True)

    def count(row, strict=False, wanted=None):
        col = to_col(row)
        parts = []
        for g, r in enumerate(range(0, tq, ROW_GROUP)):
            bound = jnp.tile(col[r:r + ROW_GROUP], (1, reps))

            def body(j, cnt):
                blk = sc_ref[r:r + ROW_GROUP, cols(j)]
                hit = blk > bound if strict else blk >= bound
                return cnt + _fold_lanes(jnp.where(hit, 1.0, 0.0), jnp.add)

            blocks = nb if wanted is None else jnp.where(wanted[g] > 0.0, nb, 0)
            parts.append(lax.fori_loop(0, blocks, body, jnp.zeros((ROW_GROUP, LANES), F32)))
        return to_row(jnp.concatenate(parts, axis=0), jnp.sum)

    lo0 = to_row(lo_ref[...], jnp.min)
    hi0 = _next_float_up(to_row(hi_ref[...], jnp.max))
    row_index = lax.broadcasted_iota(jnp.int32, (1, tq), 1)
    group_of_row = row_index >> ROW_GROUP_SHIFT
    n_visible = (((t0 + row_index) >> CHUNK_SHIFT) + 1) * CHUNK
    short_row = n_visible <= topk
    log_k = jnp.log(kf + 0.5)

    def open_groups(lo, hi, c_lo):
        finished = jnp.logical_or(jnp.logical_or(c_lo == kf, short_row),
                                  _to_ordinal(_ordinal_midpoint(lo, hi)) == _to_ordinal(lo))
        still = jnp.where(finished, 0.0, 1.0)
        return tuple(jnp.max(jnp.where(group_of_row == g, still, 0.0)) for g in range(tq // ROW_GROUP))

    def search_cond(state):
        return jnp.logical_and(state[0] < max_iters, functools.reduce(jnp.maximum, state[-1]) > 0.0)

    def search_body(state):
        it, lo, hi, c_lo, f_lo, f_hi, side, wanted = state
        guess = lo + (hi - lo) * jnp.clip(f_lo / (f_lo - f_hi), 1.0 / 64, 63.0 / 64)
        inside = jnp.logical_and(guess > lo, guess < hi)
        half = _ordinal_midpoint(lo, hi)
        mid = jnp.where(it % 3 == 2, half, jnp.where(inside, guess, half))
        c_mid = count(mid, wanted=wanted)
        counted = sum(jnp.where(group_of_row == g, w, 0.0) for g, w in enumerate(wanted)) > 0.0
        enough = jnp.logical_and(counted, c_mid >= kf)
        short = jnp.logical_and(counted, c_mid < kf)
        f_mid = jnp.log(jnp.maximum(c_mid, 0.5)) - log_k
        f_hi = jnp.where(enough, jnp.where(side > 0.0, f_hi * 0.5, f_hi), jnp.where(short, f_mid, f_hi))
        f_lo = jnp.where(enough, f_mid, jnp.where(jnp.logical_and(short, side < 0.0), f_lo * 0.5, f_lo))
        lo, c_lo = jnp.where(enough, mid, lo), jnp.where(enough, c_mid, c_lo)
        hi = jnp.where(short, mid, hi)
        side = jnp.where(enough, 1.0, jnp.where(short, -1.0, side))
        return it + 1, lo, hi, c_lo, f_lo, f_hi, side, open_groups(lo, hi, c_lo)

    c_lo0 = n_visible.astype(F32)
    state0 = (jnp.int32(0), lo0, hi0, c_lo0, jnp.log(c_lo0) - log_k, jnp.log(0.5) - log_k + jnp.zeros_like(lo0),
              jnp.zeros_like(lo0), open_groups(lo0, hi0, c_lo0))
    _, lo, _, n_ge, _, _, _, _ = lax.while_loop(search_cond, search_body, state0)
    thr_row = jnp.where(short_row, jnp.finfo(F32).min, lo)
    thr = jnp.tile(to_col(thr_row), (1, reps))

    has_extra = jnp.max(jnp.where(jnp.logical_and(n_ge > kf, jnp.logical_not(short_row)), 1.0, 0.0))

    @pl.when(has_extra > 0.0)
    def _():
        keep = to_col(kf - count(thr_row, strict=True))[:, 0:1]
        upper = (lax.broadcasted_iota(jnp.int32, (tk, tk), 0)
                 <= lax.broadcasted_iota(jnp.int32, (tk, tk), 1))
        prefix = jnp.where(upper, 1.0, 0.0).astype(BF16)

        def body(j, seen):
            blk = sc_ref[:, cols(j)]
            tie = jnp.where(blk == thr, 1.0, 0.0)
            rank = seen + jnp.dot(tie.astype(BF16), prefix, preferred_element_type=F32)
            drop = jnp.where(rank > keep, tie, 0.0)
            sc_ref[:, cols(j)] = jnp.where(drop > 0.0, -jnp.inf, blk)
            return seen + jnp.sum(tie, axis=1, keepdims=True)

        lax.fori_loop(0, nb, body, jnp.zeros((tq, 1), F32))

    def mask_block(j, carry):
        mask_ref[0, :, cols(j)] = jnp.where(sc_ref[:, cols(j)] >= thr, 1, 0).astype(mask_ref.dtype)
        return carry

    def clear_block(j, carry):
        mask_ref[0, :, cols(j)] = jnp.zeros((tq, tk), mask_ref.dtype)
        return carry

    lax.fori_loop(0, nb, mask_block, 0)
    lax.fori_loop(nb, mask_ref.shape[2] // tk, clear_block, 0)


def _dsa_index(qi, kw, kit, tq, tk):
    b, s, w = qi.shape
    assert s % tk == 0 and s % tq == 0 and tq % CHUNK == 0
    topk = min(TOPK_MAX, s // 4)
    return pl.pallas_call(
        functools.partial(_dsa_index_kernel, tq=tq, tk=tk, topk=topk, max_iters=640),
        grid=(b, s // tq),
        in_specs=[pl.BlockSpec((1, tq, w), lambda bi, i: (bi, i, 0)),
                  pl.BlockSpec((1, tq, LANES), lambda bi, i: (bi, i, 0)),
                  _resident((1, LANES, s), lambda bi, i: (bi, 0, 0))],
        out_specs=_resident((1, tq, s), lambda bi, i: (bi, i, 0)),
        out_shape=jax.ShapeDtypeStruct((b, s, s), jnp.int8),
        scratch_shapes=[pltpu.VMEM((tq, s), F32), pltpu.VMEM((tq, LANES), F32), pltpu.VMEM((tq, LANES), F32)],
        compiler_params=_params("parallel", "arbitrary"),
        name="dsa_index",
    )(qi, kw, kit)


def _dsa_attend_kernel(q_ref, kt_ref, v_ref, mask_ref, o_ref, m_ref, acc_ref, *, tq, tk):
    nb = (pl.program_id(1) * tq + tq + tk - 1) // tk
    m_ref[...] = jnp.full_like(m_ref, MASKED)
    acc_ref[...] = jnp.zeros_like(acc_ref)
    q_heads = []
    for p in range(N_PAIRS):
        q_heads.extend(_head_halves(q_ref[0, :, p * LANES:(p + 1) * LANES]))
    first_half = lax.broadcasted_iota(jnp.int32, (tk, LANES), 1) < HEAD_DIM
    one = jnp.ones((), BF16)

    def attend_block(j, carry):
        cols = pl.ds(pl.multiple_of(j * tk, tk), tk)
        bias = jnp.where(mask_ref[0, :, cols].astype(jnp.int32) > 0, 0.0, MASKED).astype(BF16)
        scores, values = [], []
        for h in range(N_HEADS):
            p, e = divmod(h, 2)
            kt = kt_ref[0, p * LANES:(p + 1) * LANES, cols]
            scores.append(jnp.dot(q_heads[h], kt, preferred_element_type=F32).astype(BF16) + bias)
            v_pair = v_ref[0, cols, p * LANES:(p + 1) * LANES]
            values.append(jnp.where(first_half, v_pair, one) if e == 0 else jnp.where(first_half, one, v_pair))
        _online_softmax_block(scores, values, m_ref, acc_ref)
        return carry

    lax.fori_loop(0, nb, attend_block, 0)
    lane = lax.broadcasted_iota(jnp.int32, (tq, LANES), 1)
    for p in range(N_PAIRS):
        o_ref[0, :, p * LANES:(p + 1) * LANES] = jnp.where(
            lane < HEAD_DIM, _normalise(acc_ref[2 * p]), _normalise(acc_ref[2 * p + 1])).astype(o_ref.dtype)


def _dsa_attend(q, kt, v, mask, tq, tk):
    b, s, w = q.shape
    assert s % tk == 0 and s % tq == 0
    qtile = pl.BlockSpec((1, tq, w), lambda bi, i: (bi, i, 0))
    return pl.pallas_call(
        functools.partial(_dsa_attend_kernel, tq=tq, tk=tk),
        grid=(b, s // tq),
        in_specs=[qtile,
                  _resident((1, w, s), lambda bi, i: (bi, 0, 0)),
                  _resident((1, s, w), lambda bi, i: (bi, 0, 0)),
                  pl.BlockSpec((1, tq, s), lambda bi, i: (bi, i, 0))],
        out_specs=qtile,
        out_shape=jax.ShapeDtypeStruct((b, s, w), BF16),
        scratch_shapes=[pltpu.VMEM((N_HEADS, tq, LANES), F32), pltpu.VMEM((N_HEADS, tq, LANES), F32)],
        compiler_params=_params("parallel", "arbitrary"),
        name="dsa_attend",
    )(q, kt, v, mask)


def _dsa(qi, kw, kit, q, kt, v, tq_index=512, tq_attend=256, tk=512):
    s = q.shape[1]
    mask = _dsa_index(qi, kw, kit, min(tq_index, s), min(tk, s))
    return _dsa_attend(q, kt, v, mask, min(tq_attend, s), min(tk, s))


def _fox_kernel(bound_ref, q_ref, kt_ref, v_ref, o_ref, m_ref, acc_ref, ckey_ref, *, tq, heads):
    i = pl.program_id(2)
    m_ref[...] = jnp.full_like(m_ref, MASKED)
    acc_ref[...] = jnp.zeros_like(acc_ref)
    causal = (lax.broadcasted_iota(jnp.int32, (tq, tq), 1) <= lax.broadcasted_iota(jnp.int32, (tq, tq), 0))

    @pl.when(i == 0)
    def _():
        for e in range(heads):
            tail = kt_ref[0, e, HEAD_DIM:HEAD_DIM + 16, :].astype(F32)
            row = lax.broadcasted_iota(jnp.int32, tail.shape, 0)
            ckey_ref[e] = -jnp.sum(jnp.where(jnp.logical_and(row >= 3, row < 6), tail, 0.0), axis=0, keepdims=True)

    pos = lax.broadcasted_iota(jnp.int32, (1, ckey_ref.shape[2]), 1)
    block_end = jnp.logical_and((pos & (tq - 1)) == tq - 1, pos < i * tq)
    lane = lax.broadcasted_iota(jnp.int32, (tq, LANES), 1)
    first = None
    for e in range(heads):
        c_first = ckey_ref[e, :, pl.ds(pl.multiple_of(i * tq, tq), LANES)][:, 0:1]
        dead = jnp.logical_and(block_end, ckey_ref[e] - c_first > 2.0 * bound_ref[0] + UNDERFLOW_BITS)
        n_dead = jnp.sum(jnp.where(dead, 1, 0))
        first = n_dead if first is None else jnp.minimum(first, n_dead)

    def block(j, diagonal):
        cols = pl.ds(pl.multiple_of(j * tq, tq), tq)
        scores = []
        for e in range(heads):
            s = jnp.dot(q_ref[0, e], kt_ref[0, e, :, cols], preferred_element_type=F32)
            scores.append(jnp.where(causal, s, MASKED) if diagonal else s)
        _online_softmax_block(scores, [v_ref[0, e, cols, :] for e in range(heads)], m_ref, acc_ref)

    def body(j, carry):
        block(j, False)
        return carry

    lax.fori_loop(first, i, body, 0)
    block(i, True)
    for p in range(heads // 2):
        a0, a1 = acc_ref[2 * p], acc_ref[2 * p + 1]
        o_ref[0, :, p * LANES:(p + 1) * LANES] = jnp.where(
            lane < HEAD_DIM, a0 / pltpu.roll(a0, HEAD_DIM, 1), pltpu.roll(a1, HEAD_DIM, 1) / a1).astype(o_ref.dtype)


def _fox(qk_bound, q, kt, v, tq=512, heads=4):
    b, nh, s, _ = q.shape
    assert s % tq == 0 and tq & (tq - 1) == 0 and nh == N_HEADS and nh % heads == 0 and heads % 2 == 0
    out_w = heads * HEAD_DIM
    return pl.pallas_call(
        functools.partial(_fox_kernel, tq=tq, heads=heads),
        grid=(b, nh // heads, s // tq),
        in_specs=[pl.BlockSpec(memory_space=pltpu.SMEM),
                  pl.BlockSpec((1, heads, tq, LANES), lambda bi, g, i: (bi, g, i, 0)),
                  _resident((1, heads, LANES, s), lambda bi, g, i: (bi, g, 0, 0)),
                  _resident((1, heads, s, LANES), lambda bi, g, i: (bi, g, 0, 0))],
        out_specs=pl.BlockSpec((1, tq, out_w), lambda bi, g, i: (bi, i, g)),
        out_shape=jax.ShapeDtypeStruct((b, s, N_HEADS * HEAD_DIM), BF16),
        scratch_shapes=[pltpu.VMEM((heads, tq, LANES), F32), pltpu.VMEM((heads, tq, LANES), F32),
                        pltpu.VMEM((heads, 1, s), F32)],
        compiler_params=_params("parallel", "parallel", "arbitrary"),
        name="fox",
    )(qk_bound, q, kt, v)


def _convmod_kernel(u_ref, cw_ref, cb_ref, g_ref, b_ref, o_ref, xbuf_ref, *, ts, width, halo):
    @pl.when(pl.program_id(1) == 0)
    def _():
        xbuf_ref[0:halo, :] = jnp.zeros((halo, width), F32)

    x = u_ref[0, :, 0:width] * jax.nn.sigmoid(u_ref[0, :, width:2 * width])
    xbuf_ref[halo:halo + ts, :] = x
    base = halo - (CONV_KERNEL - 1)
    xv = xbuf_ref[...]
    y = cb_ref[...]
    for residue in range(SUBLANES):
        shifted = xv if residue == 0 else pltpu.roll(xv, halo + ts - residue, 0)
        for j in range(CONV_KERNEL):
            if (base + j) % SUBLANES == residue:
                start = base + j - residue
                y = y + cw_ref[j:j + 1, :] * shifted[start:start + ts]
    xbuf_ref[0:halo, :] = x[ts - halo:ts, :]
    mu = jnp.mean(y, axis=-1, keepdims=True)
    var = jnp.mean(jnp.square(y - mu), axis=-1, keepdims=True)
    z = (y - mu) * lax.rsqrt(var + NORM_EPS) * g_ref[...] + b_ref[...]
    o_ref[0] = (z * jax.nn.sigmoid(z)).astype(o_ref.dtype)


def _convmod(u, conv_w, conv_b, ln_g, ln_b, ts=512, halo=32):
    b, s, w2 = u.shape
    width = w2 // 2
    assert s % ts == 0 and halo >= CONV_KERNEL - 1
    row = lambda v: v.reshape(1, width)
    const = lambda shape: _resident(shape, lambda bi, si: (0, 0))
    return pl.pallas_call(
        functools.partial(_convmod_kernel, ts=ts, width=width, halo=halo),
        grid=(b, s // ts),
        in_specs=[pl.BlockSpec((1, ts, w2), lambda bi, si: (bi, si, 0)),
                  const((CONV_KERNEL, width)), const((1, width)), const((1, width)), const((1, width))],
        out_specs=pl.BlockSpec((1, ts, width), lambda bi, si: (bi, si, 0)),
        out_shape=jax.ShapeDtypeStruct((b, s, width), BF16),
        scratch_shapes=[pltpu.VMEM((ts + halo, width), F32)],
        compiler_params=_params("parallel", "arbitrary"),
        name="conv_module",
    )(u, conv_w, row(conv_b), row(ln_g), row(ln_b))


def _out_ffn_kernel(x_ref, ya_ref, yb_ref, wo_ref, g_ref, wgu_ref, wd_ref, o_ref, *, hidden, th):
    half = ya_ref.shape[1]
    x = x_ref[...] + (jnp.dot(ya_ref[...], wo_ref[0:half, :], preferred_element_type=F32)
                      + jnp.dot(yb_ref[...], wo_ref[half:2 * half, :], preferred_element_type=F32))
    ms = jnp.mean(x * x, axis=-1, keepdims=True)
    h = (x * lax.rsqrt(ms + NORM_EPS) * g_ref[...]).astype(BF16)
    y = x
    for c in range(0, hidden, th):
        gate = jnp.dot(h, wgu_ref[:, c:c + th], preferred_element_type=F32)
        up = jnp.dot(h, wgu_ref[:, hidden + c:hidden + c + th], preferred_element_type=F32)
        act = (gate * jax.nn.sigmoid(gate) * up).astype(BF16)
        y = y + jnp.dot(act, wd_ref[c:c + th, :], preferred_element_type=F32)
    o_ref[...] = y


def _out_ffn(x2, ya, yb, w_out, g, w_gu, w_down, tm=512, th=256):
    n, d = x2.shape
    half = ya.shape[1]
    hidden = w_down.shape[0]
    assert n % tm == 0 and hidden % th == 0
    const = lambda shape: _resident(shape, lambda i: (0, 0))
    return pl.pallas_call(
        functools.partial(_out_ffn_kernel, hidden=hidden, th=th),
        grid=(n // tm,),
        in_specs=[pl.BlockSpec((tm, d), lambda i: (i, 0)),
                  pl.BlockSpec((tm, half), lambda i: (i, 0)), pl.BlockSpec((tm, half), lambda i: (i, 0)),
                  const(w_out.shape), const((1, d)), const(w_gu.shape), const(w_down.shape)],
        out_specs=pl.BlockSpec((tm, d), lambda i: (i, 0)),
        out_shape=jax.ShapeDtypeStruct((n, d), F32),
        compiler_params=_params("parallel"),
        name="out_ffn",
    )(x2, ya, yb, w_out, g.reshape(1, d), w_gu, w_down)


def _rope_tables(s, width):
    inv = ROPE_THETA ** (-jnp.arange(0, HEAD_DIM, 2, dtype=F32) / HEAD_DIM)
    ang = jnp.arange(s, dtype=F32)[:, None] * inv[None, :]
    cos, sin = jnp.cos(ang), jnp.sin(ang)
    reps = width // HEAD_DIM
    return (jnp.tile(jnp.concatenate([cos, cos], axis=-1), (1, reps)),
            jnp.tile(jnp.concatenate([-sin, sin], axis=-1), (1, reps)))


def _block_diag(blocks):
    n, d, _ = blocks.shape
    eye = jnp.eye(n, dtype=blocks.dtype)
    return jnp.einsum('nde,nm->ndme', blocks, eye).reshape(n * d, n * d)


def _pad_cols(w, total):
    return jnp.pad(w, ((0, 0), (0, total - w.shape[1])))


def kernel(x, norm_mix, norm_ffn,
           ev_w_in, ev_conv_w, ev_conv_b, ev_w_r, ev_b_r, ev_w_i, ev_b_i, ev_lam,
           ev_q_norm, ev_k_norm, ev_w_out,
           od_w_in, od_b_f, od_q_norm, od_k_norm, od_conv_w, od_conv_b, od_ln_g, od_ln_b,
           od_w_out, ffn_w_gu, ffn_w_down):
    b, s, d = x.shape
    depth = norm_mix.shape[0]
    w = N_HEADS * HEAD_DIM
    cos, sin = _rope_tables(s, w)
    head_ones = _block_diag(jnp.ones((N_HEADS, HEAD_DIM, HEAD_DIM), BF16))
    tile_gain = lambda g: jnp.tile(g, N_HEADS).reshape(1, w)
    x2 = x.reshape(b * s, d)
    seq = lambda t: t.reshape(b, s, t.shape[-1])
    for l in range(depth):
        j = l // 2
        if l % 2 == 0:
            w_in = _pad_cols(ev_w_in[j], 2 * w + 4 * w + LANES).astype(BF16)
            xg, q, k, v, qi, kw = _norm_matmul(x2, norm_mix[l], w_in, (2 * w, w, w, w, w, LANES))
            ya = _lru(seq(xg), ev_conv_w[j], ev_conv_b[j],
                      _block_diag(ev_w_r[j]).astype(BF16), ev_b_r[j],
                      _block_diag(ev_w_i[j]).astype(BF16), ev_b_i[j], ev_lam[j])
            qh, kt, vh, qih, kit = _prep_even(seq(q), seq(k), seq(v), seq(qi), seq(kw), cos, sin,
                                              tile_gain(ev_q_norm[j]), tile_gain(ev_k_norm[j]), head_ones)
            yb = _dsa(qih, seq(kw), kit, qh, kt, vh)
            w_out = ev_w_out[j]
        else:
            wi = od_w_in[j]
            w_in = jnp.concatenate([wi[:, 0:3 * w], wi[:, 3 * w + N_HEADS:],
                                    _pad_cols(wi[:, 3 * w:3 * w + N_HEADS], LANES)], axis=1).astype(BF16)
            q, k, v, u, fl = _norm_matmul(x2, norm_mix[l], w_in, (w, w, w, 2 * w, LANES))
            qh, kt, vh = _prep_odd(seq(q), seq(k), seq(v), seq(fl),
                                   _pad_cols(od_b_f[j].reshape(1, N_HEADS), LANES),
                                   tile_gain(od_q_norm[j]), tile_gain(od_k_norm[j]), head_ones)
            qk_bound = (HEAD_DIM * QK_SCALE_LOG2 * 1.01 * jnp.max(jnp.abs(od_q_norm[j]))
                        * jnp.max(jnp.abs(od_k_norm[j])) + 1.0).reshape(1).astype(F32)
            ya = _fox(qk_bound, qh, kt, vh, tq=min(512, s))
            yb = _convmod(seq(u), od_conv_w[j], od_conv_b[j], od_ln_g[j], od_ln_b[j])
            w_out = od_w_out[j]
        x2 = _out_ffn(x2, ya.reshape(b * s, w), yb.reshape(b * s, w), w_out.astype(BF16),
                      norm_ffn[l], ffn_w_gu[l].astype(BF16), ffn_w_down[l].astype(BF16))
    return x2.reshape(b, s, d)
```
